```python
import math
import jax
import jax.numpy as jnp
from jax import lax
import numpy as np

D_MODEL = 1024
BATCH = 32
SEQ = 2048
DEPTH = 4
DEC_BATCH = 32
DEC_SEQ = 32
PAST_LEN = 1024

CHUNK = 64
ROPE_THETA = 10000.0
EPS = 1e-6
NEG_BIG = -1e30
LB_FLOOR = 1e-20
H_A = 6
DA = 32
DV_A = 64
H_B = 5
D_B = 64
H_I = 4
D_I = 64
TOPK_MAX = 256
H_C = 5
DK_C = 64
DV_C = 64
MIX = H_A * DV_A + H_B * D_B + H_C * DV_C
D_FF = 2816
CONV_W = 3
QB_A = 128
QB_B = CHUNK
IN_SIZES = (H_A * 2 * DA, H_A * 2 * DA, H_A * DV_A,
            H_B * D_B, H_B * D_B, H_B * D_B,
            H_I * D_I, D_I, H_I,
            H_C * DK_C, H_C * DK_C, H_C * DV_C, H_C * DV_C)
W_IN = int(sum(IN_SIZES))
IN_OFFSETS = tuple(int(o) for o in np.cumsum(IN_SIZES)[:-1])

kernel_name = "hymba_style_diff_dsa_hgrn2_streaming_step"

F32 = jnp.float32


def rmsnorm(x, gain):
    x32 = x.astype(F32)
    y = x32 * lax.rsqrt(jnp.mean(x32 * x32, axis=-1, keepdims=True) + EPS)
    return (y * gain.astype(F32)).astype(x.dtype)


def rope_tables(pos, dim):
    inv_freq = ROPE_THETA ** (-jnp.arange(0, dim, 2, dtype=F32) / dim)
    ang = pos.astype(F32)[:, None] * inv_freq[None, :]
    return jnp.cos(ang), jnp.sin(ang)


def apply_rope(x, cos, sin):
    shape = (1, cos.shape[0]) + (1,) * (x.ndim - 3) + (cos.shape[1],)
    c = cos.reshape(shape)
    s = sin.reshape(shape)
    x1, x2 = jnp.split(x.astype(F32), 2, axis=-1)
    return jnp.concatenate([x1 * c - x2 * s, x2 * c + x1 * s], axis=-1).astype(x.dtype)


def to_blocks(x, qb):
    b, t = x.shape[:2]
    return jnp.moveaxis(x.reshape((b, t // qb, qb) + x.shape[2:]), 1, 0)


def from_blocks(y):
    y = jnp.moveaxis(y, 0, 1)
    return y.reshape((y.shape[0], y.shape[1] * y.shape[2]) + y.shape[3:])


def mixer_inputs(xn, w, pos, lb):
    b, t, _ = xn.shape
    u = xn @ w
    aq, ak, av, bq, bk, bv, iq, ik, iw, cq, cf, ci, cg = jnp.split(u, IN_OFFSETS, axis=-1)
    cos_a, sin_a = rope_tables(pos, DA)
    cos_b, sin_b = rope_tables(pos, D_B)
    cos_i, sin_i = rope_tables(pos, D_I)
    aq = apply_rope(aq.reshape(b, t, H_A, 2, DA), cos_a, sin_a)
    ak = apply_rope(ak.reshape(b, t, H_A, 2, DA), cos_a, sin_a)
    av = av.reshape(b, t, H_A, DV_A)
    bq = apply_rope(bq.reshape(b, t, H_B, D_B), cos_b, sin_b)
    bk = apply_rope(bk.reshape(b, t, H_B, D_B), cos_b, sin_b)
    bv = bv.reshape(b, t, H_B, D_B)
    iq = apply_rope(iq.reshape(b, t, H_I, D_I), cos_i, sin_i)
    ik = apply_rope(ik, cos_i, sin_i)
    iw = iw * (H_I ** -0.5)
    z = cf.astype(F32).reshape(b, t, H_C, DK_C)
    lbh = lb.reshape(H_C, DK_C)
    cgl = jnp.logaddexp(jnp.log(jnp.maximum(lbh, LB_FLOOR)), jnp.log1p(-lbh) + jax.nn.log_sigmoid(z))
    ck = (1.0 - lbh) * jax.nn.sigmoid(-z)
    cqf = jax.nn.silu(cq.astype(F32)).reshape(b, t, H_C, DK_C)
    cv = ci.astype(F32).reshape(b, t, H_C, DV_C)
    cgate = cg.reshape(b, t, H_C, DV_C)
    return aq, ak, av, bq, bk, bv, iq, ik, iw, cqf, ck, cgl, cv, cgate


def diff_attend(q, k, v, mask, lam):
    s = jnp.einsum('bqhmd,bkhmd->bhmqk', q, k, preferred_element_type=F32) * (DA ** -0.5)
    if mask is not None:
        s = jnp.where(mask, s, NEG_BIG)
    p = jax.nn.softmax(s, axis=-1)
    a = p[:, :, 0] - lam * p[:, :, 1]
    return jnp.einsum('bhqk,bkhd->bqhd', a.astype(v.dtype), v)


def diff_attn_prompt(q, k, v, lam):
    t = q.shape[1]
    kchunk = jnp.arange(t) // CHUNK

    def blk(args):
        qb, i = args
        qchunk = (i * QB_A + jnp.arange(QB_A)) // CHUNK
        mask = kchunk[None, :] <= qchunk[:, None]
        return diff_attend(qb, k, v, mask, lam)

    out = lax.map(blk, (to_blocks(q, QB_A), jnp.arange(t // QB_A)))
    return from_blocks(out)


def dsa_attend(q, qi, wi, k_all, v_all, ki_all, mask, n_sel):
    dots = jnp.einsum('bqhd,bsd->bqhs', qi, ki_all, preferred_element_type=F32) * (D_I ** -0.5)
    score = jnp.einsum('bqh,bqhs->bqs', wi.astype(F32), jax.nn.relu(dots))
    if mask is not None:
        score = jnp.where(mask, score, NEG_BIG)
    _, sel = lax.top_k(score, n_sel)
    gather = jax.vmap(lambda a, i: a[i])
    k_sel = gather(k_all, sel)
    v_sel = gather(v_all, sel)
    s = jnp.einsum('bqhd,bqkhd->bhqk', q, k_sel, preferred_element_type=F32) * (D_B ** -0.5)
    if mask is not None:
        valid = mask[jnp.arange(q.shape[1])[None, :, None], sel]
        s = jnp.where(valid[:, None], s, NEG_BIG)
    p = jax.nn.softmax(s, axis=-1)
    return jnp.einsum('bhqk,bqkhd->bqhd', p.astype(v_sel.dtype), v_sel)


def dsa_prompt(q, qi, wi, k, v, ki):
    t = q.shape[1]
    n_sel = min(TOPK_MAX, t // 4)
    kchunk = jnp.arange(t) // CHUNK

    def blk(args):
        qb, qib, wib, i = args
        qchunk = (i * QB_B + jnp.arange(QB_B)) // CHUNK
        mask = kchunk[None, :] <= qchunk[:, None]
        return dsa_attend(qb, qib, wib, k, v, ki, mask, n_sel)

    out = lax.map(blk, (to_blocks(q, QB_B), to_blocks(qi, QB_B), to_blocks(wi, QB_B),
                        jnp.arange(t // QB_B)))
    return from_blocks(out)


def hgrn2_chunk(S, q, k, g, v):
    c = q.shape[1]
    bcum = jnp.cumsum(g, axis=1)
    causal = jnp.tril(jnp.ones((c, c), dtype=bool))[None, :, :, None, None]
    diff = bcum[:, :, None] - bcum[:, None, :]
    decay = jnp.where(causal, jnp.exp(jnp.where(causal, diff, 0.0)), 0.0)
    att = jnp.einsum('bthk,btshk,bshk->bhts', q, decay, k)
    o = (jnp.einsum('bhts,bshv->bthv', att, v)
         + jnp.einsum('bthk,bhkv->bthv', q * jnp.exp(bcum), S))
    b_last = bcum[:, -1]
    S_new = (jnp.exp(b_last)[..., None] * S
             + jnp.einsum('bshk,bshv->bhkv', k * jnp.exp(b_last[:, None] - bcum), v))
    return S_new, o


def hgrn2_prompt(q, k, g, v):
    b = q.shape[0]
    S0 = jnp.zeros((b, H_C, DK_C, DV_C), F32)
    xs = (to_blocks(q, CHUNK), to_blocks(k, CHUNK), to_blocks(g, CHUNK), to_blocks(v, CHUNK))
    S, o = lax.scan(lambda S, a: hgrn2_chunk(S, *a), S0, xs)
    return S, from_blocks(o)


def merge_heads(oa, ob, oc, cgate, lam_init, a_gain, c_gain, w_out):
    b, t = oa.shape[:2]
    oa = rmsnorm(oa, a_gain) * (1.0 - lam_init)
    oc = rmsnorm(oc, c_gain) * jax.nn.silu(cgate)
    cat = jnp.concatenate([oa.reshape(b, t, -1), ob.reshape(b, t, -1), oc.reshape(b, t, -1)], axis=-1)
    return cat @ w_out


def conv_ffn(h, hist, w_up, w_conv, b_conv, w_down):
    a, gate = jnp.split(h @ w_up, 2, axis=-1)
    t = a.shape[1]
    ext = jnp.concatenate([hist.astype(a.dtype), a], axis=1)
    conv = b_conv
    for j in range(CONV_W):
        conv = conv + ext[:, j:j + t] * w_conv[j]
    return (jax.nn.silu(conv) * gate) @ w_down, ext[:, t:]


def setup_inputs(seed: int = 0) -> dict:
    key = jax.random.key(seed)
    ks = jax.random.split(key, 25)

    def nrm(k, shape, scale):
        return scale * jax.random.normal(k, shape, F32)

    return {
        "x_prompt": nrm(ks[0], (BATCH, SEQ, D_MODEL), 1.0),
        "x_sample": nrm(ks[1], (DEC_BATCH, DEC_SEQ, D_MODEL), 1.0),
        "cache_a_k": nrm(ks[2], (DEPTH, DEC_BATCH, PAST_LEN, H_A, 2 * DA), 1.0),
        "cache_a_v": nrm(ks[3], (DEPTH, DEC_BATCH, PAST_LEN, H_A, DV_A), 1.0),
        "cache_b_k": nrm(ks[4], (DEPTH, DEC_BATCH, PAST_LEN, H_B, D_B), 1.0),
        "cache_b_v": nrm(ks[5], (DEPTH, DEC_BATCH, PAST_LEN, H_B, D_B), 1.0),
        "cache_b_kidx": nrm(ks[6], (DEPTH, DEC_BATCH, PAST_LEN, D_I), 1.0),
        "state_c": nrm(ks[7], (DEPTH, DEC_BATCH, H_C, DK_C, DV_C), 0.3),
        "state_ffn_conv": nrm(ks[8], (DEPTH, DEC_BATCH, CONV_W - 1, D_FF), 1.0),
        "norm1": 1.0 + nrm(ks[9], (DEPTH, D_MODEL), 0.02),
        "w_in": nrm(ks[10], (DEPTH, D_MODEL, W_IN), D_MODEL ** -0.5),
        "lam_q1": nrm(ks[11], (DEPTH, DA), 0.1),
        "lam_k1": nrm(ks[12], (DEPTH, DA), 0.1),
        "lam_q2": nrm(ks[13], (DEPTH, DA), 0.1),
        "lam_k2": nrm(ks[14], (DEPTH, DA), 0.1),
        "a_norm": 1.0 + nrm(ks[15], (DEPTH, DV_A), 0.02),
        "c_lower": nrm(ks[16], (DEPTH, H_C * DK_C), 0.1),
        "c_norm": 1.0 + nrm(ks[17], (DEPTH, DV_C), 0.02),
        "w_out": nrm(ks[18], (DEPTH, MIX, D_MODEL), 0.5 * MIX ** -0.5),
        "norm2": 1.0 + nrm(ks[19], (DEPTH, D_MODEL), 0.02),
        "ffn_up": nrm(ks[20], (DEPTH, D_MODEL, 2 * D_FF), D_MODEL ** -0.5),
        "ffn_conv_w": nrm(ks[21], (DEPTH, CONV_W, D_FF), CONV_W ** -0.5),
        "ffn_conv_b": nrm(ks[22], (DEPTH, D_FF), 0.01),
        "ffn_down": nrm(ks[23], (DEPTH, D_FF, D_MODEL), 0.5 * D_FF ** -0.5),
        "final_norm": 1.0 + nrm(ks[24], (D_MODEL,), 0.02),
    }


def reference(x_prompt, x_sample, cache_a_k, cache_a_v, cache_b_k, cache_b_v, cache_b_kidx, state_c,
              state_ffn_conv, norm1, w_in, lam_q1, lam_k1, lam_q2, lam_k2, a_norm, c_lower, c_norm, w_out,
              norm2, ffn_up, ffn_conv_w, ffn_conv_b, ffn_down, final_norm):
    t_p = x_prompt.shape[1]
    past = cache_a_k.shape[2]
    t_s = x_sample.shape[1]
    pos_p = jnp.arange(t_p)
    pos_s = past + jnp.arange(t_s)
    n_sel_s = min(TOPK_MAX, (past + t_s) // 4)
    lb_soft = jax.nn.softmax(c_lower.astype(F32), axis=0)
    lower = jnp.cumsum(lb_soft, axis=0) - lb_soft[0]

    xp, xs = x_prompt, x_sample
    akp, avp, bkp, bvp, bip, scp, fcp = [], [], [], [], [], [], []
    aks, avs, bks, bvs, bis, scs, fcs = [], [], [], [], [], [], []
    for l in range(DEPTH):
        lam_init = 0.8 - 0.6 * math.exp(-0.3 * l)
        lam = (jnp.exp(jnp.sum(lam_q1[l].astype(F32) * lam_k1[l].astype(F32)))
               - jnp.exp(jnp.sum(lam_q2[l].astype(F32) * lam_k2[l].astype(F32))) + lam_init)

        aq, ak, av, bq, bk, bv, iq, ik, iw, cq, ck, cg, cv, cgate = mixer_inputs(
            rmsnorm(xp, norm1[l]), w_in[l], pos_p, lower[l])
        oa = diff_attn_prompt(aq, ak, av, lam)
        ob = dsa_prompt(bq, iq, iw, bk, bv, ik)
        sc, oc = hgrn2_prompt(cq, ck, cg, cv)
        xp = xp + merge_heads(oa, ob, oc.astype(xp.dtype), cgate, lam_init, a_norm[l], c_norm[l], w_out[l])
        hist0 = jnp.zeros((xp.shape[0], CONV_W - 1, D_FF), xp.dtype)
        h, fc = conv_ffn(rmsnorm(xp, norm2[l]), hist0, ffn_up[l], ffn_conv_w[l], ffn_conv_b[l], ffn_down[l])
        xp = xp + h
        b_p = xp.shape[0]
        akp.append(ak.reshape(b_p, t_p, H_A, 2 * DA))
        avp.append(av)
        bkp.append(bk)
        bvp.append(bv)
        bip.append(ik)
        scp.append(sc.astype(xp.dtype))
        fcp.append(fc)

        aq, ak, av, bq, bk, bv, iq, ik, iw, cq, ck, cg, cv, cgate = mixer_inputs(
            rmsnorm(xs, norm1[l]), w_in[l], pos_s, lower[l])
        b_s = xs.shape[0]
        ka = jnp.concatenate([cache_a_k[l].reshape(b_s, past, H_A, 2, DA).astype(ak.dtype), ak], axis=1)
        va = jnp.concatenate([cache_a_v[l].astype(av.dtype), av], axis=1)
        oa = diff_attend(aq, ka, va, None, lam)
        kb = jnp.concatenate([cache_b_k[l].astype(bk.dtype), bk], axis=1)
        vb = jnp.concatenate([cache_b_v[l].astype(bv.dtype), bv], axis=1)
        kib = jnp.concatenate([cache_b_kidx[l].astype(ik.dtype), ik], axis=1)
        ob = dsa_attend(bq, iq, iw, kb, vb, kib, None, n_sel_s)
        sc, oc = hgrn2_chunk(state_c[l].astype(F32), cq, ck, cg, cv)
        xs = xs + merge_heads(oa, ob, oc.astype(xs.dtype), cgate, lam_init, a_norm[l], c_norm[l], w_out[l])
        h, fc = conv_ffn(rmsnorm(xs, norm2[l]), state_ffn_conv[l], ffn_up[l], ffn_conv_w[l], ffn_conv_b[l],
                         ffn_down[l])
        xs = xs + h
        aks.append(ak.reshape(b_s, t_s, H_A, 2 * DA))
        avs.append(av)
        bks.append(bk)
        bvs.append(bv)
        bis.append(ik)
        scs.append(sc.astype(state_c.dtype))
        fcs.append(fc)

    y_prompt = rmsnorm(xp, final_norm)
    y_sample = rmsnorm(xs, final_norm)
    return (y_prompt, y_sample,
            jnp.stack(akp), jnp.stack(avp), jnp.stack(bkp), jnp.stack(bvp), jnp.stack(bip),
            jnp.stack(scp), jnp.stack(fcp),
            jnp.stack(aks), jnp.stack(avs), jnp.stack(bks), jnp.stack(bvs), jnp.stack(bis),
            jnp.stack(scs), jnp.stack(fcs))
```

```python
import functools
import math

import jax
import jax.numpy as jnp
import numpy as np
from jax import lax
from jax.experimental import pallas as pl
from jax.experimental.pallas import tpu as pltpu

F32 = jnp.float32
BF16 = jnp.bfloat16

CHUNK = 64
ROPE_THETA = 10000.0
EPS = 1e-6
NEG_BIG = -1e30
LB_FLOOR = 1e-20
H_A, DA, DV_A = 6, 32, 64
H_B, D_B = 5, 64
H_I, D_I = 4, 64
TOPK_MAX = 256
H_C, DK_C, DV_C = 5, 64, 64
CONV_W = 3

LANES = 128
SUBLANES = 8
VMEM_LIMIT = 56 * 1024 * 1024

WA = H_A * 2 * DA
WB = H_B * D_B
WBP = 384
WIQ = H_I * D_I
NPAIR = WBP // LANES

SEGS = (("aq", WA, WA), ("ak", WA, WA), ("av", WA, WA),
        ("bq", WB, WBP), ("bk", WB, WBP), ("bv", WB, WBP),
        ("iq", WIQ, WIQ), ("ik", D_I, 2 * D_I), ("iw", H_I, LANES),
        ("cq", WB, WBP), ("cf", WB, WBP), ("ci", WB, WBP), ("cg", WB, WBP))
SEG_OFF = {}
_o = 0
for _n, _w, _p in SEGS:
    SEG_OFF[_n] = (_o, _p)
    _o += _p
W_PACK = _o
IN_SIZES = (WA, WA, WA, WB, WB, WB, WIQ, D_I, H_I, WB, WB, WB, WB)

NT_DIMS = (((1,), (1,)), ((), ()))


def _cparams(n_axes):
    return pltpu.CompilerParams(dimension_semantics=("arbitrary",) * n_axes,
                                vmem_limit_bytes=VMEM_LIMIT)


def _const_spec(shape):
    nd = len(shape)
    return pl.BlockSpec(shape, lambda *_: (0,) * nd, pipeline_mode=pl.Buffered(1))


def _dot(a, b):
    return jnp.dot(a, b, preferred_element_type=F32)


def _dot_nt(a, b):
    return lax.dot_general(a, b, NT_DIMS, preferred_element_type=F32)


def _rope(u, cos, sin, half):
    w = u.shape[-1]
    lane = lax.broadcasted_iota(jnp.int32, u.shape, 1)
    first = (lane % (2 * half)) < half
    rot = jnp.where(first, -pltpu.roll(u, w - half, 1), pltpu.roll(u, half, 1))
    return u * cos + rot * sin


def _inproj_kernel(x_ref, g_ref, w_ref, lb_ref, ca_ref, sa_ref, cb_ref, sb_ref,
                   aq_ref, ak_ref, av_ref, bq_ref, bk_ref, bv_ref, iq_ref, ik2_ref, ik_ref, iw_ref,
                   cq_ref, ck_ref, cgl_ref, cv_ref, cg_ref):
    x = x_ref[...]
    ms = jnp.mean(x * x, axis=-1, keepdims=True)
    xn = (x * lax.rsqrt(ms + EPS) * g_ref[...]).astype(BF16)

    def seg(name):
        off, width = SEG_OFF[name]
        return _dot(xn, w_ref[:, off:off + width])

    ca, sa = ca_ref[...], sa_ref[...]
    cb, sb = cb_ref[...], sb_ref[...]
    aq_ref[...] = (_rope(seg("aq"), ca, sa, DA // 2) * (DA ** -0.5)).astype(BF16)
    ak_ref[...] = _rope(seg("ak"), ca, sa, DA // 2)
    av_ref[...] = seg("av")
    bq_ref[...] = (_rope(seg("bq"), cb, sb, D_B // 2) * (D_B ** -0.5)).astype(BF16)
    bk_ref[...] = _rope(seg("bk"), cb, sb, D_B // 2)[:, :WB]
    bv_ref[...] = seg("bv")[:, :WB]
    iq_ref[...] = (_rope(seg("iq"), cb[:, :WIQ], sb[:, :WIQ], D_I // 2) * (D_I ** -0.5)).astype(BF16)
    ik2 = _rope(seg("ik"), cb[:, :2 * D_I], sb[:, :2 * D_I], D_I // 2)
    ik2_ref[...] = ik2.astype(BF16)
    ik_ref[...] = ik2[:, :D_I]
    iw_ref[...] = seg("iw") * (H_I ** -0.5)

    cq = seg("cq")
    cq_ref[...] = cq * jax.nn.sigmoid(cq)
    z = seg("cf")
    lb = lb_ref[...]
    la = jnp.log(jnp.maximum(lb, LB_FLOOR))
    lsig = jnp.minimum(z, 0.0) - jnp.log1p(jnp.exp(-jnp.abs(z)))
    bb = jnp.log1p(-lb) + lsig
    cgl_ref[...] = jnp.maximum(la, bb) + jnp.log1p(jnp.exp(-jnp.abs(la - bb)))
    ck_ref[...] = (1.0 - lb) * jax.nn.sigmoid(-z)
    cv_ref[...] = seg("ci")
    cg_ref[...] = seg("cg")


def _in_projection(x2d, gain, w_pack, lb_row, tabs, tm):
    n = x2d.shape[0]
    ca, sa, cb, sb = tabs
    period = ca.shape[0] // tm
    row = lambda w: pl.BlockSpec((tm, w), lambda i: (i, 0))
    tab = lambda w: pl.BlockSpec((tm, w), lambda i: (i % period, 0))
    outs = (("aq", WA, BF16), ("ak", WA, F32), ("av", WA, F32),
            ("bq", WBP, BF16), ("bk", WB, F32), ("bv", WB, F32),
            ("iq", WIQ, BF16), ("ik2", 2 * D_I, BF16), ("ik", D_I, F32), ("iw", LANES, F32),
            ("cq", WBP, F32), ("ck", WBP, F32), ("cgl", WBP, F32), ("cv", WBP, F32), ("cg", WBP, F32))
    res = pl.pallas_call(
        _inproj_kernel,
        grid=(n // tm,),
        in_specs=[row(x2d.shape[1]), _const_spec(gain.shape), _const_spec(w_pack.shape),
                  _const_spec(lb_row.shape), tab(WA), tab(WA), tab(WBP), tab(WBP)],
        out_specs=[row(w) for _, w, _ in outs],
        out_shape=[jax.ShapeDtypeStruct((n, w), dt) for _, w, dt in outs],
        compiler_params=_cparams(1),
        name="in_projection",
    )(x2d, gain, w_pack, lb_row, ca, sa, cb, sb)
    return dict(zip([o[0] for o in outs], res))


def _attn_a_kernel(lam_ref, q_ref, k_ref, v_ref, o_ref, m_ref, l_ref, acc_ref, *, tq, tk, q_off, kv_len):
    qi = pl.program_id(2)
    q = q_ref[...]
    lane = lax.broadcasted_iota(jnp.int32, q.shape, 1)
    qm = [jnp.where(lane // DA == i, q, jnp.zeros_like(q)) for i in range(4)]
    m_ref[...] = jnp.full(m_ref.shape, NEG_BIG, F32)
    l_ref[...] = jnp.zeros(l_ref.shape, F32)
    acc_ref[...] = jnp.zeros(acc_ref.shape, F32)

    q_first = q_off + qi * tq
    q_last = q_first + tq - 1
    n_full = jnp.minimum((q_first // CHUNK + 1) * CHUNK, kv_len) // tk
    lim = jnp.minimum((q_last // CHUNK + 1) * CHUNK, kv_len)
    n_tot = (lim + tk - 1) // tk

    def step(j, masked):
        start = pl.multiple_of(j * tk, tk)
        kb = k_ref[pl.ds(start, tk), :].astype(BF16)
        vb = v_ref[pl.ds(start, tk), :].astype(BF16)
        if masked:
            kpos = start + lax.broadcasted_iota(jnp.int32, (tq, tk), 1)
            qpos = q_first + lax.broadcasted_iota(jnp.int32, (tq, tk), 0)
            valid = (kpos // CHUNK <= qpos // CHUNK) & (kpos < kv_len)
        for i in range(4):
            s = _dot_nt(qm[i], kb)
            if masked:
                s = jnp.where(valid, s, NEG_BIG)
            m_old = m_ref[i]
            m_new = jnp.maximum(m_old, jnp.max(s, axis=-1, keepdims=True))
            alpha = jnp.exp(m_old - m_new)
            p = jnp.exp(s - m_new)
            l_ref[i] = alpha * l_ref[i] + jnp.sum(p, axis=-1, keepdims=True)
            acc_ref[i] = alpha * acc_ref[i] + _dot(p.astype(BF16), vb)
            m_ref[i] = m_new

    def full_body(j, c):
        step(j, False)
        return c

    def masked_body(j, c):
        step(j, True)
        return c

    lax.fori_loop(0, n_full, full_body, 0)
    lax.fori_loop(n_full, n_tot, masked_body, 0)

    lam = lam_ref[0]
    o0 = acc_ref[0] / l_ref[0] - lam * (acc_ref[1] / l_ref[1])
    o1 = acc_ref[2] / l_ref[2] - lam * (acc_ref[3] / l_ref[3])
    o_ref[...] = jnp.where(lane < DV_A, o0, o1)


def _attention_a(lam, q, k, v, *, q_off, kv_len, tq, tk):
    b, t, _ = q.shape
    lk = k.shape[1]
    kern = functools.partial(_attn_a_kernel, tq=tq, tk=tk, q_off=q_off, kv_len=kv_len)
    return pl.pallas_call(
        kern,
        grid=(b, WA // LANES, t // tq),
        in_specs=[pl.BlockSpec(memory_space=pltpu.SMEM),
                  pl.BlockSpec((None, tq, LANES), lambda bi, hi, qi: (bi, qi, hi)),
                  pl.BlockSpec((None, lk, LANES), lambda bi, hi, qi: (bi, 0, hi)),
                  pl.BlockSpec((None, lk, LANES), lambda bi, hi, qi: (bi, 0, hi))],
        out_specs=pl.BlockSpec((None, tq, LANES), lambda bi, hi, qi: (bi, qi, hi)),
        out_shape=jax.ShapeDtypeStruct((b, t, WA), F32),
        scratch_shapes=[pltpu.VMEM((4, tq, 1), F32), pltpu.VMEM((4, tq, 1), F32),
                        pltpu.VMEM((4, tq, LANES), F32)],
        compiler_params=_cparams(3),
        name="mixer_a",
    )(lam, q, k, v)


TIE_BLOCK = 256


def _attn_b_kernel(q_ref, iq_ref, iw_ref, k_ref, v_ref, ik_ref, o_ref, key_ref, bias_ref,
                   *, tq, q_off, kv_len, n_sel):
    qc = pl.program_id(1)
    lk = k_ref.shape[0]
    ik2 = ik_ref[...]
    iq = iq_ref[...]
    iw = iw_ref[...]
    lane_q = lax.broadcasted_iota(jnp.int32, (tq, LANES), 1)

    score = jnp.zeros((tq, lk), F32)
    for h in range(H_I):
        pair = iq[:, LANES * (h // 2):LANES * (h // 2 + 1)]
        qh = jnp.where(lane_q // D_I == h % 2, pair, jnp.zeros_like(pair))
        score = score + iw[:, h:h + 1] * jnp.maximum(_dot_nt(qh, ik2), 0.0)

    kpos = lax.broadcasted_iota(jnp.int32, (tq, lk), 1)
    qpos = q_off + qc * tq + lax.broadcasted_iota(jnp.int32, (tq, lk), 0)
    valid = (kpos // CHUNK <= qpos // CHUNK) & (kpos < kv_len)
    score = jnp.where(valid, score, NEG_BIG)

    bits = lax.bitcast_convert_type(score, jnp.int32)
    key = bits ^ ((bits >> 31) & jnp.int32(0x7FFFFFFF))
    key = jnp.where(key == -1, 0, key)
    key_ref[...] = key

    kf = float(n_sel)

    def count_ge(cand):
        return jnp.sum(jnp.where(key_ref[...] >= cand, 1.0, 0.0), axis=-1, keepdims=True)

    int_min = jnp.int32(-2 ** 31)
    cur = jnp.where(count_ge(jnp.zeros((tq, 1), jnp.int32)) >= kf, jnp.int32(0), int_min)

    def bit_body(i, cur):
        cand = cur | jnp.left_shift(jnp.int32(1), 30 - i)
        return jnp.where(count_ge(cand) >= kf, cand, cur)

    thr = lax.fori_loop(0, 31, bit_body, cur)

    key = key_ref[...]
    gt = key > thr
    eq = key == thr
    need = kf - jnp.sum(jnp.where(gt, 1.0, 0.0), axis=-1, keepdims=True)
    r_i = lax.broadcasted_iota(jnp.int32, (TIE_BLOCK, TIE_BLOCK), 0)
    c_i = lax.broadcasted_iota(jnp.int32, (TIE_BLOCK, TIE_BLOCK), 1)
    tri = jnp.where(r_i <= c_i, 1.0, 0.0).astype(BF16)
    carry = jnp.zeros((tq, 1), F32)
    for jb in range(lk // TIE_BLOCK):
        sl = slice(jb * TIE_BLOCK, (jb + 1) * TIE_BLOCK)
        eq_b = eq[:, sl]
        pref = _dot(jnp.where(eq_b, 1.0, 0.0).astype(BF16), tri) + carry
        carry = pref[:, TIE_BLOCK - 1:TIE_BLOCK]
        sel = (gt[:, sl] | (eq_b & (pref <= need))) & valid[:, sl]
        bias_ref[:, sl] = jnp.where(sel, 0.0, NEG_BIG)

    q = q_ref[...]
    o_ref[:, WB:] = jnp.zeros((tq, WBP - WB), F32)
    for p in range(NPAIR):
        width = min(LANES, WB - p * LANES)
        kp = k_ref[:, p * LANES:p * LANES + width].astype(BF16)
        vp = v_ref[:, p * LANES:p * LANES + width].astype(BF16)
        qp = q[:, p * LANES:p * LANES + width]
        lane_p = lax.broadcasted_iota(jnp.int32, (tq, width), 1)
        o_pair = jnp.zeros((tq, width), F32)
        for hh in range(width // D_B):
            qh = jnp.where(lane_p // D_B == hh, qp, jnp.zeros_like(qp))
            s = _dot_nt(qh, kp) + bias_ref[...]
            m = jnp.max(s, axis=-1, keepdims=True)
            pr = jnp.exp(s - m)
            l = jnp.sum(pr, axis=-1, keepdims=True)
            o_h = _dot(pr.astype(BF16), vp) / l
            o_pair = jnp.where(lane_p // D_B == hh, o_h, o_pair)
        o_ref[:, p * LANES:p * LANES + width] = o_pair


def _attention_b(q, iq, iw, k, v, ik2, *, q_off, kv_len, tq, n_sel):
    b, t, _ = q.shape
    lk = k.shape[1]
    kern = functools.partial(_attn_b_kernel, tq=tq, q_off=q_off, kv_len=kv_len, n_sel=n_sel)
    qspec = lambda w: pl.BlockSpec((None, tq, w), lambda bi, qi: (bi, qi, 0))
    kspec = lambda w: pl.BlockSpec((None, lk, w), lambda bi, qi: (bi, 0, 0))
    return pl.pallas_call(
        kern,
        grid=(b, t // tq),
        in_specs=[qspec(WBP), qspec(WIQ), qspec(LANES), kspec(WB), kspec(WB), kspec(2 * D_I)],
        out_specs=qspec(WBP),
        out_shape=jax.ShapeDtypeStruct((b, t, WBP), F32),
        scratch_shapes=[pltpu.VMEM((tq, lk), jnp.int32), pltpu.VMEM((tq, lk), F32)],
        compiler_params=_cparams(2),
        name="mixer_b",
    )(q, iq, iw, k, v, ik2)


GROUP = SUBLANES


def _hgrn2_kernel(q_ref, k_ref, g_ref, v_ref, s0_ref, o_ref, s_out_ref, st_ref, *, c):
    ci = pl.program_id(1)

    @pl.when(ci == 0)
    def _():
        st_ref[...] = s0_ref[...]

    q = q_ref[...]
    k = k_ref[...]
    g = g_ref[...]
    v = v_ref[...]
    w = q.shape[-1]
    row = lax.broadcasted_iota(jnp.int32, (c, w), 0)
    lane = lax.broadcasted_iota(jnp.int32, (c, w), 1)

    cs, tots = {1: g}, {1: g}
    cum, tot, m = g, g, 1
    while m < c:
        upper = (row // m) % 2 == 1
        prev_tot = pltpu.roll(tot, m, 0)
        next_tot = pltpu.roll(tot, c - m, 0)
        cum = cum + jnp.where(upper, prev_tot, 0.0)
        tot = tot + jnp.where(upper, prev_tot, next_tot)
        m *= 2
        cs[m], tots[m] = cum, tot
    bcum, blast = cs[c], tots[c]

    head_masks = [lane // DK_C == h for h in range(w // DK_C)]
    n_heads = H_C

    rq = lax.broadcasted_iota(jnp.int32, (c, c), 0)
    rk = lax.broadcasted_iota(jnp.int32, (c, c), 1)
    att = jnp.zeros((n_heads * c, c), F32)
    half = GROUP
    while half < c:
        upper = (row // half) % 2 == 1
        qt = jnp.where(upper, q * jnp.exp(cs[half]), 0.0)
        kt = jnp.where(upper, 0.0, k * jnp.exp(tots[half] - cs[half])).astype(BF16)
        lhs = jnp.concatenate([jnp.where(head_masks[h], qt, 0.0) for h in range(n_heads)], axis=0)
        blk = _dot_nt(lhs.astype(BF16), kt)
        same = (rq // (2 * half)) == (rk // (2 * half))
        same = jnp.concatenate([same] * n_heads, axis=0)
        att = att + jnp.where(same, blk, 0.0)
        half *= 2
    res = _dot(att.astype(BF16), v.astype(BF16))
    o = jnp.zeros((c, w), F32)
    for h in range(n_heads):
        o = o + jnp.where(head_masks[h], res[h * c:(h + 1) * c], 0.0)

    def group_row(x, j):
        x3 = x.reshape(c // GROUP, GROUP, w)
        return jnp.broadcast_to(x3[:, j:j + 1, :], x3.shape).reshape(c, w)

    c8 = cs[GROUP]
    vals = []
    for j in range(GROUP):
        ok = (row % GROUP) >= j
        e = jnp.where(ok, c8 - group_row(c8, j), 0.0)
        vals.append(jnp.where(ok, q * group_row(k, j) * jnp.exp(e), 0.0))
    r_i = lax.broadcasted_iota(jnp.int32, (w, w), 0)
    c_i = lax.broadcasted_iota(jnp.int32, (w, w), 1)
    head_sum = jnp.where(r_i // DK_C == c_i // DK_C, 1.0, 0.0).astype(BF16)
    wts = _dot(jnp.concatenate(vals, axis=0).astype(BF16), head_sum)
    for j in range(GROUP):
        o = o + wts[j * c:(j + 1) * c] * group_row(v, j)

    qe = (q * jnp.exp(bcum)).astype(BF16)
    k2 = (k * jnp.exp(blast - bcum)).astype(BF16)
    decay = jnp.exp(blast[0:1, :])
    pr = lax.broadcasted_iota(jnp.int32, (LANES, LANES), 0)
    pc = lax.broadcasted_iota(jnp.int32, (LANES, LANES), 1)
    diag = pr // DK_C == pc // DK_C
    o_state = []
    for p in range(w // LANES):
        sl = slice(p * LANES, (p + 1) * LANES)
        st = st_ref[p]
        o_state.append(_dot_nt(qe[:, sl], st.astype(BF16)))
        upd = _dot(v[:, sl].T.astype(BF16), k2[:, sl])
        st_ref[p] = st * decay[:, sl] + jnp.where(diag, upd, 0.0)
    o_ref[...] = o + jnp.concatenate(o_state, axis=-1)

    @pl.when(ci == pl.num_programs(1) - 1)
    def _():
        s_out_ref[...] = st_ref[...]


def _hgrn2(q, k, g, v, s0, *, c):
    b, t, w = q.shape
    kern = functools.partial(_hgrn2_kernel, c=c)
    blk = pl.BlockSpec((None, c, w), lambda bi, ci: (bi, ci, 0))
    sblk = pl.BlockSpec((None, w // LANES, LANES, LANES), lambda bi, ci: (bi, 0, 0, 0))
    return pl.pallas_call(
        kern,
        grid=(b, t // c),
        in_specs=[blk, blk, blk, blk, sblk],
        out_specs=[blk, sblk],
        out_shape=[jax.ShapeDtypeStruct((b, t, w), F32),
                   jax.ShapeDtypeStruct((b, w // LANES, LANES, LANES), F32)],
        scratch_shapes=[pltpu.VMEM((w // LANES, LANES, LANES), F32)],
        compiler_params=_cparams(2),
        name="mixer_c",
    )(q, k, g, v, s0)


def _state_to_pairs(s):
    b = s.shape[0]
    st = jnp.swapaxes(s.astype(F32), -1, -2)
    st = jnp.pad(st, ((0, 0), (0, 2 * NPAIR - H_C), (0, 0), (0, 0)))
    st = st.reshape(b, NPAIR, 2, DV_C, DK_C)
    eye = jnp.eye(2, dtype=F32)
    full = st[:, :, :, :, None, :] * eye[None, None, :, None, :, None]
    return full.reshape(b, NPAIR, 2 * DV_C, 2 * DK_C)


def _pairs_to_state(sp):
    b = sp.shape[0]
    s6 = sp.reshape(b, NPAIR, 2, DV_C, 2, DK_C)
    diag = jnp.stack([s6[:, :, a, :, a, :] for a in range(2)], axis=2)
    return jnp.swapaxes(diag.reshape(b, 2 * NPAIR, DV_C, DK_C)[:, :H_C], -1, -2)


FF_BLOCK = 256


def _head_norm(y, gain):
    w = y.shape[-1]
    r_i = lax.broadcasted_iota(jnp.int32, (w, w), 0)
    c_i = lax.broadcasted_iota(jnp.int32, (w, w), 1)
    head_sum = jnp.where(r_i // DV_A == c_i // DV_A, 1.0, 0.0).astype(BF16)
    y2 = y * y
    hi = y2.astype(BF16)
    lo = (y2 - hi.astype(F32)).astype(BF16)
    ms = (_dot(hi, head_sum) + _dot(lo, head_sum)) * (1.0 / DV_A)
    return y * lax.rsqrt(ms + EPS) * gain


def _merge_kernel(scal_ref, x_ref, oa_ref, ob_ref, oc_ref, cg_ref, hist_ref, ag_ref, cgn_ref, wo_ref,
                  n2_ref, wup_ref, cw_ref, cb_ref, wdn_ref, fn_ref, *out_and_scratch, d_ff, final):
    if final:
        x_out_ref, fc_ref, y_ref, carry_ref = out_and_scratch
    else:
        x_out_ref, fc_ref, carry_ref = out_and_scratch
    ti = pl.program_id(1)
    tm = x_ref.shape[0]

    @pl.when(ti == 0)
    def _():
        carry_ref[...] = hist_ref[...]

    oa = _head_norm(oa_ref[...], ag_ref[...]) * scal_ref[0]
    cg = cg_ref[...]
    oc = _head_norm(oc_ref[...], cgn_ref[...]) * (cg * jax.nn.sigmoid(cg))
    mixed = (_dot(oa.astype(BF16), wo_ref[0:WA, :])
             + _dot(ob_ref[...].astype(BF16), wo_ref[WA:WA + WBP, :])
             + _dot(oc.astype(BF16), wo_ref[WA + WBP:WA + 2 * WBP, :]))
    x = x_ref[...] + mixed

    ms = jnp.mean(x * x, axis=-1, keepdims=True)
    h = (x * lax.rsqrt(ms + EPS) * n2_ref[...]).astype(BF16)
    row = lax.broadcasted_iota(jnp.int32, (tm, FF_BLOCK), 0)
    acc = jnp.zeros(x.shape, F32)
    for cblk in range(d_ff // FF_BLOCK):
        sl = slice(cblk * FF_BLOCK, (cblk + 1) * FF_BLOCK)
        a = _dot(h, wup_ref[:, sl])
        gate = _dot(h, wup_ref[:, d_ff + cblk * FF_BLOCK:d_ff + (cblk + 1) * FF_BLOCK])
        prev2 = carry_ref[0:1, sl]
        prev1 = carry_ref[1:2, sl]
        a1 = jnp.where(row == 0, prev1, pltpu.roll(a, 1, 0))
        a2 = jnp.where(row == 0, prev2, jnp.where(row == 1, prev1, pltpu.roll(a, 2, 0)))
        conv = cb_ref[:, sl] + a2 * cw_ref[0:1, sl] + a1 * cw_ref[1:2, sl] + a * cw_ref[2:3, sl]
        act = conv * jax.nn.sigmoid(conv) * gate
        acc = acc + _dot(act.astype(BF16), wdn_ref[sl, :])
        carry_ref[:, sl] = a[tm - (CONV_W - 1):, :]
    x = x + acc
    x_out_ref[...] = x
    fc_ref[...] = carry_ref[...]
    if final:
        ms = jnp.mean(x * x, axis=-1, keepdims=True)
        y_ref[...] = x * lax.rsqrt(ms + EPS) * fn_ref[...]


def _merge_ffn(scal, x, oa, ob, oc, cg, hist, a_gain, c_gain, wo, n2, wup, cw, cb, wdn, fnorm, *, tm, final):
    b, t, d = x.shape
    d_ff = wdn.shape[0]
    kern = functools.partial(_merge_kernel, d_ff=d_ff, final=final)
    blk = lambda w: pl.BlockSpec((None, tm, w), lambda bi, ti: (bi, ti, 0))
    per_b = pl.BlockSpec((None, CONV_W - 1, d_ff), lambda bi, ti: (bi, 0, 0))
    out_specs = [blk(d), per_b]
    out_shape = [jax.ShapeDtypeStruct((b, t, d), F32), jax.ShapeDtypeStruct((b, CONV_W - 1, d_ff), F32)]
    if final:
        out_specs.append(blk(d))
        out_shape.append(jax.ShapeDtypeStruct((b, t, d), F32))
    return pl.pallas_call(
        kern,
        grid=(b, t // tm),
        in_specs=[pl.BlockSpec(memory_space=pltpu.SMEM), blk(d), blk(WA), blk(WBP), blk(WBP), blk(WBP), per_b,
                  _const_spec(a_gain.shape), _const_spec(c_gain.shape), _const_spec(wo.shape),
                  _const_spec(n2.shape), _const_spec(wup.shape), _const_spec(cw.shape),
                  _const_spec(cb.shape), _const_spec(wdn.shape), _const_spec(fnorm.shape)],
        out_specs=out_specs,
        out_shape=out_shape,
        scratch_shapes=[pltpu.VMEM((CONV_W - 1, d_ff), F32)],
        compiler_params=_cparams(2),
        name="merge_ffn",
    )(scal, x, oa, ob, oc, cg, hist, a_gain, c_gain, wo, n2, wup, cw, cb, wdn, fnorm)


def _pack_w_in(w):
    parts = jnp.split(w, np.cumsum(IN_SIZES)[:-1].tolist(), axis=-1)
    cols = []
    for (name, width, padded), part in zip(SEGS, parts):
        if name == "ik":
            part = jnp.concatenate([part, part], axis=-1)
            width = 2 * D_I
        cols.append(jnp.pad(part, ((0, 0), (0, padded - width))))
    return jnp.concatenate(cols, axis=-1).astype(BF16)


def _pack_w_out(w):
    wa, wb, wc = w[:WA], w[WA:WA + WB], w[WA + WB:]
    pad = lambda m: jnp.pad(m, ((0, WBP - WB), (0, 0)))
    return jnp.concatenate([wa, pad(wb), pad(wc)], axis=0).astype(BF16)


def _rope_table(pos, dim, width):
    inv_freq = ROPE_THETA ** (-jnp.arange(0, dim, 2, dtype=F32) / dim)
    ang = pos.astype(F32)[:, None] * inv_freq[None, :]
    reps = width // (dim // 2)
    return jnp.tile(jnp.cos(ang), (1, reps)), jnp.tile(jnp.sin(ang), (1, reps))


def _pad_rows(x, rows):
    return jnp.pad(x, ((0, 0), (0, rows - x.shape[1]), (0, 0)))


def _round_up(n, m):
    return (n + m - 1) // m * m


def kernel(x_prompt, x_sample, cache_a_k, cache_a_v, cache_b_k, cache_b_v, cache_b_kidx, state_c, state_ffn_conv, norm1, w_in, lam_q1, lam_k1, lam_q2, lam_k2, a_norm, c_lower, c_norm, w_out, norm2, ffn_up, ffn_conv_w, ffn_conv_b, ffn_down, final_norm):
    depth = w_in.shape[0]
    b_p, t_p, d = x_prompt.shape
    b_s, t_s, _ = x_sample.shape
    past = cache_a_k.shape[2]
    d_ff = ffn_down.shape[1]
    kv_s = past + t_s
    n_sel_p = min(TOPK_MAX, t_p // 4)
    n_sel_s = min(TOPK_MAX, kv_s // 4)
    tk = 256
    lk_s = _round_up(kv_s, tk)

    lb_soft = jax.nn.softmax(c_lower.astype(F32), axis=0)
    lower = jnp.cumsum(lb_soft, axis=0) - lb_soft[0]
    lower = jnp.pad(lower, ((0, 0), (0, WBP - WB)))

    tm_p = min(512, t_p)
    tm_s = min(256, b_s * t_s)
    pos_p = jnp.arange(t_p)
    pos_s = jnp.tile(past + jnp.arange(t_s), tm_s // t_s)
    tabs_p = _rope_table(pos_p, DA, WA) + _rope_table(pos_p, D_B, WBP)
    tabs_s = _rope_table(pos_s, DA, WA) + _rope_table(pos_s, D_B, WBP)

    xp, xs = x_prompt, x_sample
    outs_p = [[] for _ in range(7)]
    outs_s = [[] for _ in range(7)]
    y_p = y_s = None
    fnorm = final_norm.reshape(1, d)
    for l in range(depth):
        lam_init = 0.8 - 0.6 * math.exp(-0.3 * l)
        lam = (jnp.exp(jnp.sum(lam_q1[l].astype(F32) * lam_k1[l].astype(F32)))
               - jnp.exp(jnp.sum(lam_q2[l].astype(F32) * lam_k2[l].astype(F32))) + lam_init)
        lam_arr = lam.reshape(1).astype(F32)
        scal = jnp.full((1,), 1.0 - lam_init, F32)
        w_pack = _pack_w_in(w_in[l])
        wo = _pack_w_out(w_out[l])
        gain1 = norm1[l].reshape(1, d)
        gain2 = norm2[l].reshape(1, d)
        lb_row = lower[l].reshape(1, WBP)
        a_gain = jnp.tile(a_norm[l], H_A).reshape(1, WA)
        c_gain = jnp.pad(jnp.tile(c_norm[l], H_C), (0, WBP - WB)).reshape(1, WBP)
        wup = ffn_up[l].astype(BF16)
        wdn = ffn_down[l].astype(BF16)
        cw = ffn_conv_w[l]
        cb = ffn_conv_b[l].reshape(1, d_ff)
        final = l == depth - 1

        u = _in_projection(xp.reshape(b_p * t_p, d), gain1, w_pack, lb_row, tabs_p, tm_p)
        r3 = lambda a, b=b_p, t=t_p: a.reshape(b, t, a.shape[-1])
        oa = _attention_a(lam_arr, r3(u["aq"]), r3(u["ak"]), r3(u["av"]),
                          q_off=0, kv_len=t_p, tq=min(256, t_p), tk=min(tk, t_p))
        ob = _attention_b(r3(u["bq"]), r3(u["iq"]), r3(u["iw"]), r3(u["bk"]), r3(u["bv"]), r3(u["ik2"]),
                          q_off=0, kv_len=t_p, tq=CHUNK, n_sel=n_sel_p)
        s0 = jnp.zeros((b_p, NPAIR, LANES, LANES), F32)
        oc, s_new = _hgrn2(r3(u["cq"]), r3(u["ck"]), r3(u["cgl"]), r3(u["cv"]), s0, c=CHUNK)
        hist0 = jnp.zeros((b_p, CONV_W - 1, d_ff), F32)
        res = _merge_ffn(scal, xp, oa, ob, oc, r3(u["cg"]), hist0, a_gain, c_gain, wo, gain2, wup, cw, cb,
                         wdn, fnorm, tm=min(512, t_p), final=final)
        xp, fc = res[0], res[1]
        if final:
            y_p = res[2]
        for lst, val in zip(outs_p, (u["ak"].reshape(b_p, t_p, H_A, 2 * DA), u["av"].reshape(b_p, t_p, H_A, DV_A),
                                     u["bk"].reshape(b_p, t_p, H_B, D_B), u["bv"].reshape(b_p, t_p, H_B, D_B),
                                     u["ik"].reshape(b_p, t_p, D_I), _pairs_to_state(s_new), fc)):
            lst.append(val)

        u = _in_projection(xs.reshape(b_s * t_s, d), gain1, w_pack, lb_row, tabs_s, tm_s)
        r3 = lambda a, b=b_s, t=t_s: a.reshape(b, t, a.shape[-1])
        cat = lambda c, new: _pad_rows(jnp.concatenate([c.reshape(b_s, past, -1).astype(new.dtype), r3(new)],
                                                       axis=1), lk_s)
        kidx = cache_b_kidx[l]
        kidx2 = jnp.concatenate([kidx, kidx], axis=-1)
        oa = _attention_a(lam_arr, r3(u["aq"]), cat(cache_a_k[l], u["ak"]), cat(cache_a_v[l], u["av"]),
                          q_off=past, kv_len=kv_s, tq=t_s, tk=tk)
        ob = _attention_b(r3(u["bq"]), r3(u["iq"]), r3(u["iw"]), cat(cache_b_k[l], u["bk"]),
                          cat(cache_b_v[l], u["bv"]), cat(kidx2, u["ik2"]),
                          q_off=past, kv_len=kv_s, tq=t_s, n_sel=n_sel_s)
        oc, s_new = _hgrn2(r3(u["cq"]), r3(u["ck"]), r3(u["cgl"]), r3(u["cv"]), _state_to_pairs(state_c[l]),
                           c=t_s)
        res = _merge_ffn(scal, xs, oa, ob, oc, r3(u["cg"]), state_ffn_conv[l].astype(F32), a_gain, c_gain, wo,
                         gain2, wup, cw, cb, wdn, fnorm, tm=t_s, final=final)
        xs, fc = res[0], res[1]
        if final:
            y_s = res[2]
        for lst, val in zip(outs_s, (u["ak"].reshape(b_s, t_s, H_A, 2 * DA), u["av"].reshape(b_s, t_s, H_A, DV_A),
                                     u["bk"].reshape(b_s, t_s, H_B, D_B), u["bv"].reshape(b_s, t_s, H_B, D_B),
                                     u["ik"].reshape(b_s, t_s, D_I), _pairs_to_state(s_new), fc)):
            lst.append(val)

    return (y_p, y_s) + tuple(jnp.stack(v) for v in outs_p) + tuple(jnp.stack(v) for v in outs_s)
```

```python
import functools
import math

import jax
import jax.numpy as jnp
import numpy as np
from jax import lax
from jax.experimental import pallas as pl
from jax.experimental.pallas import tpu as pltpu

F32 = jnp.float32
BF16 = jnp.bfloat16

CHUNK = 64
ROPE_THETA = 10000.0
EPS = 1e-6
NEG_BIG = -1e30
LB_FLOOR = 1e-20
H_A, DA, DV_A = 6, 32, 64
H_B, D_B = 5, 64
H_I, D_I = 4, 64
TOPK_MAX = 256
H_C, DK_C, DV_C = 5, 64, 64
CONV_W = 3

LANES = 128
SUBLANES = 8
VMEM_LIMIT = 56 * 1024 * 1024

WA = H_A * 2 * DA
WB = H_B * D_B
WBP = 384
WIQ = H_I * D_I
NPAIR = WBP // LANES

SEGS = (("aq", WA, WA), ("ak", WA, WA), ("av", WA, WA),
        ("bq", WB, WBP), ("bk", WB, WBP), ("bv", WB, WBP),
        ("iq", WIQ, WIQ), ("ik", D_I, 2 * D_I), ("iw", H_I, LANES),
        ("cq", WB, WBP), ("cf", WB, WBP), ("ci", WB, WBP), ("cg", WB, WBP))
SEG_OFF = {}
_o = 0
for _n, _w, _p in SEGS:
    SEG_OFF[_n] = (_o, _p)
    _o += _p
W_PACK = _o
IN_SIZES = (WA, WA, WA, WB, WB, WB, WIQ, D_I, H_I, WB, WB, WB, WB)

LOG2E = math.log2(math.e)
NT_DIMS = (((1,), (1,)), ((), ()))


def _cparams(n_axes):
    return pltpu.CompilerParams(dimension_semantics=("arbitrary",) * n_axes,
                                vmem_limit_bytes=VMEM_LIMIT)


def _const_spec(shape):
    nd = len(shape)
    return pl.BlockSpec(shape, lambda *_: (0,) * nd, pipeline_mode=pl.Buffered(1))


def _dot(a, b):
    return jnp.dot(a, b, preferred_element_type=F32)


def _dot_nt(a, b):
    return lax.dot_general(a, b, NT_DIMS, preferred_element_type=F32)


def _rope(u, cos, sin, half):
    w = u.shape[-1]
    lane = lax.broadcasted_iota(jnp.int32, u.shape, 1)
    first = (lane % (2 * half)) < half
    rot = jnp.where(first, -pltpu.roll(u, w - half, 1), pltpu.roll(u, half, 1))
    return u * cos + rot * sin


def _inproj_kernel(x_ref, g_ref, w_ref, lb_ref, ca_ref, sa_ref, cb_ref, sb_ref,
                   aq_ref, ak_ref, av_ref, ak16_ref, av16_ref, bq_ref, bk_ref, bv_ref, bk16_ref, bv16_ref,
                   iq_ref, ik2_ref, ik_ref, iw_ref, cq_ref, ck_ref, cgl_ref, cv_ref, cg_ref):
    x = x_ref[...]
    ms = jnp.mean(x * x, axis=-1, keepdims=True)
    xn = (x * lax.rsqrt(ms + EPS) * g_ref[...]).astype(BF16)

    def seg(name):
        off, width = SEG_OFF[name]
        return _dot(xn, w_ref[:, off:off + width])

    ca, sa = ca_ref[...], sa_ref[...]
    cb, sb = cb_ref[...], sb_ref[...]
    aq_ref[...] = (_rope(seg("aq"), ca, sa, DA // 2) * (DA ** -0.5 * LOG2E)).astype(BF16)
    ak = _rope(seg("ak"), ca, sa, DA // 2)
    ak_ref[...] = ak
    ak16_ref[...] = ak.astype(BF16)
    av = seg("av")
    av_ref[...] = av
    av16_ref[...] = av.astype(BF16)
    bq_ref[...] = (_rope(seg("bq"), cb, sb, D_B // 2) * (D_B ** -0.5 * LOG2E)).astype(BF16)
    bk = _rope(seg("bk"), cb, sb, D_B // 2)
    bk_ref[...] = bk[:, :WB]
    bk16_ref[...] = bk.astype(BF16)
    bv = seg("bv")
    bv_ref[...] = bv[:, :WB]
    bv16_ref[...] = bv.astype(BF16)
    iq_ref[...] = (_rope(seg("iq"), cb[:, :WIQ], sb[:, :WIQ], D_I // 2) * (D_I ** -0.5)).astype(BF16)
    ik2 = _rope(seg("ik"), cb[:, :2 * D_I], sb[:, :2 * D_I], D_I // 2)
    ik2_ref[...] = ik2.astype(BF16)
    ik_ref[...] = ik2[:, :D_I]
    iw_ref[...] = seg("iw") * (H_I ** -0.5)

    cq = seg("cq")
    cq_ref[...] = cq * jax.nn.sigmoid(cq)
    z = seg("cf")
    lb = lb_ref[...]
    la = jnp.log(jnp.maximum(lb, LB_FLOOR))
    lsig = jnp.minimum(z, 0.0) - jnp.log1p(jnp.exp(-jnp.abs(z)))
    bb = jnp.log1p(-lb) + lsig
    cgl_ref[...] = jnp.maximum(la, bb) + jnp.log1p(jnp.exp(-jnp.abs(la - bb)))
    ck_ref[...] = (1.0 - lb) * jax.nn.sigmoid(-z)
    cv_ref[...] = seg("ci")
    cg_ref[...] = seg("cg")


def _in_projection(x2d, gain, w_pack, lb_row, tabs, tm):
    n = x2d.shape[0]
    ca, sa, cb, sb = tabs
    period = ca.shape[0] // tm
    row = lambda w: pl.BlockSpec((tm, w), lambda i: (i, 0))
    tab = lambda w: pl.BlockSpec((tm, w), lambda i: (i % period, 0))
    outs = (("aq", WA, BF16), ("ak", WA, F32), ("av", WA, F32), ("ak16", WA, BF16), ("av16", WA, BF16),
            ("bq", WBP, BF16), ("bk", WB, F32), ("bv", WB, F32), ("bk16", WBP, BF16), ("bv16", WBP, BF16),
            ("iq", WIQ, BF16), ("ik2", 2 * D_I, BF16), ("ik", D_I, F32), ("iw", LANES, F32),
            ("cq", WBP, F32), ("ck", WBP, F32), ("cgl", WBP, F32), ("cv", WBP, F32), ("cg", WBP, F32))
    res = pl.pallas_call(
        _inproj_kernel,
        grid=(n // tm,),
        in_specs=[row(x2d.shape[1]), _const_spec(gain.shape), _const_spec(w_pack.shape),
                  _const_spec(lb_row.shape), tab(WA), tab(WA), tab(WBP), tab(WBP)],
        out_specs=[row(w) for _, w, _ in outs],
        out_shape=[jax.ShapeDtypeStruct((n, w), dt) for _, w, dt in outs],
        compiler_params=_cparams(1),
        name="in_projection",
    )(x2d, gain, w_pack, lb_row, ca, sa, cb, sb)
    return dict(zip([o[0] for o in outs], res))


def _attn_a_kernel(lam_ref, q_ref, k_ref, v_ref, o_ref, q4_ref, m_ref, acc_ref, *, tq, tk, q_off, kv_len):
    qi = pl.program_id(2)
    q = q_ref[...]
    lane = lax.broadcasted_iota(jnp.int32, q.shape, 1)
    for i in range(4):
        q4_ref[i * tq:(i + 1) * tq, :] = jnp.where(lane // DA == i, q, jnp.zeros_like(q))
    m_ref[...] = jnp.full(m_ref.shape, NEG_BIG, F32)
    acc_ref[...] = jnp.zeros(acc_ref.shape, F32)

    q_first = q_off + qi * tq
    q_last = q_first + tq - 1
    n_full = jnp.minimum((q_first // CHUNK + 1) * CHUNK, kv_len) // tk
    lim = jnp.minimum((q_last // CHUNK + 1) * CHUNK, kv_len)
    n_tot = (lim + tk - 1) // tk

    def step(j, masked):
        start = pl.multiple_of(j * tk, tk)
        kb = k_ref[pl.ds(start, tk), :]
        vb = v_ref[pl.ds(start, tk), :]
        s = _dot_nt(q4_ref[...], kb)
        if masked:
            kpos = start + lax.broadcasted_iota(jnp.int32, (4 * tq, tk), 1)
            qpos = q_first + lax.broadcasted_iota(jnp.int32, (4 * tq, tk), 0) % tq
            valid = (kpos // CHUNK <= qpos // CHUNK) & (kpos < kv_len)
            s = jnp.where(valid, s, NEG_BIG)
        m_old = m_ref[...]
        m_new = jnp.maximum(m_old, jnp.max(s, axis=-1, keepdims=True))
        alpha = jnp.exp2(m_old - m_new)
        p = jnp.exp2(s - m_new).astype(BF16)
        lane_v = lax.broadcasted_iota(jnp.int32, vb.shape, 1)
        ones = jnp.ones_like(vb)
        pv = jnp.concatenate([_dot(p[0:2 * tq], jnp.where(lane_v < DV_A, vb, ones)),
                              _dot(p[2 * tq:4 * tq], jnp.where(lane_v < DV_A, ones, vb))], axis=0)
        acc_ref[...] = alpha * acc_ref[...] + pv
        m_ref[...] = m_new

    def full_body(j, c):
        step(j, False)
        return c

    def masked_body(j, c):
        step(j, True)
        return c

    lax.fori_loop(0, n_full, full_body, 0)
    lax.fori_loop(n_full, n_tot, masked_body, 0)

    lam = lam_ref[0]
    acc0 = acc_ref[0:2 * tq, :]
    acc1 = acc_ref[2 * tq:4 * tq, :]
    on0 = acc0 / acc0[:, DV_A:DV_A + 1]
    on1 = acc1 / acc1[:, 0:1]
    o0 = on0[0:tq] - lam * on0[tq:2 * tq]
    o1 = on1[0:tq] - lam * on1[tq:2 * tq]
    o_ref[...] = jnp.where(lane < DV_A, o0, o1)


def _attention_a(lam, q, k, v, *, q_off, kv_len, tq, tk):
    b, t, _ = q.shape
    lk = k.shape[1]
    kern = functools.partial(_attn_a_kernel, tq=tq, tk=tk, q_off=q_off, kv_len=kv_len)
    return pl.pallas_call(
        kern,
        grid=(b, WA // LANES, t // tq),
        in_specs=[pl.BlockSpec(memory_space=pltpu.SMEM),
                  pl.BlockSpec((None, tq, LANES), lambda bi, hi, qi: (bi, qi, hi)),
                  pl.BlockSpec((None, lk, LANES), lambda bi, hi, qi: (bi, 0, hi)),
                  pl.BlockSpec((None, lk, LANES), lambda bi, hi, qi: (bi, 0, hi))],
        out_specs=pl.BlockSpec((None, tq, LANES), lambda bi, hi, qi: (bi, qi, hi)),
        out_shape=jax.ShapeDtypeStruct((b, t, WA), F32),
        scratch_shapes=[pltpu.VMEM((4 * tq, LANES), BF16), pltpu.VMEM((4 * tq, 1), F32),
                        pltpu.VMEM((4 * tq, LANES), F32)],
        compiler_params=_cparams(3),
        name="mixer_a",
    )(lam, q, k, v)


TIE_BLOCK = 256


def _attn_b_kernel(q_ref, iq_ref, iw_ref, k_ref, v_ref, ik_ref, o_ref, key_ref, bias_ref,
                   *, tq, q_off, kv_len, n_sel, variants):
    q_first = q_off + pl.program_id(1) * tq
    need_keys = jnp.minimum(((q_first + tq - 1) // CHUNK + 1) * CHUNK, kv_len)
    lo = 0
    for lk in variants:
        @pl.when((need_keys > lo) & (need_keys <= lk))
        def _(lk=lk):
            _attn_b_body(q_ref, iq_ref, iw_ref, k_ref, v_ref, ik_ref, o_ref, key_ref, bias_ref,
                         tq=tq, q_first=q_first, kv_len=kv_len, n_sel=n_sel, lk=lk)
        lo = lk


def _stack_heads(x, n_heads, width):
    lane = lax.broadcasted_iota(jnp.int32, x.shape, 1)
    return jnp.concatenate([jnp.where(lane // width == h, x, jnp.zeros_like(x)) for h in range(n_heads)],
                           axis=0)


def _attn_b_body(q_ref, iq_ref, iw_ref, k_ref, v_ref, ik_ref, o_ref, key_ref, bias_ref,
                 *, tq, q_first, kv_len, n_sel, lk):
    ik2 = ik_ref[0:lk, :]
    iq = iq_ref[...]
    iw = iw_ref[...]

    score = jnp.zeros((tq, lk), F32)
    for pi in range(H_I // 2):
        y = _stack_heads(iq[:, LANES * pi:LANES * (pi + 1)], 2, D_I)
        d = jnp.maximum(_dot_nt(y, ik2), 0.0)
        score = score + iw[:, 2 * pi:2 * pi + 1] * d[0:tq] + iw[:, 2 * pi + 1:2 * pi + 2] * d[tq:2 * tq]

    kpos = lax.broadcasted_iota(jnp.int32, (tq, lk), 1)
    qpos = q_first + lax.broadcasted_iota(jnp.int32, (tq, lk), 0)
    valid = (kpos // CHUNK <= qpos // CHUNK) & (kpos < kv_len)
    score = jnp.where(valid, score, NEG_BIG)

    bits = lax.bitcast_convert_type(score, jnp.int32)
    key = bits ^ ((bits >> 31) & jnp.int32(0x7FFFFFFF))
    key = jnp.where(key == -1, 0, key)
    key_ref[:, 0:lk] = key

    kf = float(n_sel)

    def count_ge(cand):
        return jnp.sum(jnp.where(key_ref[:, 0:lk] >= cand, 1.0, 0.0), axis=-1, keepdims=True)

    int_min = jnp.int32(-2 ** 31)
    cur = jnp.where(count_ge(jnp.zeros((tq, 1), jnp.int32)) >= kf, jnp.int32(0), int_min)

    def bit_body(i, cur):
        cand = cur | jnp.left_shift(jnp.int32(1), 30 - i)
        return jnp.where(count_ge(cand) >= kf, cand, cur)

    thr = lax.fori_loop(0, 31, bit_body, cur)

    key = key_ref[:, 0:lk]
    gt = key > thr
    eq = key == thr
    need = kf - jnp.sum(jnp.where(gt, 1.0, 0.0), axis=-1, keepdims=True)
    r_i = lax.broadcasted_iota(jnp.int32, (TIE_BLOCK, TIE_BLOCK), 0)
    c_i = lax.broadcasted_iota(jnp.int32, (TIE_BLOCK, TIE_BLOCK), 1)
    tri = jnp.where(r_i <= c_i, 1.0, 0.0).astype(BF16)
    carry = jnp.zeros((tq, 1), F32)
    for jb in range(lk // TIE_BLOCK):
        sl = slice(jb * TIE_BLOCK, (jb + 1) * TIE_BLOCK)
        eq_b = eq[:, sl]
        pref = _dot(jnp.where(eq_b, 1.0, 0.0).astype(BF16), tri) + carry
        carry = pref[:, TIE_BLOCK - 1:TIE_BLOCK]
        sel = (gt[:, sl] | (eq_b & (pref <= need))) & valid[:, sl]
        bias_ref[:, sl] = jnp.where(sel, 0.0, NEG_BIG)

    q = q_ref[...]
    lane = lax.broadcasted_iota(jnp.int32, (tq, LANES), 1)
    bias = bias_ref[:, 0:lk]
    for p in range(NPAIR):
        n_heads = min(2, H_B - 2 * p)
        sl = slice(p * LANES, (p + 1) * LANES)
        y = _stack_heads(q[:, sl], n_heads, D_B)
        s = _dot_nt(y, k_ref[0:lk, sl]).reshape(n_heads, tq, lk) + bias[None]
        m = jnp.max(s, axis=-1, keepdims=True)
        pr = jnp.exp2(s - m)
        l = jnp.sum(pr, axis=-1, keepdims=True)
        o = _dot(pr.reshape(n_heads * tq, lk).astype(BF16), v_ref[0:lk, sl]) / l.reshape(n_heads * tq, 1)
        o_ref[:, sl] = o if n_heads == 1 else jnp.where(lane < D_B, o[0:tq], o[tq:2 * tq])


def _attention_b(q, iq, iw, k, v, ik2, *, q_off, kv_len, tq, n_sel):
    b, t, _ = q.shape
    lk = k.shape[1]
    granule = 2 * TIE_BLOCK if lk % (2 * TIE_BLOCK) == 0 else lk
    variants = tuple(range(granule, lk + 1, granule))
    kern = functools.partial(_attn_b_kernel, tq=tq, q_off=q_off, kv_len=kv_len, n_sel=n_sel,
                             variants=variants)
    qspec = lambda w: pl.BlockSpec((None, tq, w), lambda bi, qi: (bi, qi, 0))
    kspec = lambda w: pl.BlockSpec((None, lk, w), lambda bi, qi: (bi, 0, 0))
    return pl.pallas_call(
        kern,
        grid=(b, t // tq),
        in_specs=[qspec(WBP), qspec(WIQ), qspec(LANES), kspec(WBP), kspec(WBP), kspec(2 * D_I)],
        out_specs=qspec(WBP),
        out_shape=jax.ShapeDtypeStruct((b, t, WBP), F32),
        scratch_shapes=[pltpu.VMEM((tq, lk), jnp.int32), pltpu.VMEM((tq, lk), F32)],
        compiler_params=_cparams(2),
        name="mixer_b",
    )(q, iq, iw, k, v, ik2)


GROUP = SUBLANES


def _hgrn2_kernel(q_ref, k_ref, g_ref, v_ref, s0_ref, o_ref, s_out_ref, st_ref, *, c):
    ci = pl.program_id(1)

    @pl.when(ci == 0)
    def _():
        st_ref[...] = s0_ref[...]

    q = q_ref[...]
    k = k_ref[...]
    g = g_ref[...]
    v = v_ref[...]
    w = q.shape[-1]
    row = lax.broadcasted_iota(jnp.int32, (c, w), 0)
    lane = lax.broadcasted_iota(jnp.int32, (c, w), 1)

    cs, tots = {1: g}, {1: g}
    cum, tot, m = g, g, 1
    while m < c:
        upper = (row // m) % 2 == 1
        prev_tot = pltpu.roll(tot, m, 0)
        next_tot = pltpu.roll(tot, c - m, 0)
        cum = cum + jnp.where(upper, prev_tot, 0.0)
        tot = tot + jnp.where(upper, prev_tot, next_tot)
        m *= 2
        cs[m], tots[m] = cum, tot
    bcum, blast = cs[c], tots[c]

    head_masks = [lane // DK_C == h for h in range(w // DK_C)]
    n_heads = H_C

    rq = lax.broadcasted_iota(jnp.int32, (c, c), 0)
    rk = lax.broadcasted_iota(jnp.int32, (c, c), 1)
    att = jnp.zeros((n_heads * c, c), F32)
    half = GROUP
    while half < c:
        upper = (row // half) % 2 == 1
        qt = jnp.where(upper, q * jnp.exp(cs[half]), 0.0)
        kt = jnp.where(upper, 0.0, k * jnp.exp(tots[half] - cs[half])).astype(BF16)
        lhs = jnp.concatenate([jnp.where(head_masks[h], qt, 0.0) for h in range(n_heads)], axis=0)
        blk = _dot_nt(lhs.astype(BF16), kt)
        same = (rq // (2 * half)) == (rk // (2 * half))
        same = jnp.concatenate([same] * n_heads, axis=0)
        att = att + jnp.where(same, blk, 0.0)
        half *= 2
    res = _dot(att.astype(BF16), v.astype(BF16))
    o = jnp.zeros((c, w), F32)
    for h in range(n_heads):
        o = o + jnp.where(head_masks[h], res[h * c:(h + 1) * c], 0.0)

    def group_row(x, j):
        x3 = x.reshape(c // GROUP, GROUP, w)
        return jnp.broadcast_to(x3[:, j:j + 1, :], x3.shape).reshape(c, w)

    c8 = cs[GROUP]
    vals = []
    for j in range(GROUP):
        ok = (row % GROUP) >= j
        e = jnp.where(ok, c8 - group_row(c8, j), 0.0)
        vals.append(jnp.where(ok, q * group_row(k, j) * jnp.exp(e), 0.0))
    r_i = lax.broadcasted_iota(jnp.int32, (w, w), 0)
    c_i = lax.broadcasted_iota(jnp.int32, (w, w), 1)
    head_sum = jnp.where(r_i // DK_C == c_i // DK_C, 1.0, 0.0).astype(BF16)
    wts = _dot(jnp.concatenate(vals, axis=0).astype(BF16), head_sum)
    for j in range(GROUP):
        o = o + wts[j * c:(j + 1) * c] * group_row(v, j)

    qe = (q * jnp.exp(bcum)).astype(BF16)
    k2 = (k * jnp.exp(blast - bcum)).astype(BF16)
    decay = jnp.exp(blast[0:1, :])
    pr = lax.broadcasted_iota(jnp.int32, (LANES, LANES), 0)
    pc = lax.broadcasted_iota(jnp.int32, (LANES, LANES), 1)
    diag = pr // DK_C == pc // DK_C
    o_state = []
    for p in range(w // LANES):
        sl = slice(p * LANES, (p + 1) * LANES)
        st = st_ref[p]
        o_state.append(_dot_nt(qe[:, sl], st.astype(BF16)))
        upd = _dot(v[:, sl].T.astype(BF16), k2[:, sl])
        st_ref[p] = st * decay[:, sl] + jnp.where(diag, upd, 0.0)
    o_ref[...] = o + jnp.concatenate(o_state, axis=-1)

    @pl.when(ci == pl.num_programs(1) - 1)
    def _():
        s_out_ref[...] = st_ref[...]


def _hgrn2(q, k, g, v, s0, *, c):
    b, t, w = q.shape
    kern = functools.partial(_hgrn2_kernel, c=c)
    blk = pl.BlockSpec((None, c, w), lambda bi, ci: (bi, ci, 0))
    sblk = pl.BlockSpec((None, w // LANES, LANES, LANES), lambda bi, ci: (bi, 0, 0, 0))
    return pl.pallas_call(
        kern,
        grid=(b, t // c),
        in_specs=[blk, blk, blk, blk, sblk],
        out_specs=[blk, sblk],
        out_shape=[jax.ShapeDtypeStruct((b, t, w), F32),
                   jax.ShapeDtypeStruct((b, w // LANES, LANES, LANES), F32)],
        scratch_shapes=[pltpu.VMEM((w // LANES, LANES, LANES), F32)],
        compiler_params=_cparams(2),
        name="mixer_c",
    )(q, k, g, v, s0)


def _state_to_pairs(s):
    b = s.shape[0]
    st = jnp.swapaxes(s.astype(F32), -1, -2)
    st = jnp.pad(st, ((0, 0), (0, 2 * NPAIR - H_C), (0, 0), (0, 0)))
    st = st.reshape(b, NPAIR, 2, DV_C, DK_C)
    eye = jnp.eye(2, dtype=F32)
    full = st[:, :, :, :, None, :] * eye[None, None, :, None, :, None]
    return full.reshape(b, NPAIR, 2 * DV_C, 2 * DK_C)


def _pairs_to_state(sp):
    b = sp.shape[0]
    s6 = sp.reshape(b, NPAIR, 2, DV_C, 2, DK_C)
    diag = jnp.stack([s6[:, :, a, :, a, :] for a in range(2)], axis=2)
    return jnp.swapaxes(diag.reshape(b, 2 * NPAIR, DV_C, DK_C)[:, :H_C], -1, -2)


FF_BLOCK = 256


def _head_norm(y, gain):
    w = y.shape[-1]
    r_i = lax.broadcasted_iota(jnp.int32, (w, w), 0)
    c_i = lax.broadcasted_iota(jnp.int32, (w, w), 1)
    head_sum = jnp.where(r_i // DV_A == c_i // DV_A, 1.0, 0.0).astype(BF16)
    y2 = y * y
    hi = y2.astype(BF16)
    lo = (y2 - hi.astype(F32)).astype(BF16)
    ms = (_dot(hi, head_sum) + _dot(lo, head_sum)) * (1.0 / DV_A)
    return y * lax.rsqrt(ms + EPS) * gain


def _merge_kernel(scal_ref, x_ref, oa_ref, ob_ref, oc_ref, cg_ref, hist_ref, ag_ref, cgn_ref, wo_ref,
                  n2_ref, wup_ref, cw_ref, cb_ref, wdn_ref, fn_ref, *out_and_scratch, d_ff, final):
    if final:
        x_out_ref, fc_ref, y_ref, carry_ref = out_and_scratch
    else:
        x_out_ref, fc_ref, carry_ref = out_and_scratch
    ti = pl.program_id(1)
    tm = x_ref.shape[0]

    @pl.when(ti == 0)
    def _():
        carry_ref[...] = hist_ref[...]

    oa = _head_norm(oa_ref[...], ag_ref[...]) * scal_ref[0]
    cg = cg_ref[...]
    oc = _head_norm(oc_ref[...], cgn_ref[...]) * (cg * jax.nn.sigmoid(cg))
    mixed = (_dot(oa.astype(BF16), wo_ref[0:WA, :])
             + _dot(ob_ref[...].astype(BF16), wo_ref[WA:WA + WBP, :])
             + _dot(oc.astype(BF16), wo_ref[WA + WBP:WA + 2 * WBP, :]))
    x = x_ref[...] + mixed

    ms = jnp.mean(x * x, axis=-1, keepdims=True)
    h = (x * lax.rsqrt(ms + EPS) * n2_ref[...]).astype(BF16)
    row = lax.broadcasted_iota(jnp.int32, (tm, FF_BLOCK), 0)
    acc = jnp.zeros(x.shape, F32)
    for cblk in range(d_ff // FF_BLOCK):
        sl = slice(cblk * FF_BLOCK, (cblk + 1) * FF_BLOCK)
        a = _dot(h, wup_ref[:, sl])
        gate = _dot(h, wup_ref[:, d_ff + cblk * FF_BLOCK:d_ff + (cblk + 1) * FF_BLOCK])
        prev2 = carry_ref[0:1, sl]
        prev1 = carry_ref[1:2, sl]
        a1 = jnp.where(row == 0, prev1, pltpu.roll(a, 1, 0))
        a2 = jnp.where(row == 0, prev2, jnp.where(row == 1, prev1, pltpu.roll(a, 2, 0)))
        conv = cb_ref[:, sl] + a2 * cw_ref[0:1, sl] + a1 * cw_ref[1:2, sl] + a * cw_ref[2:3, sl]
        act = conv * jax.nn.sigmoid(conv) * gate
        acc = acc + _dot(act.astype(BF16), wdn_ref[sl, :])
        carry_ref[:, sl] = a[tm - (CONV_W - 1):, :]
    x = x + acc
    x_out_ref[...] = x
    fc_ref[...] = carry_ref[...]
    if final:
        ms = jnp.mean(x * x, axis=-1, keepdims=True)
        y_ref[...] = x * lax.rsqrt(ms + EPS) * fn_ref[...]


def _merge_ffn(scal, x, oa, ob, oc, cg, hist, a_gain, c_gain, wo, n2, wup, cw, cb, wdn, fnorm, *, tm, final):
    b, t, d = x.shape
    d_ff = wdn.shape[0]
    kern = functools.partial(_merge_kernel, d_ff=d_ff, final=final)
    blk = lambda w: pl.BlockSpec((None, tm, w), lambda bi, ti: (bi, ti, 0))
    per_b = pl.BlockSpec((None, CONV_W - 1, d_ff), lambda bi, ti: (bi, 0, 0))
    out_specs = [blk(d), per_b]
    out_shape = [jax.ShapeDtypeStruct((b, t, d), F32), jax.ShapeDtypeStruct((b, CONV_W - 1, d_ff), F32)]
    if final:
        out_specs.append(blk(d))
        out_shape.append(jax.ShapeDtypeStruct((b, t, d), F32))
    return pl.pallas_call(
        kern,
        grid=(b, t // tm),
        in_specs=[pl.BlockSpec(memory_space=pltpu.SMEM), blk(d), blk(WA), blk(WBP), blk(WBP), blk(WBP), per_b,
                  _const_spec(a_gain.shape), _const_spec(c_gain.shape), _const_spec(wo.shape),
                  _const_spec(n2.shape), _const_spec(wup.shape), _const_spec(cw.shape),
                  _const_spec(cb.shape), _const_spec(wdn.shape), _const_spec(fnorm.shape)],
        out_specs=out_specs,
        out_shape=out_shape,
        scratch_shapes=[pltpu.VMEM((CONV_W - 1, d_ff), F32)],
        compiler_params=_cparams(2),
        name="merge_ffn",
    )(scal, x, oa, ob, oc, cg, hist, a_gain, c_gain, wo, n2, wup, cw, cb, wdn, fnorm)


def _pack_w_in(w):
    parts = jnp.split(w, np.cumsum(IN_SIZES)[:-1].tolist(), axis=-1)
    cols = []
    for (name, width, padded), part in zip(SEGS, parts):
        if name == "ik":
            part = jnp.concatenate([part, part], axis=-1)
            width = 2 * D_I
        cols.append(jnp.pad(part, ((0, 0), (0, padded - width))))
    return jnp.concatenate(cols, axis=-1).astype(BF16)


def _pack_w_out(w):
    wa, wb, wc = w[:WA], w[WA:WA + WB], w[WA + WB:]
    pad = lambda m: jnp.pad(m, ((0, WBP - WB), (0, 0)))
    return jnp.concatenate([wa, pad(wb), pad(wc)], axis=0).astype(BF16)


def _rope_table(pos, dim, width):
    inv_freq = ROPE_THETA ** (-jnp.arange(0, dim, 2, dtype=F32) / dim)
    ang = pos.astype(F32)[:, None] * inv_freq[None, :]
    reps = width // (dim // 2)
    return jnp.tile(jnp.cos(ang), (1, reps)), jnp.tile(jnp.sin(ang), (1, reps))


def _pad_rows(x, rows):
    return jnp.pad(x, ((0, 0), (0, rows - x.shape[1]), (0, 0)))


def _round_up(n, m):
    return (n + m - 1) // m * m


def kernel(x_prompt, x_sample, cache_a_k, cache_a_v, cache_b_k, cache_b_v, cache_b_kidx, state_c, state_ffn_conv, norm1, w_in, lam_q1, lam_k1, lam_q2, lam_k2, a_norm, c_lower, c_norm, w_out, norm2, ffn_up, ffn_conv_w, ffn_conv_b, ffn_down, final_norm):
    depth = w_in.shape[0]
    b_p, t_p, d = x_prompt.shape
    b_s, t_s, _ = x_sample.shape
    past = cache_a_k.shape[2]
    d_ff = ffn_down.shape[1]
    kv_s = past + t_s
    n_sel_p = min(TOPK_MAX, t_p // 4)
    n_sel_s = min(TOPK_MAX, kv_s // 4)
    tk = 256
    lk_s = _round_up(kv_s, tk)

    lb_soft = jax.nn.softmax(c_lower.astype(F32), axis=0)
    lower = jnp.cumsum(lb_soft, axis=0) - lb_soft[0]
    lower = jnp.pad(lower, ((0, 0), (0, WBP - WB)))

    tm_p = min(512, t_p)
    tm_s = min(256, b_s * t_s)
    pos_p = jnp.arange(t_p)
    pos_s = jnp.tile(past + jnp.arange(t_s), tm_s // t_s)
    tabs_p = _rope_table(pos_p, DA, WA) + _rope_table(pos_p, D_B, WBP)
    tabs_s = _rope_table(pos_s, DA, WA) + _rope_table(pos_s, D_B, WBP)

    xp, xs = x_prompt, x_sample
    outs_p = [[] for _ in range(7)]
    outs_s = [[] for _ in range(7)]
    y_p = y_s = None
    fnorm = final_norm.reshape(1, d)
    for l in range(depth):
        lam_init = 0.8 - 0.6 * math.exp(-0.3 * l)
        lam = (jnp.exp(jnp.sum(lam_q1[l].astype(F32) * lam_k1[l].astype(F32)))
               - jnp.exp(jnp.sum(lam_q2[l].astype(F32) * lam_k2[l].astype(F32))) + lam_init)
        lam_arr = lam.reshape(1).astype(F32)
        scal = jnp.full((1,), 1.0 - lam_init, F32)
        w_pack = _pack_w_in(w_in[l])
        wo = _pack_w_out(w_out[l])
        gain1 = norm1[l].reshape(1, d)
        gain2 = norm2[l].reshape(1, d)
        lb_row = lower[l].reshape(1, WBP)
        a_gain = jnp.tile(a_norm[l], H_A).reshape(1, WA)
        c_gain = jnp.pad(jnp.tile(c_norm[l], H_C), (0, WBP - WB)).reshape(1, WBP)
        wup = ffn_up[l].astype(BF16)
        wdn = ffn_down[l].astype(BF16)
        cw = ffn_conv_w[l]
        cb = ffn_conv_b[l].reshape(1, d_ff)
        final = l == depth - 1

        u = _in_projection(xp.reshape(b_p * t_p, d), gain1, w_pack, lb_row, tabs_p, tm_p)
        r3 = lambda a, b=b_p, t=t_p: a.reshape(b, t, a.shape[-1])
        oa = _attention_a(lam_arr, r3(u["aq"]), r3(u["ak16"]), r3(u["av16"]),
                          q_off=0, kv_len=t_p, tq=min(256, t_p), tk=min(512, t_p))
        ob = _attention_b(r3(u["bq"]), r3(u["iq"]), r3(u["iw"]), r3(u["bk16"]), r3(u["bv16"]), r3(u["ik2"]),
                          q_off=0, kv_len=t_p, tq=min(256, t_p), n_sel=n_sel_p)
        s0 = jnp.zeros((b_p, NPAIR, LANES, LANES), F32)
        oc, s_new = _hgrn2(r3(u["cq"]), r3(u["ck"]), r3(u["cgl"]), r3(u["cv"]), s0, c=CHUNK)
        hist0 = jnp.zeros((b_p, CONV_W - 1, d_ff), F32)
        res = _merge_ffn(scal, xp, oa, ob, oc, r3(u["cg"]), hist0, a_gain, c_gain, wo, gain2, wup, cw, cb,
                         wdn, fnorm, tm=min(512, t_p), final=final)
        xp, fc = res[0], res[1]
        if final:
            y_p = res[2]
        for lst, val in zip(outs_p, (u["ak"].reshape(b_p, t_p, H_A, 2 * DA), u["av"].reshape(b_p, t_p, H_A, DV_A),
                                     u["bk"].reshape(b_p, t_p, H_B, D_B), u["bv"].reshape(b_p, t_p, H_B, D_B),
                                     u["ik"].reshape(b_p, t_p, D_I), _pairs_to_state(s_new), fc)):
            lst.append(val)

        u = _in_projection(xs.reshape(b_s * t_s, d), gain1, w_pack, lb_row, tabs_s, tm_s)
        r3 = lambda a, b=b_s, t=t_s: a.reshape(b, t, a.shape[-1])

        def cat(c, new):
            c = c.reshape(b_s, past, -1).astype(new.dtype)
            c = jnp.pad(c, ((0, 0), (0, 0), (0, new.shape[-1] - c.shape[-1])))
            return _pad_rows(jnp.concatenate([c, r3(new)], axis=1), lk_s)

        kidx = cache_b_kidx[l]
        kidx2 = jnp.concatenate([kidx, kidx], axis=-1)
        oa = _attention_a(lam_arr, r3(u["aq"]), cat(cache_a_k[l], u["ak16"]), cat(cache_a_v[l], u["av16"]),
                          q_off=past, kv_len=kv_s, tq=t_s, tk=tk)
        ob = _attention_b(r3(u["bq"]), r3(u["iq"]), r3(u["iw"]), cat(cache_b_k[l], u["bk16"]),
                          cat(cache_b_v[l], u["bv16"]), cat(kidx2, u["ik2"]),
                          q_off=past, kv_len=kv_s, tq=t_s, n_sel=n_sel_s)
        oc, s_new = _hgrn2(r3(u["cq"]), r3(u["ck"]), r3(u["cgl"]), r3(u["cv"]), _state_to_pairs(state_c[l]),
                           c=t_s)
        res = _merge_ffn(scal, xs, oa, ob, oc, r3(u["cg"]), state_ffn_conv[l].astype(F32), a_gain, c_gain, wo,
                         gain2, wup, cw, cb, wdn, fnorm, tm=t_s, final=final)
        xs, fc = res[0], res[1]
        if final:
            y_s = res[2]
        for lst, val in zip(outs_s, (u["ak"].reshape(b_s, t_s, H_A, 2 * DA), u["av"].reshape(b_s, t_s, H_A, DV_A),
                                     u["bk"].reshape(b_s, t_s, H_B, D_B), u["bv"].reshape(b_s, t_s, H_B, D_B),
                                     u["ik"].reshape(b_s, t_s, D_I), _pairs_to_state(s_new), fc)):
            lst.append(val)

    return (y_p, y_s) + tuple(jnp.stack(v) for v in outs_p) + tuple(jnp.stack(v) for v in outs_s)
```

```python
import functools
import math

import jax
import jax.numpy as jnp
import numpy as np
from jax import lax
from jax.experimental import pallas as pl
from jax.experimental.pallas import tpu as pltpu

F32 = jnp.float32
BF16 = jnp.bfloat16

CHUNK = 64
ROPE_THETA = 10000.0
EPS = 1e-6
NEG_BIG = -1e30
LB_FLOOR = 1e-20
H_A, DA, DV_A = 6, 32, 64
H_B, D_B = 5, 64
H_I, D_I = 4, 64
TOPK_MAX = 256
H_C, DK_C, DV_C = 5, 64, 64
CONV_W = 3

LANES = 128
SUBLANES = 8
VMEM_LIMIT = 56 * 1024 * 1024

WA = H_A * 2 * DA
WB = H_B * D_B
WBP = 384
WIQ = H_I * D_I
NPAIR = WBP // LANES

SEGS = (("aq", WA, WA), ("ak", WA, WA), ("av", WA, WA),
        ("bq", WB, WBP), ("bk", WB, WBP), ("bv", WB, WBP),
        ("iq", WIQ, WIQ), ("ik", D_I, 2 * D_I), ("iw", H_I, LANES),
        ("cq", WB, WBP), ("cf", WB, WBP), ("ci", WB, WBP), ("cg", WB, WBP))
SEG_OFF = {}
_o = 0
for _n, _w, _p in SEGS:
    SEG_OFF[_n] = (_o, _p)
    _o += _p
W_PACK = _o
IN_SIZES = (WA, WA, WA, WB, WB, WB, WIQ, D_I, H_I, WB, WB, WB, WB)

LOG2E = math.log2(math.e)
NT_DIMS = (((1,), (1,)), ((), ()))


def _cparams(n_axes):
    return pltpu.CompilerParams(dimension_semantics=("arbitrary",) * n_axes,
                                vmem_limit_bytes=VMEM_LIMIT)


def _const_spec(shape):
    nd = len(shape)
    return pl.BlockSpec(shape, lambda *_: (0,) * nd, pipeline_mode=pl.Buffered(1))


def _dot(a, b):
    return jnp.dot(a, b, preferred_element_type=F32)


def _dot_nt(a, b):
    return lax.dot_general(a, b, NT_DIMS, preferred_element_type=F32)


def _rope(u, cos, sin, half):
    w = u.shape[-1]
    lane = lax.broadcasted_iota(jnp.int32, u.shape, 1)
    first = (lane % (2 * half)) < half
    rot = jnp.where(first, -pltpu.roll(u, w - half, 1), pltpu.roll(u, half, 1))
    return u * cos + rot * sin


def _inproj_kernel(x_ref, g_ref, w_ref, lb_ref, ca_ref, sa_ref, cb_ref, sb_ref,
                   aq_ref, ak_ref, av_ref, ak16_ref, av16_ref, bq_ref, bk_ref, bv_ref, bk16_ref, bv16_ref,
                   iq_ref, ik2_ref, ik_ref, iw_ref, cq_ref, ck_ref, cgl_ref, cv_ref, cg_ref):
    x = x_ref[...]
    ms = jnp.mean(x * x, axis=-1, keepdims=True)
    xn = (x * lax.rsqrt(ms + EPS) * g_ref[...]).astype(BF16)

    def seg(name):
        off, width = SEG_OFF[name]
        return _dot(xn, w_ref[:, off:off + width])

    ca, sa = ca_ref[...], sa_ref[...]
    cb, sb = cb_ref[...], sb_ref[...]
    aq_ref[...] = (_rope(seg("aq"), ca, sa, DA // 2) * (DA ** -0.5 * LOG2E)).astype(BF16)
    ak = _rope(seg("ak"), ca, sa, DA // 2)
    ak_ref[...] = ak
    ak16_ref[...] = ak.astype(BF16)
    av = seg("av")
    av_ref[...] = av
    av16_ref[...] = av.astype(BF16)
    bq_ref[...] = (_rope(seg("bq"), cb, sb, D_B // 2) * (D_B ** -0.5 * LOG2E)).astype(BF16)
    bk = _rope(seg("bk"), cb, sb, D_B // 2)
    bk_ref[...] = bk[:, :WB]
    bk16_ref[...] = bk.astype(BF16)
    bv = seg("bv")
    bv_ref[...] = bv[:, :WB]
    bv16_ref[...] = bv.astype(BF16)
    iq_ref[...] = (_rope(seg("iq"), cb[:, :WIQ], sb[:, :WIQ], D_I // 2) * (D_I ** -0.5)).astype(BF16)
    ik2 = _rope(seg("ik"), cb[:, :2 * D_I], sb[:, :2 * D_I], D_I // 2)
    ik2_ref[...] = ik2.astype(BF16)
    ik_ref[...] = ik2[:, :D_I]
    iw_ref[...] = seg("iw") * (H_I ** -0.5)

    cq = seg("cq")
    cq_ref[...] = cq * jax.nn.sigmoid(cq)
    z = seg("cf")
    lb = lb_ref[...]
    la = jnp.log(jnp.maximum(lb, LB_FLOOR))
    lsig = jnp.minimum(z, 0.0) - jnp.log1p(jnp.exp(-jnp.abs(z)))
    bb = jnp.log1p(-lb) + lsig
    cgl_ref[...] = jnp.maximum(la, bb) + jnp.log1p(jnp.exp(-jnp.abs(la - bb)))
    ck_ref[...] = (1.0 - lb) * jax.nn.sigmoid(-z)
    cv_ref[...] = seg("ci")
    cg_ref[...] = seg("cg")


def _in_projection(x2d, gain, w_pack, lb_row, tabs, tm):
    n = x2d.shape[0]
    ca, sa, cb, sb = tabs
    period = ca.shape[0] // tm
    row = lambda w: pl.BlockSpec((tm, w), lambda i: (i, 0))
    tab = lambda w: pl.BlockSpec((tm, w), lambda i: (i % period, 0))
    outs = (("aq", WA, BF16), ("ak", WA, F32), ("av", WA, F32), ("ak16", WA, BF16), ("av16", WA, BF16),
            ("bq", WBP, BF16), ("bk", WB, F32), ("bv", WB, F32), ("bk16", WBP, BF16), ("bv16", WBP, BF16),
            ("iq", WIQ, BF16), ("ik2", 2 * D_I, BF16), ("ik", D_I, F32), ("iw", LANES, F32),
            ("cq", WBP, F32), ("ck", WBP, F32), ("cgl", WBP, F32), ("cv", WBP, F32), ("cg", WBP, F32))
    res = pl.pallas_call(
        _inproj_kernel,
        grid=(n // tm,),
        in_specs=[row(x2d.shape[1]), _const_spec(gain.shape), _const_spec(w_pack.shape),
                  _const_spec(lb_row.shape), tab(WA), tab(WA), tab(WBP), tab(WBP)],
        out_specs=[row(w) for _, w, _ in outs],
        out_shape=[jax.ShapeDtypeStruct((n, w), dt) for _, w, dt in outs],
        compiler_params=_cparams(1),
        name="in_projection",
    )(x2d, gain, w_pack, lb_row, ca, sa, cb, sb)
    return dict(zip([o[0] for o in outs], res))


def _attn_a_kernel(lam_ref, q_ref, k_ref, v_ref, o_ref, q4_ref, m_ref, acc_ref, *, tq, tk, q_off, kv_len):
    qi = pl.program_id(2)
    q = q_ref[...]
    lane = lax.broadcasted_iota(jnp.int32, q.shape, 1)
    for i in range(4):
        q4_ref[i * tq:(i + 1) * tq, :] = jnp.where(lane // DA == i, q, jnp.zeros_like(q))
    m_ref[...] = jnp.full(m_ref.shape, NEG_BIG, F32)
    acc_ref[...] = jnp.zeros(acc_ref.shape, F32)

    q_first = q_off + qi * tq
    q_last = q_first + tq - 1
    n_full = jnp.minimum((q_first // CHUNK + 1) * CHUNK, kv_len) // tk
    lim = jnp.minimum((q_last // CHUNK + 1) * CHUNK, kv_len)
    n_tot = (lim + tk - 1) // tk

    def step(j, masked):
        start = pl.multiple_of(j * tk, tk)
        kb = k_ref[pl.ds(start, tk), :]
        vb = v_ref[pl.ds(start, tk), :]
        s = _dot_nt(q4_ref[...], kb)
        if masked:
            kpos = start + lax.broadcasted_iota(jnp.int32, (tq, tk), 1)
            qpos = q_first + lax.broadcasted_iota(jnp.int32, (tq, tk), 0)
            valid = (kpos // CHUNK <= qpos // CHUNK) & (kpos < kv_len)
            s = jnp.where(valid[None], s.reshape(4, tq, tk), NEG_BIG).reshape(4 * tq, tk)
        m_old = m_ref[...]
        m_new = jnp.maximum(m_old, jnp.max(s, axis=-1, keepdims=True))
        alpha = jnp.exp2(m_old - m_new)
        p = jnp.exp2(s - jnp.concatenate([m_new] * (tk // LANES), axis=1)).astype(BF16)
        lane_v = lax.broadcasted_iota(jnp.int32, vb.shape, 1)
        ones = jnp.ones_like(vb)
        pv = jnp.concatenate([_dot(p[0:2 * tq], jnp.where(lane_v < DV_A, vb, ones)),
                              _dot(p[2 * tq:4 * tq], jnp.where(lane_v < DV_A, ones, vb))], axis=0)
        acc_ref[...] = alpha * acc_ref[...] + pv
        m_ref[...] = m_new

    def full_body(j, c):
        step(j, False)
        return c

    def masked_body(j, c):
        step(j, True)
        return c

    lax.fori_loop(0, n_full, full_body, 0)
    lax.fori_loop(n_full, n_tot, masked_body, 0)

    lam = lam_ref[0]
    acc0 = acc_ref[0:2 * tq, :]
    acc1 = acc_ref[2 * tq:4 * tq, :]
    on0 = acc0 / acc0[:, DV_A:DV_A + 1]
    on1 = acc1 / acc1[:, 0:1]
    o0 = on0[0:tq] - lam * on0[tq:2 * tq]
    o1 = on1[0:tq] - lam * on1[tq:2 * tq]
    o_ref[...] = jnp.where(lane < DV_A, o0, o1)


def _attention_a(lam, q, k, v, *, q_off, kv_len, tq, tk):
    b, t, _ = q.shape
    lk = k.shape[1]
    kern = functools.partial(_attn_a_kernel, tq=tq, tk=tk, q_off=q_off, kv_len=kv_len)
    return pl.pallas_call(
        kern,
        grid=(b, WA // LANES, t // tq),
        in_specs=[pl.BlockSpec(memory_space=pltpu.SMEM),
                  pl.BlockSpec((None, tq, LANES), lambda bi, hi, qi: (bi, qi, hi)),
                  pl.BlockSpec((None, lk, LANES), lambda bi, hi, qi: (bi, 0, hi)),
                  pl.BlockSpec((None, lk, LANES), lambda bi, hi, qi: (bi, 0, hi))],
        out_specs=pl.BlockSpec((None, tq, LANES), lambda bi, hi, qi: (bi, qi, hi)),
        out_shape=jax.ShapeDtypeStruct((b, t, WA), F32),
        scratch_shapes=[pltpu.VMEM((4 * tq, LANES), BF16), pltpu.VMEM((4 * tq, LANES), F32),
                        pltpu.VMEM((4 * tq, LANES), F32)],
        compiler_params=_cparams(3),
        name="mixer_a",
    )(lam, q, k, v)


TIE_BLOCK = 256


def _attn_b_kernel(q_ref, iq_ref, iw_ref, k_ref, v_ref, ik_ref, o_ref, key_ref, bias_ref,
                   *, tq, q_off, kv_len, n_sel, variants):
    q_first = q_off + pl.program_id(1) * tq
    need_keys = jnp.minimum(((q_first + tq - 1) // CHUNK + 1) * CHUNK, kv_len)
    lo = 0
    for lk in variants:
        @pl.when((need_keys > lo) & (need_keys <= lk))
        def _(lk=lk):
            _attn_b_body(q_ref, iq_ref, iw_ref, k_ref, v_ref, ik_ref, o_ref, key_ref, bias_ref,
                         tq=tq, q_first=q_first, need_keys=need_keys, kv_len=kv_len, n_sel=n_sel, lk=lk)
        lo = lk


def _stack_heads(x, n_heads, width):
    lane = lax.broadcasted_iota(jnp.int32, x.shape, 1)
    return jnp.concatenate([jnp.where(lane // width == h, x, jnp.zeros_like(x)) for h in range(n_heads)],
                           axis=0)


def _attn_b_body(q_ref, iq_ref, iw_ref, k_ref, v_ref, ik_ref, o_ref, key_ref, bias_ref,
                 *, tq, q_first, need_keys, kv_len, n_sel, lk):
    ik2 = ik_ref[0:lk, :]
    iq = iq_ref[...]
    iw = iw_ref[...]

    score = jnp.zeros((tq, lk), F32)
    for pi in range(H_I // 2):
        y = _stack_heads(iq[:, LANES * pi:LANES * (pi + 1)], 2, D_I)
        d = jnp.maximum(_dot_nt(y, ik2), 0.0)
        score = score + iw[:, 2 * pi:2 * pi + 1] * d[0:tq] + iw[:, 2 * pi + 1:2 * pi + 2] * d[tq:2 * tq]

    kpos = lax.broadcasted_iota(jnp.int32, (tq, lk), 1)
    qpos = q_first + lax.broadcasted_iota(jnp.int32, (tq, lk), 0)
    valid = (kpos // CHUNK <= qpos // CHUNK) & (kpos < kv_len)
    score = jnp.where(valid, score, NEG_BIG)

    bits = lax.bitcast_convert_type(score, jnp.int32)
    key = bits ^ ((bits >> 31) & jnp.int32(0x7FFFFFFF))
    key = jnp.where(key == -1, 0, key)
    key_ref[:, 0:lk] = key

    kf = float(n_sel)

    def count_ge(cand):
        wide = jnp.concatenate([cand] * (lk // LANES), axis=1)
        return jnp.sum(jnp.where(key_ref[:, 0:lk] >= wide, 1.0, 0.0), axis=-1, keepdims=True)

    int_min = jnp.int32(-2 ** 31)
    zero = jnp.zeros((tq, LANES), jnp.int32)
    select_all = need_keys <= n_sel
    cur = jnp.where(select_all | (count_ge(zero) < kf), int_min, zero)

    def bit_body(i, cur):
        cand = cur | jnp.left_shift(jnp.int32(1), 30 - i)
        return jnp.where(count_ge(cand) >= kf, cand, cur)

    thr = lax.fori_loop(0, jnp.where(select_all, 0, 31), bit_body, cur)

    key = key_ref[:, 0:lk]
    thr = jnp.concatenate([thr] * (lk // LANES), axis=1)
    gt = key > thr
    eq = key == thr
    need = kf - jnp.sum(jnp.where(gt, 1.0, 0.0), axis=-1, keepdims=True)
    r_i = lax.broadcasted_iota(jnp.int32, (TIE_BLOCK, TIE_BLOCK), 0)
    c_i = lax.broadcasted_iota(jnp.int32, (TIE_BLOCK, TIE_BLOCK), 1)
    tri = jnp.where(r_i <= c_i, 1.0, 0.0).astype(BF16)
    carry = jnp.zeros((tq, 1), F32)
    for jb in range(lk // TIE_BLOCK):
        sl = slice(jb * TIE_BLOCK, (jb + 1) * TIE_BLOCK)
        eq_b = eq[:, sl]
        pref = _dot(jnp.where(eq_b, 1.0, 0.0).astype(BF16), tri) + carry
        carry = pref[:, TIE_BLOCK - 1:TIE_BLOCK]
        sel = (gt[:, sl] | (eq_b & (pref <= need))) & valid[:, sl]
        bias_ref[:, sl] = jnp.where(sel, 0.0, NEG_BIG)

    q = q_ref[...]
    lane = lax.broadcasted_iota(jnp.int32, (tq, LANES), 1)
    bias = bias_ref[:, 0:lk]
    for p in range(NPAIR):
        n_heads = min(2, H_B - 2 * p)
        sl = slice(p * LANES, (p + 1) * LANES)
        y = _stack_heads(q[:, sl], n_heads, D_B)
        s = _dot_nt(y, k_ref[0:lk, sl]).reshape(n_heads, tq, lk) + bias[None]
        m = jnp.max(s, axis=-1, keepdims=True)
        pr = jnp.exp2(s - m)
        l = jnp.sum(pr, axis=-1, keepdims=True)
        o = _dot(pr.reshape(n_heads * tq, lk).astype(BF16), v_ref[0:lk, sl]) / l.reshape(n_heads * tq, 1)
        o_ref[:, sl] = o if n_heads == 1 else jnp.where(lane < D_B, o[0:tq], o[tq:2 * tq])


def _attention_b(q, iq, iw, k, v, ik2, *, q_off, kv_len, tq, n_sel):
    b, t, _ = q.shape
    lk = k.shape[1]
    granule = 2 * TIE_BLOCK if lk % (2 * TIE_BLOCK) == 0 else lk
    variants = tuple(range(granule, lk + 1, granule))
    kern = functools.partial(_attn_b_kernel, tq=tq, q_off=q_off, kv_len=kv_len, n_sel=n_sel,
                             variants=variants)
    qspec = lambda w: pl.BlockSpec((None, tq, w), lambda bi, qi: (bi, qi, 0))
    kspec = lambda w: pl.BlockSpec((None, lk, w), lambda bi, qi: (bi, 0, 0))
    return pl.pallas_call(
        kern,
        grid=(b, t // tq),
        in_specs=[qspec(WBP), qspec(WIQ), qspec(LANES), kspec(WBP), kspec(WBP), kspec(2 * D_I)],
        out_specs=qspec(WBP),
        out_shape=jax.ShapeDtypeStruct((b, t, WBP), F32),
        scratch_shapes=[pltpu.VMEM((tq, lk), jnp.int32), pltpu.VMEM((tq, lk), F32)],
        compiler_params=_cparams(2),
        name="mixer_b",
    )(q, iq, iw, k, v, ik2)


GROUP = SUBLANES


def _hgrn2_kernel(q_ref, k_ref, g_ref, v_ref, s0_ref, o_ref, s_out_ref, st_ref, *, c):
    ci = pl.program_id(1)

    @pl.when(ci == 0)
    def _():
        st_ref[...] = s0_ref[...]

    q = q_ref[...]
    k = k_ref[...]
    g = g_ref[...]
    v = v_ref[...]
    w = q.shape[-1]
    row = lax.broadcasted_iota(jnp.int32, (c, w), 0)
    lane = lax.broadcasted_iota(jnp.int32, (c, w), 1)

    cs, tots = {1: g}, {1: g}
    cum, tot, m = g, g, 1
    while m < c:
        upper = (row // m) % 2 == 1
        prev_tot = pltpu.roll(tot, m, 0)
        next_tot = pltpu.roll(tot, c - m, 0)
        cum = cum + jnp.where(upper, prev_tot, 0.0)
        tot = tot + jnp.where(upper, prev_tot, next_tot)
        m *= 2
        cs[m], tots[m] = cum, tot
    bcum, blast = cs[c], tots[c]

    head_masks = [lane // DK_C == h for h in range(w // DK_C)]
    n_heads = H_C

    rq = lax.broadcasted_iota(jnp.int32, (c, c), 0)
    rk = lax.broadcasted_iota(jnp.int32, (c, c), 1)
    att = jnp.zeros((n_heads * c, c), F32)
    half = GROUP
    while half < c:
        upper = (row // half) % 2 == 1
        qt = jnp.where(upper, q * jnp.exp(cs[half]), 0.0)
        kt = jnp.where(upper, 0.0, k * jnp.exp(tots[half] - cs[half])).astype(BF16)
        lhs = jnp.concatenate([jnp.where(head_masks[h], qt, 0.0) for h in range(n_heads)], axis=0)
        blk = _dot_nt(lhs.astype(BF16), kt)
        same = (rq // (2 * half)) == (rk // (2 * half))
        same = jnp.concatenate([same] * n_heads, axis=0)
        att = att + jnp.where(same, blk, 0.0)
        half *= 2
    res = _dot(att.astype(BF16), v.astype(BF16))
    o = jnp.zeros((c, w), F32)
    for h in range(n_heads):
        o = o + jnp.where(head_masks[h], res[h * c:(h + 1) * c], 0.0)

    def group_row(x, j):
        x3 = x.reshape(c // GROUP, GROUP, w)
        return jnp.broadcast_to(x3[:, j:j + 1, :], x3.shape).reshape(c, w)

    c8 = cs[GROUP]
    vals = []
    for j in range(GROUP):
        ok = (row % GROUP) >= j
        e = jnp.where(ok, c8 - group_row(c8, j), 0.0)
        vals.append(jnp.where(ok, q * group_row(k, j) * jnp.exp(e), 0.0))
    r_i = lax.broadcasted_iota(jnp.int32, (w, w), 0)
    c_i = lax.broadcasted_iota(jnp.int32, (w, w), 1)
    head_sum = jnp.where(r_i // DK_C == c_i // DK_C, 1.0, 0.0).astype(BF16)
    wts = _dot(jnp.concatenate(vals, axis=0).astype(BF16), head_sum)
    for j in range(GROUP):
        o = o + wts[j * c:(j + 1) * c] * group_row(v, j)

    qe = (q * jnp.exp(bcum)).astype(BF16)
    k2 = (k * jnp.exp(blast - bcum)).astype(BF16)
    decay = jnp.exp(blast[0:1, :])
    pr = lax.broadcasted_iota(jnp.int32, (LANES, LANES), 0)
    pc = lax.broadcasted_iota(jnp.int32, (LANES, LANES), 1)
    diag = pr // DK_C == pc // DK_C
    o_state = []
    for p in range(w // LANES):
        sl = slice(p * LANES, (p + 1) * LANES)
        st = st_ref[p]
        o_state.append(_dot_nt(qe[:, sl], st.astype(BF16)))
        upd = _dot(v[:, sl].T.astype(BF16), k2[:, sl])
        st_ref[p] = st * decay[:, sl] + jnp.where(diag, upd, 0.0)
    o_ref[...] = o + jnp.concatenate(o_state, axis=-1)

    @pl.when(ci == pl.num_programs(1) - 1)
    def _():
        s_out_ref[...] = st_ref[...]


def _hgrn2(q, k, g, v, s0, *, c):
    b, t, w = q.shape
    kern = functools.partial(_hgrn2_kernel, c=c)
    blk = pl.BlockSpec((None, c, w), lambda bi, ci: (bi, ci, 0))
    sblk = pl.BlockSpec((None, w // LANES, LANES, LANES), lambda bi, ci: (bi, 0, 0, 0))
    return pl.pallas_call(
        kern,
        grid=(b, t // c),
        in_specs=[blk, blk, blk, blk, sblk],
        out_specs=[blk, sblk],
        out_shape=[jax.ShapeDtypeStruct((b, t, w), F32),
                   jax.ShapeDtypeStruct((b, w // LANES, LANES, LANES), F32)],
        scratch_shapes=[pltpu.VMEM((w // LANES, LANES, LANES), F32)],
        compiler_params=_cparams(2),
        name="mixer_c",
    )(q, k, g, v, s0)


def _state_to_pairs(s):
    b = s.shape[0]
    st = jnp.swapaxes(s.astype(F32), -1, -2)
    st = jnp.pad(st, ((0, 0), (0, 2 * NPAIR - H_C), (0, 0), (0, 0)))
    st = st.reshape(b, NPAIR, 2, DV_C, DK_C)
    eye = jnp.eye(2, dtype=F32)
    full = st[:, :, :, :, None, :] * eye[None, None, :, None, :, None]
    return full.reshape(b, NPAIR, 2 * DV_C, 2 * DK_C)


def _pairs_to_state(sp):
    b = sp.shape[0]
    s6 = sp.reshape(b, NPAIR, 2, DV_C, 2, DK_C)
    diag = jnp.stack([s6[:, :, a, :, a, :] for a in range(2)], axis=2)
    return jnp.swapaxes(diag.reshape(b, 2 * NPAIR, DV_C, DK_C)[:, :H_C], -1, -2)


FF_BLOCK = 256


def _head_norm(y, gain):
    w = y.shape[-1]
    r_i = lax.broadcasted_iota(jnp.int32, (w, w), 0)
    c_i = lax.broadcasted_iota(jnp.int32, (w, w), 1)
    head_sum = jnp.where(r_i // DV_A == c_i // DV_A, 1.0, 0.0).astype(BF16)
    y2 = y * y
    hi = y2.astype(BF16)
    lo = (y2 - hi.astype(F32)).astype(BF16)
    ms = (_dot(hi, head_sum) + _dot(lo, head_sum)) * (1.0 / DV_A)
    return y * lax.rsqrt(ms + EPS) * gain


def _merge_kernel(scal_ref, x_ref, oa_ref, ob_ref, oc_ref, cg_ref, hist_ref, ag_ref, cgn_ref, wo_ref,
                  n2_ref, wup_ref, cw_ref, cb_ref, wdn_ref, fn_ref, *out_and_scratch, d_ff, final):
    if final:
        x_out_ref, fc_ref, y_ref, carry_ref = out_and_scratch
    else:
        x_out_ref, fc_ref, carry_ref = out_and_scratch
    ti = pl.program_id(1)
    tm = x_ref.shape[0]

    @pl.when(ti == 0)
    def _():
        carry_ref[...] = hist_ref[...]

    oa = _head_norm(oa_ref[...], ag_ref[...]) * scal_ref[0]
    cg = cg_ref[...]
    oc = _head_norm(oc_ref[...], cgn_ref[...]) * (cg * jax.nn.sigmoid(cg))
    mixed = (_dot(oa.astype(BF16), wo_ref[0:WA, :])
             + _dot(ob_ref[...].astype(BF16), wo_ref[WA:WA + WBP, :])
             + _dot(oc.astype(BF16), wo_ref[WA + WBP:WA + 2 * WBP, :]))
    x = x_ref[...] + mixed

    ms = jnp.mean(x * x, axis=-1, keepdims=True)
    h = (x * lax.rsqrt(ms + EPS) * n2_ref[...]).astype(BF16)
    row = lax.broadcasted_iota(jnp.int32, (tm, FF_BLOCK), 0)
    acc = jnp.zeros(x.shape, F32)
    for cblk in range(d_ff // FF_BLOCK):
        sl = slice(cblk * FF_BLOCK, (cblk + 1) * FF_BLOCK)
        a = _dot(h, wup_ref[:, sl])
        gate = _dot(h, wup_ref[:, d_ff + cblk * FF_BLOCK:d_ff + (cblk + 1) * FF_BLOCK])
        prev2 = carry_ref[0:1, sl]
        prev1 = carry_ref[1:2, sl]
        a1 = jnp.where(row == 0, prev1, pltpu.roll(a, 1, 0))
        a2 = jnp.where(row == 0, prev2, jnp.where(row == 1, prev1, pltpu.roll(a, 2, 0)))
        conv = cb_ref[:, sl] + a2 * cw_ref[0:1, sl] + a1 * cw_ref[1:2, sl] + a * cw_ref[2:3, sl]
        act = conv * jax.nn.sigmoid(conv) * gate
        acc = acc + _dot(act.astype(BF16), wdn_ref[sl, :])
        carry_ref[:, sl] = a[tm - (CONV_W - 1):, :]
    x = x + acc
    x_out_ref[...] = x
    fc_ref[...] = carry_ref[...]
    if final:
        ms = jnp.mean(x * x, axis=-1, keepdims=True)
        y_ref[...] = x * lax.rsqrt(ms + EPS) * fn_ref[...]


def _merge_ffn(scal, x, oa, ob, oc, cg, hist, a_gain, c_gain, wo, n2, wup, cw, cb, wdn, fnorm, *, tm, final):
    b, t, d = x.shape
    d_ff = wdn.shape[0]
    kern = functools.partial(_merge_kernel, d_ff=d_ff, final=final)
    blk = lambda w: pl.BlockSpec((None, tm, w), lambda bi, ti: (bi, ti, 0))
    per_b = pl.BlockSpec((None, CONV_W - 1, d_ff), lambda bi, ti: (bi, 0, 0))
    out_specs = [blk(d), per_b]
    out_shape = [jax.ShapeDtypeStruct((b, t, d), F32), jax.ShapeDtypeStruct((b, CONV_W - 1, d_ff), F32)]
    if final:
        out_specs.append(blk(d))
        out_shape.append(jax.ShapeDtypeStruct((b, t, d), F32))
    return pl.pallas_call(
        kern,
        grid=(b, t // tm),
        in_specs=[pl.BlockSpec(memory_space=pltpu.SMEM), blk(d), blk(WA), blk(WBP), blk(WBP), blk(WBP), per_b,
                  _const_spec(a_gain.shape), _const_spec(c_gain.shape), _const_spec(wo.shape),
                  _const_spec(n2.shape), _const_spec(wup.shape), _const_spec(cw.shape),
                  _const_spec(cb.shape), _const_spec(wdn.shape), _const_spec(fnorm.shape)],
        out_specs=out_specs,
        out_shape=out_shape,
        scratch_shapes=[pltpu.VMEM((CONV_W - 1, d_ff), F32)],
        compiler_params=_cparams(2),
        name="merge_ffn",
    )(scal, x, oa, ob, oc, cg, hist, a_gain, c_gain, wo, n2, wup, cw, cb, wdn, fnorm)


def _pack_w_in(w):
    parts = jnp.split(w, np.cumsum(IN_SIZES)[:-1].tolist(), axis=-1)
    cols = []
    for (name, width, padded), part in zip(SEGS, parts):
        if name == "ik":
            part = jnp.concatenate([part, part], axis=-1)
            width = 2 * D_I
        cols.append(jnp.pad(part, ((0, 0), (0, padded - width))))
    return jnp.concatenate(cols, axis=-1).astype(BF16)


def _pack_w_out(w):
    wa, wb, wc = w[:WA], w[WA:WA + WB], w[WA + WB:]
    pad = lambda m: jnp.pad(m, ((0, WBP - WB), (0, 0)))
    return jnp.concatenate([wa, pad(wb), pad(wc)], axis=0).astype(BF16)


def _rope_table(pos, dim, width):
    inv_freq = ROPE_THETA ** (-jnp.arange(0, dim, 2, dtype=F32) / dim)
    ang = pos.astype(F32)[:, None] * inv_freq[None, :]
    reps = width // (dim // 2)
    return jnp.tile(jnp.cos(ang), (1, reps)), jnp.tile(jnp.sin(ang), (1, reps))


def _pad_rows(x, rows):
    return jnp.pad(x, ((0, 0), (0, rows - x.shape[1]), (0, 0)))


def _round_up(n, m):
    return (n + m - 1) // m * m


def kernel(x_prompt, x_sample, cache_a_k, cache_a_v, cache_b_k, cache_b_v, cache_b_kidx, state_c, state_ffn_conv, norm1, w_in, lam_q1, lam_k1, lam_q2, lam_k2, a_norm, c_lower, c_norm, w_out, norm2, ffn_up, ffn_conv_w, ffn_conv_b, ffn_down, final_norm):
    depth = w_in.shape[0]
    b_p, t_p, d = x_prompt.shape
    b_s, t_s, _ = x_sample.shape
    past = cache_a_k.shape[2]
    d_ff = ffn_down.shape[1]
    kv_s = past + t_s
    n_sel_p = min(TOPK_MAX, t_p // 4)
    n_sel_s = min(TOPK_MAX, kv_s // 4)
    tk = 256
    lk_s = _round_up(kv_s, tk)

    lb_soft = jax.nn.softmax(c_lower.astype(F32), axis=0)
    lower = jnp.cumsum(lb_soft, axis=0) - lb_soft[0]
    lower = jnp.pad(lower, ((0, 0), (0, WBP - WB)))

    tm_p = min(512, t_p)
    tm_s = min(256, b_s * t_s)
    pos_p = jnp.arange(t_p)
    pos_s = jnp.tile(past + jnp.arange(t_s), tm_s // t_s)
    tabs_p = _rope_table(pos_p, DA, WA) + _rope_table(pos_p, D_B, WBP)
    tabs_s = _rope_table(pos_s, DA, WA) + _rope_table(pos_s, D_B, WBP)

    xp, xs = x_prompt, x_sample
    outs_p = [[] for _ in range(7)]
    outs_s = [[] for _ in range(7)]
    y_p = y_s = None
    fnorm = final_norm.reshape(1, d)
    for l in range(depth):
        lam_init = 0.8 - 0.6 * math.exp(-0.3 * l)
        lam = (jnp.exp(jnp.sum(lam_q1[l].astype(F32) * lam_k1[l].astype(F32)))
               - jnp.exp(jnp.sum(lam_q2[l].astype(F32) * lam_k2[l].astype(F32))) + lam_init)
        lam_arr = lam.reshape(1).astype(F32)
        scal = jnp.full((1,), 1.0 - lam_init, F32)
        w_pack = _pack_w_in(w_in[l])
        wo = _pack_w_out(w_out[l])
        gain1 = norm1[l].reshape(1, d)
        gain2 = norm2[l].reshape(1, d)
        lb_row = lower[l].reshape(1, WBP)
        a_gain = jnp.tile(a_norm[l], H_A).reshape(1, WA)
        c_gain = jnp.pad(jnp.tile(c_norm[l], H_C), (0, WBP - WB)).reshape(1, WBP)
        wup = ffn_up[l].astype(BF16)
        wdn = ffn_down[l].astype(BF16)
        cw = ffn_conv_w[l]
        cb = ffn_conv_b[l].reshape(1, d_ff)
        final = l == depth - 1

        u = _in_projection(xp.reshape(b_p * t_p, d), gain1, w_pack, lb_row, tabs_p, tm_p)
        r3 = lambda a, b=b_p, t=t_p: a.reshape(b, t, a.shape[-1])
        oa = _attention_a(lam_arr, r3(u["aq"]), r3(u["ak16"]), r3(u["av16"]),
                          q_off=0, kv_len=t_p, tq=min(256, t_p), tk=min(512, t_p))
        ob = _attention_b(r3(u["bq"]), r3(u["iq"]), r3(u["iw"]), r3(u["bk16"]), r3(u["bv16"]), r3(u["ik2"]),
                          q_off=0, kv_len=t_p, tq=min(256, t_p), n_sel=n_sel_p)
        s0 = jnp.zeros((b_p, NPAIR, LANES, LANES), F32)
        oc, s_new = _hgrn2(r3(u["cq"]), r3(u["ck"]), r3(u["cgl"]), r3(u["cv"]), s0, c=CHUNK)
        hist0 = jnp.zeros((b_p, CONV_W - 1, d_ff), F32)
        res = _merge_ffn(scal, xp, oa, ob, oc, r3(u["cg"]), hist0, a_gain, c_gain, wo, gain2, wup, cw, cb,
                         wdn, fnorm, tm=min(512, t_p), final=final)
        xp, fc = res[0], res[1]
        if final:
            y_p = res[2]
        for lst, val in zip(outs_p, (u["ak"].reshape(b_p, t_p, H_A, 2 * DA), u["av"].reshape(b_p, t_p, H_A, DV_A),
                                     u["bk"].reshape(b_p, t_p, H_B, D_B), u["bv"].reshape(b_p, t_p, H_B, D_B),
                                     u["ik"].reshape(b_p, t_p, D_I), _pairs_to_state(s_new), fc)):
            lst.append(val)

        u = _in_projection(xs.reshape(b_s * t_s, d), gain1, w_pack, lb_row, tabs_s, tm_s)
        r3 = lambda a, b=b_s, t=t_s: a.reshape(b, t, a.shape[-1])

        def cat(c, new):
            c = c.reshape(b_s, past, -1).astype(new.dtype)
            c = jnp.pad(c, ((0, 0), (0, 0), (0, new.shape[-1] - c.shape[-1])))
            return _pad_rows(jnp.concatenate([c, r3(new)], axis=1), lk_s)

        kidx = cache_b_kidx[l]
        kidx2 = jnp.concatenate([kidx, kidx], axis=-1)
        oa = _attention_a(lam_arr, r3(u["aq"]), cat(cache_a_k[l], u["ak16"]), cat(cache_a_v[l], u["av16"]),
                          q_off=past, kv_len=kv_s, tq=t_s, tk=tk)
        ob = _attention_b(r3(u["bq"]), r3(u["iq"]), r3(u["iw"]), cat(cache_b_k[l], u["bk16"]),
                          cat(cache_b_v[l], u["bv16"]), cat(kidx2, u["ik2"]),
                          q_off=past, kv_len=kv_s, tq=t_s, n_sel=n_sel_s)
        oc, s_new = _hgrn2(r3(u["cq"]), r3(u["ck"]), r3(u["cgl"]), r3(u["cv"]), _state_to_pairs(state_c[l]),
                           c=t_s)
        res = _merge_ffn(scal, xs, oa, ob, oc, r3(u["cg"]), state_ffn_conv[l].astype(F32), a_gain, c_gain, wo,
                         gain2, wup, cw, cb, wdn, fnorm, tm=t_s, final=final)
        xs, fc = res[0], res[1]
        if final:
            y_s = res[2]
        for lst, val in zip(outs_s, (u["ak"].reshape(b_s, t_s, H_A, 2 * DA), u["av"].reshape(b_s, t_s, H_A, DV_A),
                                     u["bk"].reshape(b_s, t_s, H_B, D_B), u["bv"].reshape(b_s, t_s, H_B, D_B),
                                     u["ik"].reshape(b_s, t_s, D_I), _pairs_to_state(s_new), fc)):
            lst.append(val)

    return (y_p, y_s) + tuple(jnp.stack(v) for v in outs_p) + tuple(jnp.stack(v) for v in outs_s)
```

```python
import functools
import math

import jax
import jax.numpy as jnp
import numpy as np
from jax import lax
from jax.experimental import pallas as pl
from jax.experimental.pallas import tpu as pltpu

F32 = jnp.float32
BF16 = jnp.bfloat16

CHUNK = 64
ROPE_THETA = 10000.0
EPS = 1e-6
NEG_BIG = -1e30
LB_FLOOR = 1e-20
H_A, DA, DV_A = 6, 32, 64
H_B, D_B = 5, 64
H_I, D_I = 4, 64
TOPK_MAX = 256
H_C, DK_C, DV_C = 5, 64, 64
CONV_W = 3

LANES = 128
SUBLANES = 8
VMEM_LIMIT = 56 * 1024 * 1024

WA = H_A * 2 * DA
WB = H_B * D_B
WBP = 384
WIQ = H_I * D_I
NPAIR = WBP // LANES

SEGS = (("aq", WA, WA), ("ak", WA, WA), ("av", WA, WA),
        ("bq", WB, WBP), ("bk", WB, WBP), ("bv", WB, WBP),
        ("iq", WIQ, WIQ), ("ik", D_I, 2 * D_I), ("iw", H_I, LANES),
        ("cq", WB, WBP), ("cf", WB, WBP), ("ci", WB, WBP), ("cg", WB, WBP))
SEG_OFF = {}
_o = 0
for _n, _w, _p in SEGS:
    SEG_OFF[_n] = (_o, _p)
    _o += _p
W_PACK = _o
PROJ_GROUPS = tuple((SEG_OFF[first][0], SEG_OFF[last][0] + SEG_OFF[last][1] - SEG_OFF[first][0])
                    for first, last in (("aq", "ak"), ("av", "bq"), ("bk", "bv"), ("iq", "iw"),
                                        ("cq", "cf"), ("ci", "cg")))
IN_SIZES = (WA, WA, WA, WB, WB, WB, WIQ, D_I, H_I, WB, WB, WB, WB)

LOG2E = math.log2(math.e)
NT_DIMS = (((1,), (1,)), ((), ()))


def _cparams(n_axes):
    return pltpu.CompilerParams(dimension_semantics=("arbitrary",) * n_axes,
                                vmem_limit_bytes=VMEM_LIMIT)


def _const_spec(shape):
    nd = len(shape)
    return pl.BlockSpec(shape, lambda *_: (0,) * nd, pipeline_mode=pl.Buffered(1))


def _dot(a, b):
    return jnp.dot(a, b, preferred_element_type=F32)


def _dot_nt(a, b):
    return lax.dot_general(a, b, NT_DIMS, preferred_element_type=F32)


def _rope(u, cos, sin, half):
    w = u.shape[-1]
    lane = lax.broadcasted_iota(jnp.int32, u.shape, 1)
    first = (lane % (2 * half)) < half
    rot = jnp.where(first, -pltpu.roll(u, w - half, 1), pltpu.roll(u, half, 1))
    return u * cos + rot * sin


def _inproj_kernel(x_ref, g_ref, w_ref, lb_ref, ca_ref, sa_ref, cb_ref, sb_ref,
                   aq_ref, ak_ref, av_ref, ak16_ref, av16_ref, bq_ref, bk_ref, bv_ref, bk16_ref, bv16_ref,
                   iq_ref, ik2_ref, ik_ref, iw_ref, cq_ref, ck_ref, cgl_ref, cv_ref, cg_ref):
    x = x_ref[...]
    ms = jnp.mean(x * x, axis=-1, keepdims=True)
    xn = (x * lax.rsqrt(ms + EPS) * g_ref[...]).astype(BF16)

    group_dots = {}

    def seg(name):
        off, width = SEG_OFF[name]
        g_off, g_width = next((o, wd) for o, wd in PROJ_GROUPS if o <= off < o + wd)
        if g_off not in group_dots:
            group_dots[g_off] = _dot(xn, w_ref[:, g_off:g_off + g_width])
        return group_dots[g_off][:, off - g_off:off - g_off + width]

    ca, sa = ca_ref[...], sa_ref[...]
    cb, sb = cb_ref[...], sb_ref[...]
    aq_ref[...] = (_rope(seg("aq"), ca, sa, DA // 2) * (DA ** -0.5 * LOG2E)).astype(BF16)
    ak = _rope(seg("ak"), ca, sa, DA // 2)
    ak_ref[...] = ak
    ak16_ref[...] = ak.astype(BF16)
    av = seg("av")
    av_ref[...] = av
    av16_ref[...] = av.astype(BF16)
    bq_ref[...] = (_rope(seg("bq"), cb, sb, D_B // 2) * (D_B ** -0.5 * LOG2E)).astype(BF16)
    bk = _rope(seg("bk"), cb, sb, D_B // 2)
    bk_ref[...] = bk[:, :WB]
    bk16_ref[...] = bk.astype(BF16)
    bv = seg("bv")
    bv_ref[...] = bv[:, :WB]
    bv16_ref[...] = bv.astype(BF16)
    iq_ref[...] = (_rope(seg("iq"), cb[:, :WIQ], sb[:, :WIQ], D_I // 2) * (D_I ** -0.5)).astype(BF16)
    ik2 = _rope(seg("ik"), cb[:, :2 * D_I], sb[:, :2 * D_I], D_I // 2)
    ik2_ref[...] = ik2.astype(BF16)
    ik_ref[...] = ik2[:, :D_I]
    iw_ref[...] = seg("iw") * (H_I ** -0.5)

    cq = seg("cq")
    cq_ref[...] = cq * jax.nn.sigmoid(cq)
    z = seg("cf")
    lb = lb_ref[...]
    la = jnp.log(jnp.maximum(lb, LB_FLOOR))
    lsig = jnp.minimum(z, 0.0) - jnp.log1p(jnp.exp(-jnp.abs(z)))
    bb = jnp.log1p(-lb) + lsig
    cgl_ref[...] = jnp.maximum(la, bb) + jnp.log1p(jnp.exp(-jnp.abs(la - bb)))
    ck_ref[...] = (1.0 - lb) * jax.nn.sigmoid(-z)
    cv_ref[...] = seg("ci")
    cg_ref[...] = seg("cg")


def _in_projection(x2d, gain, w_pack, lb_row, tabs, tm):
    n = x2d.shape[0]
    ca, sa, cb, sb = tabs
    period = ca.shape[0] // tm
    row = lambda w: pl.BlockSpec((tm, w), lambda i: (i, 0))
    tab = lambda w: pl.BlockSpec((tm, w), lambda i: (i % period, 0))
    outs = (("aq", WA, BF16), ("ak", WA, F32), ("av", WA, F32), ("ak16", WA, BF16), ("av16", WA, BF16),
            ("bq", WBP, BF16), ("bk", WB, F32), ("bv", WB, F32), ("bk16", WBP, BF16), ("bv16", WBP, BF16),
            ("iq", WIQ, BF16), ("ik2", 2 * D_I, BF16), ("ik", D_I, F32), ("iw", LANES, F32),
            ("cq", WBP, F32), ("ck", WBP, F32), ("cgl", WBP, F32), ("cv", WBP, F32), ("cg", WBP, F32))
    res = pl.pallas_call(
        _inproj_kernel,
        grid=(n // tm,),
        in_specs=[row(x2d.shape[1]), _const_spec(gain.shape), _const_spec(w_pack.shape),
                  _const_spec(lb_row.shape), tab(WA), tab(WA), tab(WBP), tab(WBP)],
        out_specs=[row(w) for _, w, _ in outs],
        out_shape=[jax.ShapeDtypeStruct((n, w), dt) for _, w, dt in outs],
        compiler_params=_cparams(1),
        name="in_projection",
    )(x2d, gain, w_pack, lb_row, ca, sa, cb, sb)
    return dict(zip([o[0] for o in outs], res))


def _attn_a_kernel(lam_ref, q_ref, k_ref, v_ref, o_ref, q4_ref, m_ref, acc_ref, *, tq, tk, q_off, kv_len):
    qi = pl.program_id(2)
    q = q_ref[...]
    lane = lax.broadcasted_iota(jnp.int32, q.shape, 1)
    for i in range(4):
        q4_ref[i * tq:(i + 1) * tq, :] = jnp.where(lane // DA == i, q, jnp.zeros_like(q))
    m_ref[...] = jnp.full(m_ref.shape, NEG_BIG, F32)
    acc_ref[...] = jnp.zeros(acc_ref.shape, F32)

    q_first = q_off + qi * tq
    q_last = q_first + tq - 1
    n_full = jnp.minimum((q_first // CHUNK + 1) * CHUNK, kv_len) // tk
    lim = jnp.minimum((q_last // CHUNK + 1) * CHUNK, kv_len)
    n_tot = (lim + tk - 1) // tk

    def step(j, masked):
        start = pl.multiple_of(j * tk, tk)
        kb = k_ref[pl.ds(start, tk), :]
        vb = v_ref[pl.ds(start, tk), :]
        s = _dot_nt(q4_ref[...], kb)
        if masked:
            kpos = start + lax.broadcasted_iota(jnp.int32, (tq, tk), 1)
            qpos = q_first + lax.broadcasted_iota(jnp.int32, (tq, tk), 0)
            valid = (kpos // CHUNK <= qpos // CHUNK) & (kpos < kv_len)
            s = jnp.where(valid[None], s.reshape(4, tq, tk), NEG_BIG).reshape(4 * tq, tk)
        m_old = m_ref[...]
        m_new = jnp.maximum(m_old, jnp.max(s, axis=-1, keepdims=True))
        alpha = jnp.exp2(m_old - m_new)
        p = jnp.exp2(s - jnp.concatenate([m_new] * (tk // LANES), axis=1)).astype(BF16)
        lane_v = lax.broadcasted_iota(jnp.int32, vb.shape, 1)
        ones = jnp.ones_like(vb)
        pv = jnp.concatenate([_dot(p[0:2 * tq], jnp.where(lane_v < DV_A, vb, ones)),
                              _dot(p[2 * tq:4 * tq], jnp.where(lane_v < DV_A, ones, vb))], axis=0)
        acc_ref[...] = alpha * acc_ref[...] + pv
        m_ref[...] = m_new

    def full_body(j, c):
        step(j, False)
        return c

    def masked_body(j, c):
        step(j, True)
        return c

    lax.fori_loop(0, n_full, full_body, 0)
    lax.fori_loop(n_full, n_tot, masked_body, 0)

    lam = lam_ref[0]
    acc0 = acc_ref[0:2 * tq, :]
    acc1 = acc_ref[2 * tq:4 * tq, :]
    on0 = acc0 / acc0[:, DV_A:DV_A + 1]
    on1 = acc1 / acc1[:, 0:1]
    o0 = on0[0:tq] - lam * on0[tq:2 * tq]
    o1 = on1[0:tq] - lam * on1[tq:2 * tq]
    o_ref[...] = jnp.where(lane < DV_A, o0, o1)


def _attention_a(lam, q, k, v, *, q_off, kv_len, tq, tk):
    b, t, _ = q.shape
    lk = k.shape[1]
    kern = functools.partial(_attn_a_kernel, tq=tq, tk=tk, q_off=q_off, kv_len=kv_len)
    return pl.pallas_call(
        kern,
        grid=(b, WA // LANES, t // tq),
        in_specs=[pl.BlockSpec(memory_space=pltpu.SMEM),
                  pl.BlockSpec((None, tq, LANES), lambda bi, hi, qi: (bi, qi, hi)),
                  pl.BlockSpec((None, lk, LANES), lambda bi, hi, qi: (bi, 0, hi)),
                  pl.BlockSpec((None, lk, LANES), lambda bi, hi, qi: (bi, 0, hi))],
        out_specs=pl.BlockSpec((None, tq, LANES), lambda bi, hi, qi: (bi, qi, hi)),
        out_shape=jax.ShapeDtypeStruct((b, t, WA), F32),
        scratch_shapes=[pltpu.VMEM((4 * tq, LANES), BF16), pltpu.VMEM((4 * tq, LANES), F32),
                        pltpu.VMEM((4 * tq, LANES), F32)],
        compiler_params=_cparams(3),
        name="mixer_a",
    )(lam, q, k, v)


TIE_BLOCK = 256


def _attn_b_kernel(q_ref, iq_ref, iw_ref, k_ref, v_ref, ik_ref, o_ref, key_ref, bias_ref,
                   *, tq, q_off, kv_len, n_sel, variants):
    q_first = q_off + pl.program_id(1) * tq
    need_keys = jnp.minimum(((q_first + tq - 1) // CHUNK + 1) * CHUNK, kv_len)
    lo = 0
    for lk in variants:
        @pl.when((need_keys > lo) & (need_keys <= lk))
        def _(lk=lk):
            _attn_b_body(q_ref, iq_ref, iw_ref, k_ref, v_ref, ik_ref, o_ref, key_ref, bias_ref,
                         tq=tq, q_first=q_first, need_keys=need_keys, kv_len=kv_len, n_sel=n_sel, lk=lk)
        lo = lk


def _stack_heads(x, n_heads, width):
    lane = lax.broadcasted_iota(jnp.int32, x.shape, 1)
    return jnp.concatenate([jnp.where(lane // width == h, x, jnp.zeros_like(x)) for h in range(n_heads)],
                           axis=0)


def _attn_b_body(q_ref, iq_ref, iw_ref, k_ref, v_ref, ik_ref, o_ref, key_ref, bias_ref,
                 *, tq, q_first, need_keys, kv_len, n_sel, lk):
    ik2 = ik_ref[0:lk, :]
    iq = iq_ref[...]
    iw = iw_ref[...]

    score = jnp.zeros((tq, lk), F32)
    for pi in range(H_I // 2):
        y = _stack_heads(iq[:, LANES * pi:LANES * (pi + 1)], 2, D_I)
        d = jnp.maximum(_dot_nt(y, ik2), 0.0)
        score = score + iw[:, 2 * pi:2 * pi + 1] * d[0:tq] + iw[:, 2 * pi + 1:2 * pi + 2] * d[tq:2 * tq]

    kpos = lax.broadcasted_iota(jnp.int32, (tq, lk), 1)
    qpos = q_first + lax.broadcasted_iota(jnp.int32, (tq, lk), 0)
    valid = (kpos // CHUNK <= qpos // CHUNK) & (kpos < kv_len)
    score = jnp.where(valid, score, NEG_BIG)

    bits = lax.bitcast_convert_type(score, jnp.int32)
    key = bits ^ ((bits >> 31) & jnp.int32(0x7FFFFFFF))
    key = jnp.where(key == -1, 0, key)
    key_ref[:, 0:lk] = key

    kf = float(n_sel)

    def count_ge(cand):
        wide = jnp.concatenate([cand] * (lk // LANES), axis=1)
        return jnp.sum(jnp.where(key_ref[:, 0:lk] >= wide, 1.0, 0.0), axis=-1, keepdims=True)

    int_min = jnp.int32(-2 ** 31)
    zero = jnp.zeros((tq, LANES), jnp.int32)
    select_all = need_keys <= n_sel
    cur = jnp.where(select_all | (count_ge(zero) < kf), int_min, zero)

    def bit_body(i, cur):
        cand = cur | jnp.left_shift(jnp.int32(1), 30 - i)
        return jnp.where(count_ge(cand) >= kf, cand, cur)

    thr = lax.fori_loop(0, jnp.where(select_all, 0, 31), bit_body, cur)

    key = key_ref[:, 0:lk]
    thr = jnp.concatenate([thr] * (lk // LANES), axis=1)
    gt = key > thr
    eq = key == thr
    need = kf - jnp.sum(jnp.where(gt, 1.0, 0.0), axis=-1, keepdims=True)
    r_i = lax.broadcasted_iota(jnp.int32, (TIE_BLOCK, TIE_BLOCK), 0)
    c_i = lax.broadcasted_iota(jnp.int32, (TIE_BLOCK, TIE_BLOCK), 1)
    tri = jnp.where(r_i <= c_i, 1.0, 0.0).astype(BF16)
    carry = jnp.zeros((tq, 1), F32)
    for jb in range(lk // TIE_BLOCK):
        sl = slice(jb * TIE_BLOCK, (jb + 1) * TIE_BLOCK)
        eq_b = eq[:, sl]
        pref = _dot(jnp.where(eq_b, 1.0, 0.0).astype(BF16), tri) + carry
        carry = pref[:, TIE_BLOCK - 1:TIE_BLOCK]
        sel = (gt[:, sl] | (eq_b & (pref <= need))) & valid[:, sl]
        bias_ref[:, sl] = jnp.where(sel, 0.0, NEG_BIG)

    q = q_ref[...]
    lane = lax.broadcasted_iota(jnp.int32, (tq, LANES), 1)
    bias = bias_ref[:, 0:lk]
    for p in range(NPAIR):
        n_heads = min(2, H_B - 2 * p)
        sl = slice(p * LANES, (p + 1) * LANES)
        y = _stack_heads(q[:, sl], n_heads, D_B)
        s = _dot_nt(y, k_ref[0:lk, sl]).reshape(n_heads, tq, lk) + bias[None]
        m = jnp.max(s, axis=-1, keepdims=True)
        pr = jnp.exp2(s - m)
        l = jnp.sum(pr, axis=-1, keepdims=True)
        o = _dot(pr.reshape(n_heads * tq, lk).astype(BF16), v_ref[0:lk, sl]) / l.reshape(n_heads * tq, 1)
        o_ref[:, sl] = o if n_heads == 1 else jnp.where(lane < D_B, o[0:tq], o[tq:2 * tq])


def _attention_b(q, iq, iw, k, v, ik2, *, q_off, kv_len, tq, n_sel):
    b, t, _ = q.shape
    lk = k.shape[1]
    granule = tq if (tq % TIE_BLOCK == 0 and lk % tq == 0) else lk
    variants = tuple(range(granule, lk + 1, granule))
    kern = functools.partial(_attn_b_kernel, tq=tq, q_off=q_off, kv_len=kv_len, n_sel=n_sel,
                             variants=variants)
    qspec = lambda w: pl.BlockSpec((None, tq, w), lambda bi, qi: (bi, qi, 0))
    kspec = lambda w: pl.BlockSpec((None, lk, w), lambda bi, qi: (bi, 0, 0))
    return pl.pallas_call(
        kern,
        grid=(b, t // tq),
        in_specs=[qspec(WBP), qspec(WIQ), qspec(LANES), kspec(WBP), kspec(WBP), kspec(2 * D_I)],
        out_specs=qspec(WBP),
        out_shape=jax.ShapeDtypeStruct((b, t, WBP), F32),
        scratch_shapes=[pltpu.VMEM((tq, lk), jnp.int32), pltpu.VMEM((tq, lk), F32)],
        compiler_params=_cparams(2),
        name="mixer_b",
    )(q, iq, iw, k, v, ik2)


GROUP = SUBLANES
DECAY_SPLIT_LIMIT = 60.0


def _hgrn2_kernel(q_ref, k_ref, g_ref, v_ref, s0_ref, o_ref, s_out_ref, st_ref, *, c):
    ci = pl.program_id(1)

    @pl.when(ci == 0)
    def _():
        st_ref[...] = s0_ref[...]

    q = q_ref[...]
    k = k_ref[...]
    g = g_ref[...]
    v = v_ref[...]
    w = q.shape[-1]
    row = lax.broadcasted_iota(jnp.int32, (c, w), 0)
    lane = lax.broadcasted_iota(jnp.int32, (c, w), 1)

    def segment_scans():
        cs, tots = {1: g}, {1: g}
        cum, tot, m = g, g, 1
        while m < c:
            upper = (row // m) % 2 == 1
            prev_tot = pltpu.roll(tot, m, 0)
            next_tot = pltpu.roll(tot, c - m, 0)
            cum = cum + jnp.where(upper, prev_tot, 0.0)
            tot = tot + jnp.where(upper, prev_tot, next_tot)
            m *= 2
            cs[m], tots[m] = cum, tot
        return cs, tots

    def cumsum_rows():
        cum, m = g, 1
        while m < c:
            cum = cum + jnp.where(row >= m, pltpu.roll(cum, m, 0), 0.0)
            m *= 2
        return cum

    head_masks = [lane // DK_C == h for h in range(w // DK_C)]
    n_heads = H_C
    rq = lax.broadcasted_iota(jnp.int32, (c, c), 0)
    rk = lax.broadcasted_iota(jnp.int32, (c, c), 1)

    def attend(att):
        res = _dot(att.astype(BF16), v.astype(BF16))
        out = jnp.zeros((c, w), F32)
        for h in range(n_heads):
            out = out + jnp.where(head_masks[h], res[h * c:(h + 1) * c], 0.0)
        return out

    def stack_heads(x):
        return jnp.concatenate([jnp.where(head_masks[h], x, 0.0) for h in range(n_heads)], axis=0)

    def finish(o_intra, bcum, blast):
        qe = (q * jnp.exp(bcum)).astype(BF16)
        k2 = (k * jnp.exp(blast - bcum)).astype(BF16)
        decay = jnp.exp(blast[0:1, :])
        pr = lax.broadcasted_iota(jnp.int32, (LANES, LANES), 0)
        pc = lax.broadcasted_iota(jnp.int32, (LANES, LANES), 1)
        diag = pr // DK_C == pc // DK_C
        o_state = []
        for p in range(w // LANES):
            sl = slice(p * LANES, (p + 1) * LANES)
            st = st_ref[p]
            o_state.append(_dot_nt(qe[:, sl], st.astype(BF16)))
            upd = _dot(v[:, sl].T.astype(BF16), k2[:, sl])
            st_ref[p] = st * decay[:, sl] + jnp.where(diag, upd, 0.0)
        o_ref[...] = o_intra + jnp.concatenate(o_state, axis=-1)

    ag = jnp.abs(g)
    half_bound = jnp.maximum(jnp.max(jnp.sum(ag[0:c // 2], axis=0, keepdims=True)),
                             jnp.max(jnp.sum(ag[c // 2:c], axis=0, keepdims=True)))
    single_split = half_bound <= DECAY_SPLIT_LIMIT

    @pl.when(single_split)
    def _():
        bcum = cumsum_rows()
        dmid = bcum - bcum[c // 2 - 1:c // 2, :]
        qt = stack_heads(q * jnp.exp(dmid)).astype(BF16)
        kt = (k * jnp.exp(-dmid)).astype(BF16)
        causal = jnp.concatenate([rq >= rk] * n_heads, axis=0)
        o_intra = attend(jnp.where(causal, _dot_nt(qt, kt), 0.0))
        finish(o_intra, bcum, jnp.broadcast_to(bcum[c - 1:c, :], (c, w)))

    @pl.when(jnp.logical_not(single_split))
    def _():
        cs, tots = segment_scans()
        att = jnp.zeros((n_heads * c, c), F32)
        half = GROUP
        while half < c:
            upper = (row // half) % 2 == 1
            qt = jnp.where(upper, q * jnp.exp(cs[half]), 0.0)
            kt = jnp.where(upper, 0.0, k * jnp.exp(tots[half] - cs[half])).astype(BF16)
            blk = _dot_nt(stack_heads(qt).astype(BF16), kt)
            same = (rq // (2 * half)) == (rk // (2 * half))
            same = jnp.concatenate([same] * n_heads, axis=0)
            att = att + jnp.where(same, blk, 0.0)
            half *= 2
        out = attend(att)

        def group_row(x, j):
            x3 = x.reshape(c // GROUP, GROUP, w)
            return jnp.broadcast_to(x3[:, j:j + 1, :], x3.shape).reshape(c, w)

        c8 = cs[GROUP]
        vals = []
        for j in range(GROUP):
            ok = (row % GROUP) >= j
            e = jnp.where(ok, c8 - group_row(c8, j), 0.0)
            vals.append(jnp.where(ok, q * group_row(k, j) * jnp.exp(e), 0.0))
        r_i = lax.broadcasted_iota(jnp.int32, (w, w), 0)
        c_i = lax.broadcasted_iota(jnp.int32, (w, w), 1)
        head_sum = jnp.where(r_i // DK_C == c_i // DK_C, 1.0, 0.0).astype(BF16)
        wts = _dot(jnp.concatenate(vals, axis=0).astype(BF16), head_sum)
        for j in range(GROUP):
            out = out + wts[j * c:(j + 1) * c] * group_row(v, j)
        finish(out, cs[c], tots[c])

    @pl.when(ci == pl.num_programs(1) - 1)
    def _():
        s_out_ref[...] = st_ref[...]


def _hgrn2(q, k, g, v, s0, *, c):
    b, t, w = q.shape
    kern = functools.partial(_hgrn2_kernel, c=c)
    blk = pl.BlockSpec((None, c, w), lambda bi, ci: (bi, ci, 0))
    sblk = pl.BlockSpec((None, w // LANES, LANES, LANES), lambda bi, ci: (bi, 0, 0, 0))
    return pl.pallas_call(
        kern,
        grid=(b, t // c),
        in_specs=[blk, blk, blk, blk, sblk],
        out_specs=[blk, sblk],
        out_shape=[jax.ShapeDtypeStruct((b, t, w), F32),
                   jax.ShapeDtypeStruct((b, w // LANES, LANES, LANES), F32)],
        scratch_shapes=[pltpu.VMEM((w // LANES, LANES, LANES), F32)],
        compiler_params=_cparams(2),
        name="mixer_c",
    )(q, k, g, v, s0)


def _state_to_pairs(s):
    b = s.shape[0]
    st = jnp.swapaxes(s.astype(F32), -1, -2)
    st = jnp.pad(st, ((0, 0), (0, 2 * NPAIR - H_C), (0, 0), (0, 0)))
    st = st.reshape(b, NPAIR, 2, DV_C, DK_C)
    eye = jnp.eye(2, dtype=F32)
    full = st[:, :, :, :, None, :] * eye[None, None, :, None, :, None]
    return full.reshape(b, NPAIR, 2 * DV_C, 2 * DK_C)


def _pairs_to_state(sp):
    b = sp.shape[0]
    s6 = sp.reshape(b, NPAIR, 2, DV_C, 2, DK_C)
    diag = jnp.stack([s6[:, :, a, :, a, :] for a in range(2)], axis=2)
    return jnp.swapaxes(diag.reshape(b, 2 * NPAIR, DV_C, DK_C)[:, :H_C], -1, -2)


FF_BLOCK = 2816


def _head_norm(y, gain):
    w = y.shape[-1]
    r_i = lax.broadcasted_iota(jnp.int32, (w, w), 0)
    c_i = lax.broadcasted_iota(jnp.int32, (w, w), 1)
    head_sum = jnp.where(r_i // DV_A == c_i // DV_A, 1.0, 0.0).astype(BF16)
    y2 = y * y
    hi = y2.astype(BF16)
    lo = (y2 - hi.astype(F32)).astype(BF16)
    ms = (_dot(hi, head_sum) + _dot(lo, head_sum)) * (1.0 / DV_A)
    return y * lax.rsqrt(ms + EPS) * gain


def _merge_kernel(scal_ref, x_ref, oa_ref, ob_ref, oc_ref, cg_ref, hist_ref, ag_ref, cgn_ref, wo_ref,
                  n2_ref, wup_ref, cw_ref, cb_ref, wdn_ref, fn_ref, *out_and_scratch, d_ff, final):
    if final:
        x_out_ref, fc_ref, y_ref, carry_ref = out_and_scratch
    else:
        x_out_ref, fc_ref, carry_ref = out_and_scratch
    ti = pl.program_id(1)
    tm = x_ref.shape[0]

    @pl.when(ti == 0)
    def _():
        carry_ref[...] = hist_ref[...]

    oa = _head_norm(oa_ref[...], ag_ref[...]) * scal_ref[0]
    cg = cg_ref[...]
    oc = _head_norm(oc_ref[...], cgn_ref[...]) * (cg * jax.nn.sigmoid(cg))
    mixed = (_dot(oa.astype(BF16), wo_ref[0:WA, :])
             + _dot(ob_ref[...].astype(BF16), wo_ref[WA:WA + WBP, :])
             + _dot(oc.astype(BF16), wo_ref[WA + WBP:WA + 2 * WBP, :]))
    x = x_ref[...] + mixed

    ms = jnp.mean(x * x, axis=-1, keepdims=True)
    h = (x * lax.rsqrt(ms + EPS) * n2_ref[...]).astype(BF16)
    row = lax.broadcasted_iota(jnp.int32, (tm, FF_BLOCK), 0)
    acc = jnp.zeros(x.shape, F32)
    for cblk in range(d_ff // FF_BLOCK):
        sl = slice(cblk * FF_BLOCK, (cblk + 1) * FF_BLOCK)
        a = _dot(h, wup_ref[:, sl])
        gate = _dot(h, wup_ref[:, d_ff + cblk * FF_BLOCK:d_ff + (cblk + 1) * FF_BLOCK])
        prev2 = carry_ref[0:1, sl]
        prev1 = carry_ref[1:2, sl]
        a1 = jnp.where(row == 0, prev1, pltpu.roll(a, 1, 0))
        a2 = jnp.where(row == 0, prev2, jnp.where(row == 1, prev1, pltpu.roll(a, 2, 0)))
        conv = cb_ref[:, sl] + a2 * cw_ref[0:1, sl] + a1 * cw_ref[1:2, sl] + a * cw_ref[2:3, sl]
        act = conv * jax.nn.sigmoid(conv) * gate
        acc = acc + _dot(act.astype(BF16), wdn_ref[sl, :])
        carry_ref[:, sl] = a[tm - (CONV_W - 1):, :]
    x = x + acc
    x_out_ref[...] = x
    fc_ref[...] = carry_ref[...]
    if final:
        ms = jnp.mean(x * x, axis=-1, keepdims=True)
        y_ref[...] = x * lax.rsqrt(ms + EPS) * fn_ref[...]


def _merge_ffn(scal, x, oa, ob, oc, cg, hist, a_gain, c_gain, wo, n2, wup, cw, cb, wdn, fnorm, *, tm, final):
    b, t, d = x.shape
    d_ff = wdn.shape[0]
    kern = functools.partial(_merge_kernel, d_ff=d_ff, final=final)
    blk = lambda w: pl.BlockSpec((None, tm, w), lambda bi, ti: (bi, ti, 0))
    per_b = pl.BlockSpec((None, CONV_W - 1, d_ff), lambda bi, ti: (bi, 0, 0))
    out_specs = [blk(d), per_b]
    out_shape = [jax.ShapeDtypeStruct((b, t, d), F32), jax.ShapeDtypeStruct((b, CONV_W - 1, d_ff), F32)]
    if final:
        out_specs.append(blk(d))
        out_shape.append(jax.ShapeDtypeStruct((b, t, d), F32))
    return pl.pallas_call(
        kern,
        grid=(b, t // tm),
        in_specs=[pl.BlockSpec(memory_space=pltpu.SMEM), blk(d), blk(WA), blk(WBP), blk(WBP), blk(WBP), per_b,
                  _const_spec(a_gain.shape), _const_spec(c_gain.shape), _const_spec(wo.shape),
                  _const_spec(n2.shape), _const_spec(wup.shape), _const_spec(cw.shape),
                  _const_spec(cb.shape), _const_spec(wdn.shape), _const_spec(fnorm.shape)],
        out_specs=out_specs,
        out_shape=out_shape,
        scratch_shapes=[pltpu.VMEM((CONV_W - 1, d_ff), F32)],
        compiler_params=_cparams(2),
        name="merge_ffn",
    )(scal, x, oa, ob, oc, cg, hist, a_gain, c_gain, wo, n2, wup, cw, cb, wdn, fnorm)


def _pack_w_in(w):
    parts = jnp.split(w, np.cumsum(IN_SIZES)[:-1].tolist(), axis=-1)
    cols = []
    for (name, width, padded), part in zip(SEGS, parts):
        if name == "ik":
            part = jnp.concatenate([part, part], axis=-1)
            width = 2 * D_I
        cols.append(jnp.pad(part, ((0, 0), (0, padded - width))))
    return jnp.concatenate(cols, axis=-1).astype(BF16)


def _pack_w_out(w):
    wa, wb, wc = w[:WA], w[WA:WA + WB], w[WA + WB:]
    pad = lambda m: jnp.pad(m, ((0, WBP - WB), (0, 0)))
    return jnp.concatenate([wa, pad(wb), pad(wc)], axis=0).astype(BF16)


def _rope_table(pos, dim, width):
    inv_freq = ROPE_THETA ** (-jnp.arange(0, dim, 2, dtype=F32) / dim)
    ang = pos.astype(F32)[:, None] * inv_freq[None, :]
    reps = width // (dim // 2)
    return jnp.tile(jnp.cos(ang), (1, reps)), jnp.tile(jnp.sin(ang), (1, reps))


def _pad_rows(x, rows):
    return jnp.pad(x, ((0, 0), (0, rows - x.shape[1]), (0, 0)))


def _round_up(n, m):
    return (n + m - 1) // m * m


def kernel(x_prompt, x_sample, cache_a_k, cache_a_v, cache_b_k, cache_b_v, cache_b_kidx, state_c, state_ffn_conv, norm1, w_in, lam_q1, lam_k1, lam_q2, lam_k2, a_norm, c_lower, c_norm, w_out, norm2, ffn_up, ffn_conv_w, ffn_conv_b, ffn_down, final_norm):
    depth = w_in.shape[0]
    b_p, t_p, d = x_prompt.shape
    b_s, t_s, _ = x_sample.shape
    past = cache_a_k.shape[2]
    d_ff = ffn_down.shape[1]
    kv_s = past + t_s
    n_sel_p = min(TOPK_MAX, t_p // 4)
    n_sel_s = min(TOPK_MAX, kv_s // 4)
    tk = 256
    lk_s = _round_up(kv_s, tk)

    lb_soft = jax.nn.softmax(c_lower.astype(F32), axis=0)
    lower = jnp.cumsum(lb_soft, axis=0) - lb_soft[0]
    lower = jnp.pad(lower, ((0, 0), (0, WBP - WB)))

    tm_p = min(512, t_p)
    tm_s = min(256, b_s * t_s)
    pos_p = jnp.arange(t_p)
    pos_s = jnp.tile(past + jnp.arange(t_s), tm_s // t_s)
    tabs_p = _rope_table(pos_p, DA, WA) + _rope_table(pos_p, D_B, WBP)
    tabs_s = _rope_table(pos_s, DA, WA) + _rope_table(pos_s, D_B, WBP)

    xp, xs = x_prompt, x_sample
    outs_p = [[] for _ in range(7)]
    outs_s = [[] for _ in range(7)]
    y_p = y_s = None
    fnorm = final_norm.reshape(1, d)
    for l in range(depth):
        lam_init = 0.8 - 0.6 * math.exp(-0.3 * l)
        lam = (jnp.exp(jnp.sum(lam_q1[l].astype(F32) * lam_k1[l].astype(F32)))
               - jnp.exp(jnp.sum(lam_q2[l].astype(F32) * lam_k2[l].astype(F32))) + lam_init)
        lam_arr = lam.reshape(1).astype(F32)
        scal = jnp.full((1,), 1.0 - lam_init, F32)
        w_pack = _pack_w_in(w_in[l])
        wo = _pack_w_out(w_out[l])
        gain1 = norm1[l].reshape(1, d)
        gain2 = norm2[l].reshape(1, d)
        lb_row = lower[l].reshape(1, WBP)
        a_gain = jnp.tile(a_norm[l], H_A).reshape(1, WA)
        c_gain = jnp.pad(jnp.tile(c_norm[l], H_C), (0, WBP - WB)).reshape(1, WBP)
        wup = ffn_up[l].astype(BF16)
        wdn = ffn_down[l].astype(BF16)
        cw = ffn_conv_w[l]
        cb = ffn_conv_b[l].reshape(1, d_ff)
        final = l == depth - 1

        u = _in_projection(xp.reshape(b_p * t_p, d), gain1, w_pack, lb_row, tabs_p, tm_p)
        r3 = lambda a, b=b_p, t=t_p: a.reshape(b, t, a.shape[-1])
        oa = _attention_a(lam_arr, r3(u["aq"]), r3(u["ak16"]), r3(u["av16"]),
                          q_off=0, kv_len=t_p, tq=min(256, t_p), tk=min(512, t_p))
        ob = _attention_b(r3(u["bq"]), r3(u["iq"]), r3(u["iw"]), r3(u["bk16"]), r3(u["bv16"]), r3(u["ik2"]),
                          q_off=0, kv_len=t_p, tq=min(256, t_p), n_sel=n_sel_p)
        s0 = jnp.zeros((b_p, NPAIR, LANES, LANES), F32)
        oc, s_new = _hgrn2(r3(u["cq"]), r3(u["ck"]), r3(u["cgl"]), r3(u["cv"]), s0, c=CHUNK)
        hist0 = jnp.zeros((b_p, CONV_W - 1, d_ff), F32)
        res = _merge_ffn(scal, xp, oa, ob, oc, r3(u["cg"]), hist0, a_gain, c_gain, wo, gain2, wup, cw, cb,
                         wdn, fnorm, tm=min(512, t_p), final=final)
        xp, fc = res[0], res[1]
        if final:
            y_p = res[2]
        for lst, val in zip(outs_p, (u["ak"].reshape(b_p, t_p, H_A, 2 * DA), u["av"].reshape(b_p, t_p, H_A, DV_A),
                                     u["bk"].reshape(b_p, t_p, H_B, D_B), u["bv"].reshape(b_p, t_p, H_B, D_B),
                                     u["ik"].reshape(b_p, t_p, D_I), _pairs_to_state(s_new), fc)):
            lst.append(val)

        u = _in_projection(xs.reshape(b_s * t_s, d), gain1, w_pack, lb_row, tabs_s, tm_s)
        r3 = lambda a, b=b_s, t=t_s: a.reshape(b, t, a.shape[-1])

        def cat(c, new):
            c = c.reshape(b_s, past, -1).astype(new.dtype)
            c = jnp.pad(c, ((0, 0), (0, 0), (0, new.shape[-1] - c.shape[-1])))
            return _pad_rows(jnp.concatenate([c, r3(new)], axis=1), lk_s)

        kidx = cache_b_kidx[l]
        kidx2 = jnp.concatenate([kidx, kidx], axis=-1)
        oa = _attention_a(lam_arr, r3(u["aq"]), cat(cache_a_k[l], u["ak16"]), cat(cache_a_v[l], u["av16"]),
                          q_off=past, kv_len=kv_s, tq=t_s, tk=tk)
        ob = _attention_b(r3(u["bq"]), r3(u["iq"]), r3(u["iw"]), cat(cache_b_k[l], u["bk16"]),
                          cat(cache_b_v[l], u["bv16"]), cat(kidx2, u["ik2"]),
                          q_off=past, kv_len=kv_s, tq=t_s, n_sel=n_sel_s)
        oc, s_new = _hgrn2(r3(u["cq"]), r3(u["ck"]), r3(u["cgl"]), r3(u["cv"]), _state_to_pairs(state_c[l]),
                           c=t_s)
        res = _merge_ffn(scal, xs, oa, ob, oc, r3(u["cg"]), state_ffn_conv[l].astype(F32), a_gain, c_gain, wo,
                         gain2, wup, cw, cb, wdn, fnorm, tm=t_s, final=final)
        xs, fc = res[0], res[1]
        if final:
            y_s = res[2]
        for lst, val in zip(outs_s, (u["ak"].reshape(b_s, t_s, H_A, 2 * DA), u["av"].reshape(b_s, t_s, H_A, DV_A),
                                     u["bk"].reshape(b_s, t_s, H_B, D_B), u["bv"].reshape(b_s, t_s, H_B, D_B),
                                     u["ik"].reshape(b_s, t_s, D_I), _pairs_to_state(s_new), fc)):
            lst.append(val)

    return (y_p, y_s) + tuple(jnp.stack(v) for v in outs_p) + tuple(jnp.stack(v) for v in outs_s)
```

```python
import functools
import math

import jax
import jax.numpy as jnp
import numpy as np
from jax import lax
from jax.experimental import pallas as pl
from jax.experimental.pallas import tpu as pltpu

F32 = jnp.float32
BF16 = jnp.bfloat16

CHUNK = 64
ROPE_THETA = 10000.0
EPS = 1e-6
NEG_BIG = -1e30
LB_FLOOR = 1e-20
H_A, DA, DV_A = 6, 32, 64
H_B, D_B = 5, 64
H_I, D_I = 4, 64
TOPK_MAX = 256
H_C, DK_C, DV_C = 5, 64, 64
CONV_W = 3

LANES = 128
SUBLANES = 8
VMEM_LIMIT = 56 * 1024 * 1024

WA = H_A * 2 * DA
WB = H_B * D_B
WBP = 384
WIQ = H_I * D_I
NPAIR = WBP // LANES

SEGS = (("aq", WA, WA), ("ak", WA, WA), ("av", WA, WA),
        ("bq", WB, WBP), ("bk", WB, WBP), ("bv", WB, WBP),
        ("iq", WIQ, WIQ), ("ik", D_I, 2 * D_I), ("iw", H_I, LANES),
        ("cq", WB, WBP), ("cf", WB, WBP), ("ci", WB, WBP), ("cg", WB, WBP))
SEG_OFF = {}
_o = 0
for _n, _w, _p in SEGS:
    SEG_OFF[_n] = (_o, _p)
    _o += _p
W_PACK = _o
PROJ_GROUPS = tuple((SEG_OFF[first][0], SEG_OFF[last][0] + SEG_OFF[last][1] - SEG_OFF[first][0])
                    for first, last in (("aq", "ak"), ("av", "bq"), ("bk", "bv"), ("iq", "iw"),
                                        ("cq", "cf"), ("ci", "cg")))
IN_SIZES = (WA, WA, WA, WB, WB, WB, WIQ, D_I, H_I, WB, WB, WB, WB)

LOG2E = math.log2(math.e)
NT_DIMS = (((1,), (1,)), ((), ()))


def _cparams(n_axes):
    return pltpu.CompilerParams(dimension_semantics=("arbitrary",) * n_axes,
                                vmem_limit_bytes=VMEM_LIMIT)


def _const_spec(shape):
    nd = len(shape)
    return pl.BlockSpec(shape, lambda *_: (0,) * nd, pipeline_mode=pl.Buffered(1))


def _dot(a, b):
    return jnp.dot(a, b, preferred_element_type=F32)


def _dot_nt(a, b):
    return lax.dot_general(a, b, NT_DIMS, preferred_element_type=F32)


def _rope(u, cos, sin, half):
    w = u.shape[-1]
    lane = lax.broadcasted_iota(jnp.int32, u.shape, 1)
    first = (lane % (2 * half)) < half
    rot = jnp.where(first, -pltpu.roll(u, w - half, 1), pltpu.roll(u, half, 1))
    return u * cos + rot * sin


N_PROJ_INPUTS = 8


def _inproj_kernel(*refs, n_aliased, feature_major_caches):
    x_ref, g_ref, w_ref, lb_ref, ca_ref, sa_ref, cb_ref, sb_ref = refs[:N_PROJ_INPUTS]
    (aq_ref, ak_ref, av_ref, ak16_ref, av16_ref, bq_ref, bk_ref, bv_ref, bk16_ref, bv16_ref,
     iq_ref, ik2_ref, ik_ref, iw_ref, cq_ref, ck_ref, cgl_ref, cv_ref, cg_ref) = refs[N_PROJ_INPUTS + n_aliased:]

    def put_cache(ref, val, width):
        ref[...] = val.T[:width, :] if feature_major_caches else val[:, :width]

    x = x_ref[...]
    ms = jnp.mean(x * x, axis=-1, keepdims=True)
    xn = (x * lax.rsqrt(ms + EPS) * g_ref[...]).astype(BF16)

    group_dots = {}

    def seg(name):
        off, width = SEG_OFF[name]
        g_off, g_width = next((o, wd) for o, wd in PROJ_GROUPS if o <= off < o + wd)
        if g_off not in group_dots:
            group_dots[g_off] = _dot(xn, w_ref[:, g_off:g_off + g_width])
        return group_dots[g_off][:, off - g_off:off - g_off + width]

    ca, sa = ca_ref[...], sa_ref[...]
    cb, sb = cb_ref[...], sb_ref[...]
    aq_ref[...] = (_rope(seg("aq"), ca, sa, DA // 2) * (DA ** -0.5 * LOG2E)).astype(BF16)
    ak = _rope(seg("ak"), ca, sa, DA // 2)
    put_cache(ak_ref, ak, WA)
    ak16_ref[...] = ak.astype(BF16)
    av = seg("av")
    put_cache(av_ref, av, WA)
    av16_ref[...] = av.astype(BF16)
    bq_ref[...] = (_rope(seg("bq"), cb, sb, D_B // 2) * (D_B ** -0.5 * LOG2E)).astype(BF16)
    bk = _rope(seg("bk"), cb, sb, D_B // 2)
    put_cache(bk_ref, bk, WB)
    bk16_ref[...] = bk.astype(BF16)
    bv = seg("bv")
    put_cache(bv_ref, bv, WB)
    bv16_ref[...] = bv.astype(BF16)
    iq_ref[...] = (_rope(seg("iq"), cb[:, :WIQ], sb[:, :WIQ], D_I // 2) * (D_I ** -0.5)).astype(BF16)
    ik2 = _rope(seg("ik"), cb[:, :2 * D_I], sb[:, :2 * D_I], D_I // 2)
    ik2_ref[...] = ik2.astype(BF16)
    put_cache(ik_ref, ik2, D_I)
    iw_ref[...] = seg("iw") * (H_I ** -0.5)

    cq = seg("cq")
    cq_ref[...] = cq * jax.nn.sigmoid(cq)
    z = seg("cf")
    lb = lb_ref[...]
    la = jnp.log(jnp.maximum(lb, LB_FLOOR))
    lsig = jnp.minimum(z, 0.0) - jnp.log1p(jnp.exp(-jnp.abs(z)))
    bb = jnp.log1p(-lb) + lsig
    cgl_ref[...] = jnp.maximum(la, bb) + jnp.log1p(jnp.exp(-jnp.abs(la - bb)))
    ck_ref[...] = (1.0 - lb) * jax.nn.sigmoid(-z)
    cv_ref[...] = seg("ci")
    cg_ref[...] = seg("cg")


CACHE_OUTS = ("ak", "av", "bk", "bv", "ik")


def _in_projection(x2d, gain, w_pack, lb_row, tabs, tm, stacked=None):
    n = x2d.shape[0]
    ca, sa, cb, sb = tabs
    period = ca.shape[0] // tm
    row = lambda w: pl.BlockSpec((tm, w), lambda i: (i, 0))
    tab = lambda w: pl.BlockSpec((tm, w), lambda i: (i % period, 0))
    outs = (("aq", WA, BF16), ("ak", WA, F32), ("av", WA, F32), ("ak16", WA, BF16), ("av16", WA, BF16),
            ("bq", WBP, BF16), ("bk", WB, F32), ("bv", WB, F32), ("bk16", WBP, BF16), ("bv16", WBP, BF16),
            ("iq", WIQ, BF16), ("ik2", 2 * D_I, BF16), ("ik", D_I, F32), ("iw", LANES, F32),
            ("cq", WBP, F32), ("ck", WBP, F32), ("cgl", WBP, F32), ("cv", WBP, F32), ("cg", WBP, F32))
    out_specs = {name: row(w) for name, w, _ in outs}
    out_shape = {name: jax.ShapeDtypeStruct((n, w), dt) for name, w, dt in outs}
    inputs = [x2d, gain, w_pack, lb_row, ca, sa, cb, sb]
    in_specs = [row(x2d.shape[1]), _const_spec(gain.shape), _const_spec(w_pack.shape),
                _const_spec(lb_row.shape), tab(WA), tab(WA), tab(WBP), tab(WBP)]
    aliases = {}
    if stacked is not None:
        layer, depth, batch, caches = stacked
        t = n // batch
        tiles = t // tm
        for name, w, _ in outs:
            if name in CACHE_OUTS:
                out_specs[name] = pl.BlockSpec((None, None, w, tm),
                                               lambda i, layer=layer: (layer, i // tiles, 0, i % tiles))
                out_shape[name] = jax.ShapeDtypeStruct((depth, batch, w, t), F32)
        if caches is not None:
            names = [o[0] for o in outs]
            for name in CACHE_OUTS:
                aliases[len(inputs)] = names.index(name)
                inputs.append(caches[name])
                in_specs.append(pl.BlockSpec(memory_space=pl.ANY))
    kern = functools.partial(_inproj_kernel, n_aliased=len(aliases), feature_major_caches=stacked is not None)
    res = pl.pallas_call(
        kern,
        grid=(n // tm,),
        in_specs=in_specs,
        out_specs=[out_specs[o[0]] for o in outs],
        out_shape=[out_shape[o[0]] for o in outs],
        input_output_aliases=aliases,
        compiler_params=_cparams(1),
        name="in_projection",
    )(*inputs)
    return dict(zip([o[0] for o in outs], res))


def _attn_a_kernel(lam_ref, q_ref, k_ref, v_ref, o_ref, q4_ref, m_ref, acc_ref, *, tq, tk, q_off, kv_len):
    qi = pl.program_id(2)
    q = q_ref[...]
    lane = lax.broadcasted_iota(jnp.int32, q.shape, 1)
    for i in range(4):
        q4_ref[i * tq:(i + 1) * tq, :] = jnp.where(lane // DA == i, q, jnp.zeros_like(q))
    m_ref[...] = jnp.full(m_ref.shape, NEG_BIG, F32)
    acc_ref[...] = jnp.zeros(acc_ref.shape, F32)

    q_first = q_off + qi * tq
    q_last = q_first + tq - 1
    n_full = jnp.minimum((q_first // CHUNK + 1) * CHUNK, kv_len) // tk
    lim = jnp.minimum((q_last // CHUNK + 1) * CHUNK, kv_len)
    n_tot = (lim + tk - 1) // tk

    def step(j, masked):
        start = pl.multiple_of(j * tk, tk)
        kb = k_ref[pl.ds(start, tk), :]
        vb = v_ref[pl.ds(start, tk), :]
        s = _dot_nt(q4_ref[...], kb)
        if masked:
            kpos = start + lax.broadcasted_iota(jnp.int32, (tq, tk), 1)
            qpos = q_first + lax.broadcasted_iota(jnp.int32, (tq, tk), 0)
            valid = (kpos // CHUNK <= qpos // CHUNK) & (kpos < kv_len)
            s = jnp.where(valid[None], s.reshape(4, tq, tk), NEG_BIG).reshape(4 * tq, tk)
        m_old = m_ref[...]
        m_new = jnp.maximum(m_old, jnp.max(s, axis=-1, keepdims=True))
        alpha = jnp.exp2(m_old - m_new)
        p = jnp.exp2(s - jnp.concatenate([m_new] * (tk // LANES), axis=1)).astype(BF16)
        lane_v = lax.broadcasted_iota(jnp.int32, vb.shape, 1)
        ones = jnp.ones_like(vb)
        pv = jnp.concatenate([_dot(p[0:2 * tq], jnp.where(lane_v < DV_A, vb, ones)),
                              _dot(p[2 * tq:4 * tq], jnp.where(lane_v < DV_A, ones, vb))], axis=0)
        acc_ref[...] = alpha * acc_ref[...] + pv
        m_ref[...] = m_new

    def full_body(j, c):
        step(j, False)
        return c

    def masked_body(j, c):
        step(j, True)
        return c

    lax.fori_loop(0, n_full, full_body, 0)
    lax.fori_loop(n_full, n_tot, masked_body, 0)

    lam = lam_ref[0]
    acc0 = acc_ref[0:2 * tq, :]
    acc1 = acc_ref[2 * tq:4 * tq, :]
    on0 = acc0 / acc0[:, DV_A:DV_A + 1]
    on1 = acc1 / acc1[:, 0:1]
    o0 = on0[0:tq] - lam * on0[tq:2 * tq]
    o1 = on1[0:tq] - lam * on1[tq:2 * tq]
    o_ref[...] = jnp.where(lane < DV_A, o0, o1)


def _attention_a(lam, q, k, v, *, q_off, kv_len, tq, tk):
    b, t, _ = q.shape
    lk = k.shape[1]
    kern = functools.partial(_attn_a_kernel, tq=tq, tk=tk, q_off=q_off, kv_len=kv_len)
    return pl.pallas_call(
        kern,
        grid=(b, WA // LANES, t // tq),
        in_specs=[pl.BlockSpec(memory_space=pltpu.SMEM),
                  pl.BlockSpec((None, tq, LANES), lambda bi, hi, qi: (bi, qi, hi)),
                  pl.BlockSpec((None, lk, LANES), lambda bi, hi, qi: (bi, 0, hi)),
                  pl.BlockSpec((None, lk, LANES), lambda bi, hi, qi: (bi, 0, hi))],
        out_specs=pl.BlockSpec((None, tq, LANES), lambda bi, hi, qi: (bi, qi, hi)),
        out_shape=jax.ShapeDtypeStruct((b, t, WA), F32),
        scratch_shapes=[pltpu.VMEM((4 * tq, LANES), BF16), pltpu.VMEM((4 * tq, LANES), F32),
                        pltpu.VMEM((4 * tq, LANES), F32)],
        compiler_params=_cparams(3),
        name="mixer_a",
    )(lam, q, k, v)


TIE_BLOCK = 256


def _attn_b_kernel(q_ref, iq_ref, iw_ref, k_ref, v_ref, ik_ref, o_ref, key_ref, bias_ref,
                   *, tq, q_off, kv_len, n_sel, variants):
    q_first = q_off + pl.program_id(1) * tq
    need_keys = jnp.minimum(((q_first + tq - 1) // CHUNK + 1) * CHUNK, kv_len)
    lo = 0
    for lk in variants:
        @pl.when((need_keys > lo) & (need_keys <= lk))
        def _(lk=lk):
            _attn_b_body(q_ref, iq_ref, iw_ref, k_ref, v_ref, ik_ref, o_ref, key_ref, bias_ref,
                         tq=tq, q_first=q_first, need_keys=need_keys, kv_len=kv_len, n_sel=n_sel, lk=lk)
        lo = lk


def _stack_heads(x, n_heads, width):
    lane = lax.broadcasted_iota(jnp.int32, x.shape, 1)
    return jnp.concatenate([jnp.where(lane // width == h, x, jnp.zeros_like(x)) for h in range(n_heads)],
                           axis=0)


def _attn_b_body(q_ref, iq_ref, iw_ref, k_ref, v_ref, ik_ref, o_ref, key_ref, bias_ref,
                 *, tq, q_first, need_keys, kv_len, n_sel, lk):
    ik2 = ik_ref[0:lk, :]
    iq = iq_ref[...]
    iw = iw_ref[...]

    score = jnp.zeros((tq, lk), F32)
    for pi in range(H_I // 2):
        y = _stack_heads(iq[:, LANES * pi:LANES * (pi + 1)], 2, D_I)
        d = jnp.maximum(_dot_nt(y, ik2), 0.0)
        score = score + iw[:, 2 * pi:2 * pi + 1] * d[0:tq] + iw[:, 2 * pi + 1:2 * pi + 2] * d[tq:2 * tq]

    kpos = lax.broadcasted_iota(jnp.int32, (tq, lk), 1)
    qpos = q_first + lax.broadcasted_iota(jnp.int32, (tq, lk), 0)
    valid = (kpos // CHUNK <= qpos // CHUNK) & (kpos < kv_len)
    score = jnp.where(valid, score, NEG_BIG)

    bits = lax.bitcast_convert_type(score, jnp.int32)
    key = bits ^ ((bits >> 31) & jnp.int32(0x7FFFFFFF))
    key = jnp.where(key == -1, 0, key)
    key_ref[:, 0:lk] = key

    kf = float(n_sel)

    def count_ge(cand):
        wide = jnp.concatenate([cand] * (lk // LANES), axis=1)
        return jnp.sum(jnp.where(key_ref[:, 0:lk] >= wide, 1.0, 0.0), axis=-1, keepdims=True)

    int_min = jnp.int32(-2 ** 31)
    zero = jnp.zeros((tq, LANES), jnp.int32)
    select_all = need_keys <= n_sel
    cur = jnp.where(select_all | (count_ge(zero) < kf), int_min, zero)

    def bit_body(i, cur):
        cand = cur | jnp.left_shift(jnp.int32(1), 30 - i)
        return jnp.where(count_ge(cand) >= kf, cand, cur)

    thr = lax.fori_loop(0, jnp.where(select_all, 0, 31), bit_body, cur)

    key = key_ref[:, 0:lk]
    thr = jnp.concatenate([thr] * (lk // LANES), axis=1)
    gt = key > thr
    eq = key == thr
    need = kf - jnp.sum(jnp.where(gt, 1.0, 0.0), axis=-1, keepdims=True)
    r_i = lax.broadcasted_iota(jnp.int32, (TIE_BLOCK, TIE_BLOCK), 0)
    c_i = lax.broadcasted_iota(jnp.int32, (TIE_BLOCK, TIE_BLOCK), 1)
    tri = jnp.where(r_i <= c_i, 1.0, 0.0).astype(BF16)
    carry = jnp.zeros((tq, 1), F32)
    for jb in range(lk // TIE_BLOCK):
        sl = slice(jb * TIE_BLOCK, (jb + 1) * TIE_BLOCK)
        eq_b = eq[:, sl]
        pref = _dot(jnp.where(eq_b, 1.0, 0.0).astype(BF16), tri) + carry
        carry = pref[:, TIE_BLOCK - 1:TIE_BLOCK]
        sel = (gt[:, sl] | (eq_b & (pref <= need))) & valid[:, sl]
        bias_ref[:, sl] = jnp.where(sel, 0.0, NEG_BIG)

    q = q_ref[...]
    lane = lax.broadcasted_iota(jnp.int32, (tq, LANES), 1)
    bias = bias_ref[:, 0:lk]
    for p in range(NPAIR):
        n_heads = min(2, H_B - 2 * p)
        sl = slice(p * LANES, (p + 1) * LANES)
        y = _stack_heads(q[:, sl], n_heads, D_B)
        s = _dot_nt(y, k_ref[0:lk, sl]).reshape(n_heads, tq, lk) + bias[None]
        m = jnp.max(s, axis=-1, keepdims=True)
        pr = jnp.exp2(s - m)
        l = jnp.sum(pr, axis=-1, keepdims=True)
        o = _dot(pr.reshape(n_heads * tq, lk).astype(BF16), v_ref[0:lk, sl]) / l.reshape(n_heads * tq, 1)
        o_ref[:, sl] = o if n_heads == 1 else jnp.where(lane < D_B, o[0:tq], o[tq:2 * tq])


def _attention_b(q, iq, iw, k, v, ik2, *, q_off, kv_len, tq, n_sel):
    b, t, _ = q.shape
    lk = k.shape[1]
    granule = tq if (tq % TIE_BLOCK == 0 and lk % tq == 0) else lk
    variants = tuple(range(granule, lk + 1, granule))
    kern = functools.partial(_attn_b_kernel, tq=tq, q_off=q_off, kv_len=kv_len, n_sel=n_sel,
                             variants=variants)
    qspec = lambda w: pl.BlockSpec((None, tq, w), lambda bi, qi: (bi, qi, 0))
    kspec = lambda w: pl.BlockSpec((None, lk, w), lambda bi, qi: (bi, 0, 0))
    return pl.pallas_call(
        kern,
        grid=(b, t // tq),
        in_specs=[qspec(WBP), qspec(WIQ), qspec(LANES), kspec(WBP), kspec(WBP), kspec(2 * D_I)],
        out_specs=qspec(WBP),
        out_shape=jax.ShapeDtypeStruct((b, t, WBP), F32),
        scratch_shapes=[pltpu.VMEM((tq, lk), jnp.int32), pltpu.VMEM((tq, lk), F32)],
        compiler_params=_cparams(2),
        name="mixer_b",
    )(q, iq, iw, k, v, ik2)


GROUP = SUBLANES
DECAY_SPLIT_LIMIT = 60.0


def _hgrn2_kernel(q_ref, k_ref, g_ref, v_ref, s0_ref, o_ref, s_out_ref, st_ref, *, c):
    ci = pl.program_id(1)

    @pl.when(ci == 0)
    def _():
        st_ref[...] = s0_ref[...]

    q = q_ref[...]
    k = k_ref[...]
    g = g_ref[...]
    v = v_ref[...]
    w = q.shape[-1]
    row = lax.broadcasted_iota(jnp.int32, (c, w), 0)
    lane = lax.broadcasted_iota(jnp.int32, (c, w), 1)

    def segment_scans():
        cs, tots = {1: g}, {1: g}
        cum, tot, m = g, g, 1
        while m < c:
            upper = (row // m) % 2 == 1
            prev_tot = pltpu.roll(tot, m, 0)
            next_tot = pltpu.roll(tot, c - m, 0)
            cum = cum + jnp.where(upper, prev_tot, 0.0)
            tot = tot + jnp.where(upper, prev_tot, next_tot)
            m *= 2
            cs[m], tots[m] = cum, tot
        return cs, tots

    def cumsum_rows():
        cum, m = g, 1
        while m < c:
            cum = cum + jnp.where(row >= m, pltpu.roll(cum, m, 0), 0.0)
            m *= 2
        return cum

    head_masks = [lane // DK_C == h for h in range(w // DK_C)]
    n_heads = H_C
    rq = lax.broadcasted_iota(jnp.int32, (c, c), 0)
    rk = lax.broadcasted_iota(jnp.int32, (c, c), 1)

    def attend(att):
        res = _dot(att.astype(BF16), v.astype(BF16))
        out = jnp.zeros((c, w), F32)
        for h in range(n_heads):
            out = out + jnp.where(head_masks[h], res[h * c:(h + 1) * c], 0.0)
        return out

    def stack_heads(x):
        return jnp.concatenate([jnp.where(head_masks[h], x, 0.0) for h in range(n_heads)], axis=0)

    def finish(o_intra, bcum, blast):
        qe = (q * jnp.exp(bcum)).astype(BF16)
        k2 = (k * jnp.exp(blast - bcum)).astype(BF16)
        decay = jnp.exp(blast[0:1, :])
        pr = lax.broadcasted_iota(jnp.int32, (LANES, LANES), 0)
        pc = lax.broadcasted_iota(jnp.int32, (LANES, LANES), 1)
        diag = pr // DK_C == pc // DK_C
        o_state = []
        for p in range(w // LANES):
            sl = slice(p * LANES, (p + 1) * LANES)
            st = st_ref[p]
            o_state.append(_dot_nt(qe[:, sl], st.astype(BF16)))
            upd = _dot(v[:, sl].T.astype(BF16), k2[:, sl])
            st_ref[p] = st * decay[:, sl] + jnp.where(diag, upd, 0.0)
        o_ref[...] = o_intra + jnp.concatenate(o_state, axis=-1)

    ag = jnp.abs(g)
    half_bound = jnp.maximum(jnp.max(jnp.sum(ag[0:c // 2], axis=0, keepdims=True)),
                             jnp.max(jnp.sum(ag[c // 2:c], axis=0, keepdims=True)))
    single_split = half_bound <= DECAY_SPLIT_LIMIT

    @pl.when(single_split)
    def _():
        bcum = cumsum_rows()
        dmid = bcum - bcum[c // 2 - 1:c // 2, :]
        qt = stack_heads(q * jnp.exp(dmid)).astype(BF16)
        kt = (k * jnp.exp(-dmid)).astype(BF16)
        causal = jnp.concatenate([rq >= rk] * n_heads, axis=0)
        o_intra = attend(jnp.where(causal, _dot_nt(qt, kt), 0.0))
        finish(o_intra, bcum, jnp.broadcast_to(bcum[c - 1:c, :], (c, w)))

    @pl.when(jnp.logical_not(single_split))
    def _():
        cs, tots = segment_scans()
        att = jnp.zeros((n_heads * c, c), F32)
        half = GROUP
        while half < c:
            upper = (row // half) % 2 == 1
            qt = jnp.where(upper, q * jnp.exp(cs[half]), 0.0)
            kt = jnp.where(upper, 0.0, k * jnp.exp(tots[half] - cs[half])).astype(BF16)
            blk = _dot_nt(stack_heads(qt).astype(BF16), kt)
            same = (rq // (2 * half)) == (rk // (2 * half))
            same = jnp.concatenate([same] * n_heads, axis=0)
            att = att + jnp.where(same, blk, 0.0)
            half *= 2
        out = attend(att)

        def group_row(x, j):
            x3 = x.reshape(c // GROUP, GROUP, w)
            return jnp.broadcast_to(x3[:, j:j + 1, :], x3.shape).reshape(c, w)

        c8 = cs[GROUP]
        vals = []
        for j in range(GROUP):
            ok = (row % GROUP) >= j
            e = jnp.where(ok, c8 - group_row(c8, j), 0.0)
            vals.append(jnp.where(ok, q * group_row(k, j) * jnp.exp(e), 0.0))
        r_i = lax.broadcasted_iota(jnp.int32, (w, w), 0)
        c_i = lax.broadcasted_iota(jnp.int32, (w, w), 1)
        head_sum = jnp.where(r_i // DK_C == c_i // DK_C, 1.0, 0.0).astype(BF16)
        wts = _dot(jnp.concatenate(vals, axis=0).astype(BF16), head_sum)
        for j in range(GROUP):
            out = out + wts[j * c:(j + 1) * c] * group_row(v, j)
        finish(out, cs[c], tots[c])

    @pl.when(ci == pl.num_programs(1) - 1)
    def _():
        s_out_ref[...] = st_ref[...]


def _hgrn2(q, k, g, v, s0, *, c):
    b, t, w = q.shape
    kern = functools.partial(_hgrn2_kernel, c=c)
    blk = pl.BlockSpec((None, c, w), lambda bi, ci: (bi, ci, 0))
    sblk = pl.BlockSpec((None, w // LANES, LANES, LANES), lambda bi, ci: (bi, 0, 0, 0))
    return pl.pallas_call(
        kern,
        grid=(b, t // c),
        in_specs=[blk, blk, blk, blk, sblk],
        out_specs=[blk, sblk],
        out_shape=[jax.ShapeDtypeStruct((b, t, w), F32),
                   jax.ShapeDtypeStruct((b, w // LANES, LANES, LANES), F32)],
        scratch_shapes=[pltpu.VMEM((w // LANES, LANES, LANES), F32)],
        compiler_params=_cparams(2),
        name="mixer_c",
    )(q, k, g, v, s0)


def _state_to_pairs(s):
    b = s.shape[0]
    st = jnp.swapaxes(s.astype(F32), -1, -2)
    st = jnp.pad(st, ((0, 0), (0, 2 * NPAIR - H_C), (0, 0), (0, 0)))
    st = st.reshape(b, NPAIR, 2, DV_C, DK_C)
    eye = jnp.eye(2, dtype=F32)
    full = st[:, :, :, :, None, :] * eye[None, None, :, None, :, None]
    return full.reshape(b, NPAIR, 2 * DV_C, 2 * DK_C)


def _pairs_to_state(sp):
    b = sp.shape[0]
    s6 = sp.reshape(b, NPAIR, 2, DV_C, 2, DK_C)
    diag = jnp.stack([s6[:, :, a, :, a, :] for a in range(2)], axis=2)
    return jnp.swapaxes(diag.reshape(b, 2 * NPAIR, DV_C, DK_C)[:, :H_C], -1, -2)


FF_BLOCK = 2816


def _head_norm(y, gain):
    w = y.shape[-1]
    r_i = lax.broadcasted_iota(jnp.int32, (w, w), 0)
    c_i = lax.broadcasted_iota(jnp.int32, (w, w), 1)
    head_sum = jnp.where(r_i // DV_A == c_i // DV_A, 1.0, 0.0).astype(BF16)
    y2 = y * y
    hi = y2.astype(BF16)
    lo = (y2 - hi.astype(F32)).astype(BF16)
    ms = (_dot(hi, head_sum) + _dot(lo, head_sum)) * (1.0 / DV_A)
    return y * lax.rsqrt(ms + EPS) * gain


def _merge_kernel(scal_ref, x_ref, oa_ref, ob_ref, oc_ref, cg_ref, hist_ref, ag_ref, cgn_ref, wo_ref,
                  n2_ref, wup_ref, cw_ref, cb_ref, wdn_ref, fn_ref, *out_and_scratch, d_ff, final):
    if final:
        x_out_ref, fc_ref, y_ref, carry_ref = out_and_scratch
    else:
        x_out_ref, fc_ref, carry_ref = out_and_scratch
    ti = pl.program_id(1)
    tm = x_ref.shape[0]

    @pl.when(ti == 0)
    def _():
        carry_ref[...] = hist_ref[...]

    oa = _head_norm(oa_ref[...], ag_ref[...]) * scal_ref[0]
    cg = cg_ref[...]
    oc = _head_norm(oc_ref[...], cgn_ref[...]) * (cg * jax.nn.sigmoid(cg))
    mixed = (_dot(oa.astype(BF16), wo_ref[0:WA, :])
             + _dot(ob_ref[...].astype(BF16), wo_ref[WA:WA + WBP, :])
             + _dot(oc.astype(BF16), wo_ref[WA + WBP:WA + 2 * WBP, :]))
    x = x_ref[...] + mixed

    ms = jnp.mean(x * x, axis=-1, keepdims=True)
    h = (x * lax.rsqrt(ms + EPS) * n2_ref[...]).astype(BF16)
    row = lax.broadcasted_iota(jnp.int32, (tm, FF_BLOCK), 0)
    acc = jnp.zeros(x.shape, F32)
    for cblk in range(d_ff // FF_BLOCK):
        sl = slice(cblk * FF_BLOCK, (cblk + 1) * FF_BLOCK)
        a = _dot(h, wup_ref[:, sl])
        gate = _dot(h, wup_ref[:, d_ff + cblk * FF_BLOCK:d_ff + (cblk + 1) * FF_BLOCK])
        prev2 = carry_ref[0:1, sl]
        prev1 = carry_ref[1:2, sl]
        a1 = jnp.where(row == 0, prev1, pltpu.roll(a, 1, 0))
        a2 = jnp.where(row == 0, prev2, jnp.where(row == 1, prev1, pltpu.roll(a, 2, 0)))
        conv = cb_ref[:, sl] + a2 * cw_ref[0:1, sl] + a1 * cw_ref[1:2, sl] + a * cw_ref[2:3, sl]
        act = conv * jax.nn.sigmoid(conv) * gate
        acc = acc + _dot(act.astype(BF16), wdn_ref[sl, :])
        carry_ref[:, sl] = a[tm - (CONV_W - 1):, :]
    x = x + acc
    x_out_ref[...] = x
    fc_ref[...] = carry_ref[...]
    if final:
        ms = jnp.mean(x * x, axis=-1, keepdims=True)
        y_ref[...] = x * lax.rsqrt(ms + EPS) * fn_ref[...]


def _merge_ffn(scal, x, oa, ob, oc, cg, hist, a_gain, c_gain, wo, n2, wup, cw, cb, wdn, fnorm, *, tm, final):
    b, t, d = x.shape
    d_ff = wdn.shape[0]
    kern = functools.partial(_merge_kernel, d_ff=d_ff, final=final)
    blk = lambda w: pl.BlockSpec((None, tm, w), lambda bi, ti: (bi, ti, 0))
    per_b = pl.BlockSpec((None, CONV_W - 1, d_ff), lambda bi, ti: (bi, 0, 0))
    out_specs = [blk(d), per_b]
    out_shape = [jax.ShapeDtypeStruct((b, t, d), F32), jax.ShapeDtypeStruct((b, CONV_W - 1, d_ff), F32)]
    if final:
        out_specs.append(blk(d))
        out_shape.append(jax.ShapeDtypeStruct((b, t, d), F32))
    return pl.pallas_call(
        kern,
        grid=(b, t // tm),
        in_specs=[pl.BlockSpec(memory_space=pltpu.SMEM), blk(d), blk(WA), blk(WBP), blk(WBP), blk(WBP), per_b,
                  _const_spec(a_gain.shape), _const_spec(c_gain.shape), _const_spec(wo.shape),
                  _const_spec(n2.shape), _const_spec(wup.shape), _const_spec(cw.shape),
                  _const_spec(cb.shape), _const_spec(wdn.shape), _const_spec(fnorm.shape)],
        out_specs=out_specs,
        out_shape=out_shape,
        scratch_shapes=[pltpu.VMEM((CONV_W - 1, d_ff), F32)],
        compiler_params=_cparams(2),
        name="merge_ffn",
    )(scal, x, oa, ob, oc, cg, hist, a_gain, c_gain, wo, n2, wup, cw, cb, wdn, fnorm)


def _pack_w_in(w):
    parts = jnp.split(w, np.cumsum(IN_SIZES)[:-1].tolist(), axis=-1)
    cols = []
    for (name, width, padded), part in zip(SEGS, parts):
        if name == "ik":
            part = jnp.concatenate([part, part], axis=-1)
            width = 2 * D_I
        cols.append(jnp.pad(part, ((0, 0), (0, padded - width))))
    return jnp.concatenate(cols, axis=-1).astype(BF16)


def _pack_w_out(w):
    wa, wb, wc = w[:WA], w[WA:WA + WB], w[WA + WB:]
    pad = lambda m: jnp.pad(m, ((0, WBP - WB), (0, 0)))
    return jnp.concatenate([wa, pad(wb), pad(wc)], axis=0).astype(BF16)


def _rope_table(pos, dim, width):
    inv_freq = ROPE_THETA ** (-jnp.arange(0, dim, 2, dtype=F32) / dim)
    ang = pos.astype(F32)[:, None] * inv_freq[None, :]
    reps = width // (dim // 2)
    return jnp.tile(jnp.cos(ang), (1, reps)), jnp.tile(jnp.sin(ang), (1, reps))


def _cache_prep_kernel(c_ref, o_ref, *, n_feat, past):
    x = c_ref[...]
    if x.shape[0] < LANES:
        x = jnp.concatenate([x] * (LANES // x.shape[0]), axis=0)
        n_valid = LANES
    else:
        n_valid = n_feat - pl.program_id(1) * LANES
    xt = x.T
    lane = lax.broadcasted_iota(jnp.int32, xt.shape, 1)
    o_ref[0:past, :] = jnp.where(lane < n_valid, xt, 0.0).astype(BF16)
    o_ref[past:, :] = jnp.zeros((o_ref.shape[0] - past, LANES), BF16)


def _cache_prep(cache, rows):
    depth, batch, past = cache.shape[:3]
    n_feat = int(np.prod(cache.shape[3:]))
    nd = cache.ndim
    c = jnp.transpose(cache, (0, 1) + tuple(range(3, nd)) + (2,)).reshape(depth * batch, n_feat, past)
    fblk = min(LANES, n_feat)
    n_blocks = -(-n_feat // LANES)
    out = pl.pallas_call(
        functools.partial(_cache_prep_kernel, n_feat=n_feat, past=past),
        grid=(depth * batch, n_blocks),
        in_specs=[pl.BlockSpec((None, fblk, past), lambda i, j: (i, j, 0))],
        out_specs=pl.BlockSpec((None, rows, LANES), lambda i, j: (i, 0, j)),
        out_shape=jax.ShapeDtypeStruct((depth * batch, rows, n_blocks * LANES), BF16),
        compiler_params=_cparams(2),
        name="cache_prep",
    )(c)
    return out.reshape(depth, batch, rows, n_blocks * LANES)


def _pad_rows(x, rows):
    return jnp.pad(x, ((0, 0), (0, rows - x.shape[1]), (0, 0)))


def _round_up(n, m):
    return (n + m - 1) // m * m


def kernel(x_prompt, x_sample, cache_a_k, cache_a_v, cache_b_k, cache_b_v, cache_b_kidx, state_c, state_ffn_conv, norm1, w_in, lam_q1, lam_k1, lam_q2, lam_k2, a_norm, c_lower, c_norm, w_out, norm2, ffn_up, ffn_conv_w, ffn_conv_b, ffn_down, final_norm):
    depth = w_in.shape[0]
    b_p, t_p, d = x_prompt.shape
    b_s, t_s, _ = x_sample.shape
    past = cache_a_k.shape[2]
    d_ff = ffn_down.shape[1]
    kv_s = past + t_s
    n_sel_p = min(TOPK_MAX, t_p // 4)
    n_sel_s = min(TOPK_MAX, kv_s // 4)
    tk = 256
    lk_s = _round_up(kv_s, tk)

    lb_soft = jax.nn.softmax(c_lower.astype(F32), axis=0)
    lower = jnp.cumsum(lb_soft, axis=0) - lb_soft[0]
    lower = jnp.pad(lower, ((0, 0), (0, WBP - WB)))

    tm_p = min(512, t_p)
    tm_s = min(256, b_s * t_s)
    pos_p = jnp.arange(t_p)
    pos_s = jnp.tile(past + jnp.arange(t_s), tm_s // t_s)
    tabs_p = _rope_table(pos_p, DA, WA) + _rope_table(pos_p, D_B, WBP)
    tabs_s = _rope_table(pos_s, DA, WA) + _rope_table(pos_s, D_B, WBP)

    past_kv = {name: _cache_prep(c, lk_s) for name, c in
               (("ak", cache_a_k), ("av", cache_a_v), ("bk", cache_b_k), ("bv", cache_b_v), ("ik", cache_b_kidx))}

    xp, xs = x_prompt, x_sample
    caches_p, state_p, conv_p = None, [], []
    outs_s = [[] for _ in range(7)]
    y_p = y_s = None
    fnorm = final_norm.reshape(1, d)
    for l in range(depth):
        lam_init = 0.8 - 0.6 * math.exp(-0.3 * l)
        lam = (jnp.exp(jnp.sum(lam_q1[l].astype(F32) * lam_k1[l].astype(F32)))
               - jnp.exp(jnp.sum(lam_q2[l].astype(F32) * lam_k2[l].astype(F32))) + lam_init)
        lam_arr = lam.reshape(1).astype(F32)
        scal = jnp.full((1,), 1.0 - lam_init, F32)
        w_pack = _pack_w_in(w_in[l])
        wo = _pack_w_out(w_out[l])
        gain1 = norm1[l].reshape(1, d)
        gain2 = norm2[l].reshape(1, d)
        lb_row = lower[l].reshape(1, WBP)
        a_gain = jnp.tile(a_norm[l], H_A).reshape(1, WA)
        c_gain = jnp.pad(jnp.tile(c_norm[l], H_C), (0, WBP - WB)).reshape(1, WBP)
        wup = ffn_up[l].astype(BF16)
        wdn = ffn_down[l].astype(BF16)
        cw = ffn_conv_w[l]
        cb = ffn_conv_b[l].reshape(1, d_ff)
        final = l == depth - 1

        u = _in_projection(xp.reshape(b_p * t_p, d), gain1, w_pack, lb_row, tabs_p, tm_p,
                           stacked=(l, depth, b_p, caches_p))
        caches_p = {name: u[name] for name in CACHE_OUTS}
        r3 = lambda a, b=b_p, t=t_p: a.reshape(b, t, a.shape[-1])
        oa = _attention_a(lam_arr, r3(u["aq"]), r3(u["ak16"]), r3(u["av16"]),
                          q_off=0, kv_len=t_p, tq=min(256, t_p), tk=min(512, t_p))
        ob = _attention_b(r3(u["bq"]), r3(u["iq"]), r3(u["iw"]), r3(u["bk16"]), r3(u["bv16"]), r3(u["ik2"]),
                          q_off=0, kv_len=t_p, tq=min(256, t_p), n_sel=n_sel_p)
        s0 = jnp.zeros((b_p, NPAIR, LANES, LANES), F32)
        oc, s_new = _hgrn2(r3(u["cq"]), r3(u["ck"]), r3(u["cgl"]), r3(u["cv"]), s0, c=CHUNK)
        hist0 = jnp.zeros((b_p, CONV_W - 1, d_ff), F32)
        res = _merge_ffn(scal, xp, oa, ob, oc, r3(u["cg"]), hist0, a_gain, c_gain, wo, gain2, wup, cw, cb,
                         wdn, fnorm, tm=min(512, t_p), final=final)
        xp, fc = res[0], res[1]
        if final:
            y_p = res[2]
        state_p.append(_pairs_to_state(s_new))
        conv_p.append(fc)

        u = _in_projection(xs.reshape(b_s * t_s, d), gain1, w_pack, lb_row, tabs_s, tm_s)
        r3 = lambda a, b=b_s, t=t_s: a.reshape(b, t, a.shape[-1])

        def cat(name, new):
            return lax.dynamic_update_slice(past_kv[name][l], r3(new), (0, past, 0))

        oa = _attention_a(lam_arr, r3(u["aq"]), cat("ak", u["ak16"]), cat("av", u["av16"]),
                          q_off=past, kv_len=kv_s, tq=t_s, tk=tk)
        ob = _attention_b(r3(u["bq"]), r3(u["iq"]), r3(u["iw"]), cat("bk", u["bk16"]),
                          cat("bv", u["bv16"]), cat("ik", u["ik2"]),
                          q_off=past, kv_len=kv_s, tq=t_s, n_sel=n_sel_s)
        oc, s_new = _hgrn2(r3(u["cq"]), r3(u["ck"]), r3(u["cgl"]), r3(u["cv"]), _state_to_pairs(state_c[l]),
                           c=t_s)
        res = _merge_ffn(scal, xs, oa, ob, oc, r3(u["cg"]), state_ffn_conv[l].astype(F32), a_gain, c_gain, wo,
                         gain2, wup, cw, cb, wdn, fnorm, tm=t_s, final=final)
        xs, fc = res[0], res[1]
        if final:
            y_s = res[2]
        for lst, val in zip(outs_s, (u["ak"].reshape(b_s, t_s, H_A, 2 * DA), u["av"].reshape(b_s, t_s, H_A, DV_A),
                                     u["bk"].reshape(b_s, t_s, H_B, D_B), u["bv"].reshape(b_s, t_s, H_B, D_B),
                                     u["ik"].reshape(b_s, t_s, D_I), _pairs_to_state(s_new), fc)):
            lst.append(val)

    def frames_major(c, heads):
        c = c.reshape(depth, b_p, heads, c.shape[2] // heads, t_p)
        return jnp.transpose(c, (0, 1, 4, 2, 3))

    outs_p = (frames_major(caches_p["ak"], H_A), frames_major(caches_p["av"], H_A),
              frames_major(caches_p["bk"], H_B), frames_major(caches_p["bv"], H_B),
              jnp.swapaxes(caches_p["ik"], 2, 3), jnp.stack(state_p), jnp.stack(conv_p))
    return (y_p, y_s) + outs_p + tuple(jnp.stack(v) for v in outs_s)
```

```python
import functools
import math

import jax
import jax.numpy as jnp
import numpy as np
from jax import lax
from jax.experimental import pallas as pl
from jax.experimental.pallas import tpu as pltpu

F32 = jnp.float32
BF16 = jnp.bfloat16

CHUNK = 64
ROPE_THETA = 10000.0
EPS = 1e-6
NEG_BIG = -1e30
LB_FLOOR = 1e-20
H_A, DA, DV_A = 6, 32, 64
H_B, D_B = 5, 64
H_I, D_I = 4, 64
TOPK_MAX = 256
H_C, DK_C, DV_C = 5, 64, 64
CONV_W = 3

LANES = 128
SUBLANES = 8
VMEM_LIMIT = 56 * 1024 * 1024

WA = H_A * 2 * DA
WB = H_B * D_B
WBP = 384
WIQ = H_I * D_I
NPAIR = WBP // LANES

SEGS = (("aq", WA, WA), ("ak", WA, WA), ("av", WA, WA),
        ("bq", WB, WBP), ("bk", WB, WBP), ("bv", WB, WBP),
        ("iq", WIQ, WIQ), ("ik", D_I, 2 * D_I), ("iw", H_I, LANES),
        ("cq", WB, WBP), ("cf", WB, WBP), ("ci", WB, WBP), ("cg", WB, WBP))
SEG_OFF = {}
_o = 0
for _n, _w, _p in SEGS:
    SEG_OFF[_n] = (_o, _p)
    _o += _p
W_PACK = _o
PROJ_GROUPS = tuple((SEG_OFF[first][0], SEG_OFF[last][0] + SEG_OFF[last][1] - SEG_OFF[first][0])
                    for first, last in (("aq", "ak"), ("av", "bq"), ("bk", "bv"), ("iq", "iw"),
                                        ("cq", "cf"), ("ci", "cg")))
IN_SIZES = (WA, WA, WA, WB, WB, WB, WIQ, D_I, H_I, WB, WB, WB, WB)

LOG2E = math.log2(math.e)
NT_DIMS = (((1,), (1,)), ((), ()))


def _cparams(n_axes):
    return pltpu.CompilerParams(dimension_semantics=("arbitrary",) * n_axes,
                                vmem_limit_bytes=VMEM_LIMIT)


def _const_spec(shape):
    nd = len(shape)
    return pl.BlockSpec(shape, lambda *_: (0,) * nd, pipeline_mode=pl.Buffered(1))


def _dot(a, b):
    return jnp.dot(a, b, preferred_element_type=F32)


def _dot_nt(a, b):
    return lax.dot_general(a, b, NT_DIMS, preferred_element_type=F32)


def _rope(u, cos, sin, half):
    w = u.shape[-1]
    lane = lax.broadcasted_iota(jnp.int32, u.shape, 1)
    first = (lane % (2 * half)) < half
    rot = jnp.where(first, -pltpu.roll(u, w - half, 1), pltpu.roll(u, half, 1))
    return u * cos + rot * sin


N_PROJ_INPUTS = 8


def _inproj_kernel(*refs, n_aliased, feature_major_caches):
    x_ref, g_ref, w_ref, lb_ref, ca_ref, sa_ref, cb_ref, sb_ref = refs[:N_PROJ_INPUTS]
    (aq_ref, ak_ref, av_ref, ak16_ref, av16_ref, bq_ref, bk_ref, bv_ref, bk16_ref, bv16_ref,
     iq_ref, ik2_ref, ik_ref, iw_ref, cq_ref, ck_ref, cgl_ref, cv_ref, cg_ref) = refs[N_PROJ_INPUTS + n_aliased:]

    def put_cache(ref, val, width):
        ref[...] = val.T[:width, :] if feature_major_caches else val[:, :width]

    x = x_ref[...]
    ms = jnp.mean(x * x, axis=-1, keepdims=True)
    xn = (x * lax.rsqrt(ms + EPS) * g_ref[...]).astype(BF16)

    group_dots = {}

    def seg(name):
        off, width = SEG_OFF[name]
        g_off, g_width = next((o, wd) for o, wd in PROJ_GROUPS if o <= off < o + wd)
        if g_off not in group_dots:
            group_dots[g_off] = _dot(xn, w_ref[:, g_off:g_off + g_width])
        return group_dots[g_off][:, off - g_off:off - g_off + width]

    ca, sa = ca_ref[...], sa_ref[...]
    cb, sb = cb_ref[...], sb_ref[...]
    aq_ref[...] = (_rope(seg("aq"), ca, sa, DA // 2) * (DA ** -0.5 * LOG2E)).astype(BF16)
    ak = _rope(seg("ak"), ca, sa, DA // 2)
    put_cache(ak_ref, ak, WA)
    ak16_ref[...] = ak.astype(BF16)
    av = seg("av")
    put_cache(av_ref, av, WA)
    av16_ref[...] = av.astype(BF16)
    bq_ref[...] = (_rope(seg("bq"), cb, sb, D_B // 2) * (D_B ** -0.5 * LOG2E)).astype(BF16)
    bk = _rope(seg("bk"), cb, sb, D_B // 2)
    put_cache(bk_ref, bk, WB)
    bk16_ref[...] = bk.astype(BF16)
    bv = seg("bv")
    put_cache(bv_ref, bv, WB)
    bv16_ref[...] = bv.astype(BF16)
    iq_ref[...] = (_rope(seg("iq"), cb[:, :WIQ], sb[:, :WIQ], D_I // 2) * (D_I ** -0.5)).astype(BF16)
    ik2 = _rope(seg("ik"), cb[:, :2 * D_I], sb[:, :2 * D_I], D_I // 2)
    ik2_ref[...] = ik2.astype(BF16)
    put_cache(ik_ref, ik2, D_I)
    iw_ref[...] = seg("iw") * (H_I ** -0.5)

    cq = seg("cq")
    cq_ref[...] = cq * jax.nn.sigmoid(cq)
    z = seg("cf")
    lb = lb_ref[...]
    la = jnp.log(jnp.maximum(lb, LB_FLOOR))
    lsig = jnp.minimum(z, 0.0) - jnp.log1p(jnp.exp(-jnp.abs(z)))
    bb = jnp.log1p(-lb) + lsig
    cgl_ref[...] = jnp.maximum(la, bb) + jnp.log1p(jnp.exp(-jnp.abs(la - bb)))
    ck_ref[...] = (1.0 - lb) * jax.nn.sigmoid(-z)
    cv_ref[...] = seg("ci")
    cg_ref[...] = seg("cg")


CACHE_OUTS = ("ak", "av", "bk", "bv", "ik")


def _in_projection(x2d, gain, w_pack, lb_row, tabs, tm, stacked=None):
    n = x2d.shape[0]
    ca, sa, cb, sb = tabs
    period = ca.shape[0] // tm
    row = lambda w: pl.BlockSpec((tm, w), lambda i: (i, 0))
    tab = lambda w: pl.BlockSpec((tm, w), lambda i: (i % period, 0))
    outs = (("aq", WA, BF16), ("ak", WA, F32), ("av", WA, F32), ("ak16", WA, BF16), ("av16", WA, BF16),
            ("bq", WBP, BF16), ("bk", WB, F32), ("bv", WB, F32), ("bk16", WBP, BF16), ("bv16", WBP, BF16),
            ("iq", WIQ, BF16), ("ik2", 2 * D_I, BF16), ("ik", D_I, F32), ("iw", LANES, F32),
            ("cq", WBP, F32), ("ck", WBP, F32), ("cgl", WBP, F32), ("cv", WBP, F32), ("cg", WBP, F32))
    out_specs = {name: row(w) for name, w, _ in outs}
    out_shape = {name: jax.ShapeDtypeStruct((n, w), dt) for name, w, dt in outs}
    inputs = [x2d, gain, w_pack, lb_row, ca, sa, cb, sb]
    in_specs = [row(x2d.shape[1]), _const_spec(gain.shape), _const_spec(w_pack.shape),
                _const_spec(lb_row.shape), tab(WA), tab(WA), tab(WBP), tab(WBP)]
    aliases = {}
    if stacked is not None:
        layer, depth, batch, caches = stacked
        t = n // batch
        tiles = t // tm
        for name, w, _ in outs:
            if name in CACHE_OUTS:
                out_specs[name] = pl.BlockSpec((None, None, w, tm),
                                               lambda i, layer=layer: (layer, i // tiles, 0, i % tiles))
                out_shape[name] = jax.ShapeDtypeStruct((depth, batch, w, t), F32)
        if caches is not None:
            names = [o[0] for o in outs]
            for name in CACHE_OUTS:
                aliases[len(inputs)] = names.index(name)
                inputs.append(caches[name])
                in_specs.append(pl.BlockSpec(memory_space=pl.ANY))
    kern = functools.partial(_inproj_kernel, n_aliased=len(aliases), feature_major_caches=stacked is not None)
    res = pl.pallas_call(
        kern,
        grid=(n // tm,),
        in_specs=in_specs,
        out_specs=[out_specs[o[0]] for o in outs],
        out_shape=[out_shape[o[0]] for o in outs],
        input_output_aliases=aliases,
        compiler_params=_cparams(1),
        name="in_projection",
    )(*inputs)
    return dict(zip([o[0] for o in outs], res))


NARROW_KEY_BLOCK = 256


def _attn_a_kernel(lam_ref, q_ref, k_ref, v_ref, o_ref, q4_ref, m_ref, acc_ref, *, tq, tk, q_off, kv_len):
    qi = pl.program_id(2)
    q = q_ref[...]
    lane = lax.broadcasted_iota(jnp.int32, q.shape, 1)
    for i in range(4):
        q4_ref[i * tq:(i + 1) * tq, :] = jnp.where(lane // DA == i, q, jnp.zeros_like(q))
    m_ref[...] = jnp.full(m_ref.shape, NEG_BIG, F32)
    acc_ref[...] = jnp.zeros(acc_ref.shape, F32)

    q_first = q_off + qi * tq
    q_last = q_first + tq - 1
    n_full = jnp.minimum((q_first // CHUNK + 1) * CHUNK, kv_len) // tk
    lim = jnp.minimum((q_last // CHUNK + 1) * CHUNK, kv_len)
    tkn = min(tk, NARROW_KEY_BLOCK)
    n_narrow = (lim + tkn - 1) // tkn
    n_wide = n_full + (lim - n_full * tk) // tk if tkn < tk else n_narrow

    def step(j, tk, masked):
        start = pl.multiple_of(j * tk, tk)
        kb = k_ref[pl.ds(start, tk), :]
        vb = v_ref[pl.ds(start, tk), :]
        s = _dot_nt(q4_ref[...], kb)
        if masked:
            kpos = start + lax.broadcasted_iota(jnp.int32, (tq, tk), 1)
            qpos = q_first + lax.broadcasted_iota(jnp.int32, (tq, tk), 0)
            valid = (kpos // CHUNK <= qpos // CHUNK) & (kpos < kv_len)
            s = jnp.where(valid[None], s.reshape(4, tq, tk), NEG_BIG).reshape(4 * tq, tk)
        m_old = m_ref[...]
        m_new = jnp.maximum(m_old, jnp.max(s, axis=-1, keepdims=True))
        alpha = jnp.exp2(m_old - m_new)
        p = jnp.exp2(s - jnp.concatenate([m_new] * (tk // LANES), axis=1)).astype(BF16)
        lane_v = lax.broadcasted_iota(jnp.int32, vb.shape, 1)
        ones = jnp.ones_like(vb)
        pv = jnp.concatenate([_dot(p[0:2 * tq], jnp.where(lane_v < DV_A, vb, ones)),
                              _dot(p[2 * tq:4 * tq], jnp.where(lane_v < DV_A, ones, vb))], axis=0)
        acc_ref[...] = alpha * acc_ref[...] + pv
        m_ref[...] = m_new

    def body(tk, masked):
        def run(j, c):
            step(j, tk, masked)
            return c
        return run

    lax.fori_loop(0, n_full, body(tk, False), 0)
    lax.fori_loop(n_full, n_wide, body(tk, True), 0)
    if tkn < tk:
        lax.fori_loop(n_wide * (tk // tkn), n_narrow, body(tkn, True), 0)

    lam = lam_ref[0]
    acc0 = acc_ref[0:2 * tq, :]
    acc1 = acc_ref[2 * tq:4 * tq, :]
    on0 = acc0 / acc0[:, DV_A:DV_A + 1]
    on1 = acc1 / acc1[:, 0:1]
    o0 = on0[0:tq] - lam * on0[tq:2 * tq]
    o1 = on1[0:tq] - lam * on1[tq:2 * tq]
    o_ref[...] = jnp.where(lane < DV_A, o0, o1)


def _attention_a(lam, q, k, v, *, q_off, kv_len, tq, tk):
    b, t, _ = q.shape
    lk = k.shape[1]
    kern = functools.partial(_attn_a_kernel, tq=tq, tk=tk, q_off=q_off, kv_len=kv_len)
    return pl.pallas_call(
        kern,
        grid=(b, WA // LANES, t // tq),
        in_specs=[pl.BlockSpec(memory_space=pltpu.SMEM),
                  pl.BlockSpec((None, tq, LANES), lambda bi, hi, qi: (bi, qi, hi)),
                  pl.BlockSpec((None, lk, LANES), lambda bi, hi, qi: (bi, 0, hi)),
                  pl.BlockSpec((None, lk, LANES), lambda bi, hi, qi: (bi, 0, hi))],
        out_specs=pl.BlockSpec((None, tq, LANES), lambda bi, hi, qi: (bi, qi, hi)),
        out_shape=jax.ShapeDtypeStruct((b, t, WA), F32),
        scratch_shapes=[pltpu.VMEM((4 * tq, LANES), BF16), pltpu.VMEM((4 * tq, LANES), F32),
                        pltpu.VMEM((4 * tq, LANES), F32)],
        compiler_params=_cparams(3),
        name="mixer_a",
    )(lam, q, k, v)


TIE_BLOCK = 256
SEARCH_BITS_PER_TRIP = 3


def _attn_b_kernel(q_ref, iq_ref, iw_ref, k_ref, v_ref, ik_ref, o_ref, key_ref, bias_ref,
                   *, tq, q_off, kv_len, n_sel, variants):
    q_first = q_off + pl.program_id(1) * tq
    need_keys = jnp.minimum(((q_first + tq - 1) // CHUNK + 1) * CHUNK, kv_len)
    lo = 0
    for lk in variants:
        @pl.when((need_keys > lo) & (need_keys <= lk))
        def _(lk=lk):
            _attn_b_body(q_ref, iq_ref, iw_ref, k_ref, v_ref, ik_ref, o_ref, key_ref, bias_ref,
                         tq=tq, q_first=q_first, need_keys=need_keys, kv_len=kv_len, n_sel=n_sel, lk=lk)
        lo = lk


def _stack_heads(x, n_heads, width):
    lane = lax.broadcasted_iota(jnp.int32, x.shape, 1)
    return jnp.concatenate([jnp.where(lane // width == h, x, jnp.zeros_like(x)) for h in range(n_heads)],
                           axis=0)


def _attn_b_body(q_ref, iq_ref, iw_ref, k_ref, v_ref, ik_ref, o_ref, key_ref, bias_ref,
                 *, tq, q_first, need_keys, kv_len, n_sel, lk):
    ik2 = ik_ref[0:lk, :]
    iq = iq_ref[...]
    iw = iw_ref[...]

    score = jnp.zeros((tq, lk), F32)
    for pi in range(H_I // 2):
        y = _stack_heads(iq[:, LANES * pi:LANES * (pi + 1)], 2, D_I)
        d = jnp.maximum(_dot_nt(y, ik2), 0.0)
        score = score + iw[:, 2 * pi:2 * pi + 1] * d[0:tq] + iw[:, 2 * pi + 1:2 * pi + 2] * d[tq:2 * tq]

    kpos = lax.broadcasted_iota(jnp.int32, (tq, lk), 1)
    qpos = q_first + lax.broadcasted_iota(jnp.int32, (tq, lk), 0)
    valid = (kpos // CHUNK <= qpos // CHUNK) & (kpos < kv_len)
    score = jnp.where(valid, score, NEG_BIG)

    bits = lax.bitcast_convert_type(score, jnp.int32)
    key = bits ^ ((bits >> 31) & jnp.int32(0x7FFFFFFF))
    key = jnp.where(key == -1, 0, key)
    key_ref[:, 0:lk] = key

    kf = float(n_sel)

    def count_ge(cand):
        wide = jnp.concatenate([cand] * (lk // LANES), axis=1)
        return jnp.sum(jnp.where(key_ref[:, 0:lk] >= wide, 1.0, 0.0), axis=-1, keepdims=True)

    int_min = jnp.int32(-2 ** 31)
    zero = jnp.zeros((tq, LANES), jnp.int32)
    select_all = need_keys <= n_sel
    cur = jnp.where(select_all | (count_ge(zero) < kf), int_min, zero)

    def try_bit(cur, bit):
        cand = cur | bit
        return jnp.where(count_ge(cand) >= kf, cand, cur)

    cur = jnp.where(select_all, int_min, try_bit(cur, jnp.int32(1 << 30)))

    def bits_body(i, cur):
        for j in range(SEARCH_BITS_PER_TRIP):
            cur = try_bit(cur, jnp.left_shift(jnp.int32(1), 29 - (SEARCH_BITS_PER_TRIP * i + j)))
        return cur

    thr = lax.fori_loop(0, jnp.where(select_all, 0, 30 // SEARCH_BITS_PER_TRIP), bits_body, cur)

    key = key_ref[:, 0:lk]
    thr = jnp.concatenate([thr] * (lk // LANES), axis=1)
    gt = key > thr
    eq = key == thr
    need = kf - jnp.sum(jnp.where(gt, 1.0, 0.0), axis=-1, keepdims=True)
    r_i = lax.broadcasted_iota(jnp.int32, (TIE_BLOCK, TIE_BLOCK), 0)
    c_i = lax.broadcasted_iota(jnp.int32, (TIE_BLOCK, TIE_BLOCK), 1)
    tri = jnp.where(r_i <= c_i, 1.0, 0.0).astype(BF16)
    carry = jnp.zeros((tq, 1), F32)
    for jb in range(lk // TIE_BLOCK):
        sl = slice(jb * TIE_BLOCK, (jb + 1) * TIE_BLOCK)
        eq_b = eq[:, sl]
        pref = _dot(jnp.where(eq_b, 1.0, 0.0).astype(BF16), tri) + carry
        carry = pref[:, TIE_BLOCK - 1:TIE_BLOCK]
        sel = (gt[:, sl] | (eq_b & (pref <= need))) & valid[:, sl]
        bias_ref[:, sl] = jnp.where(sel, 0.0, NEG_BIG)

    q = q_ref[...]
    lane = lax.broadcasted_iota(jnp.int32, (tq, LANES), 1)
    bias = bias_ref[:, 0:lk]
    for p in range(NPAIR):
        n_heads = min(2, H_B - 2 * p)
        sl = slice(p * LANES, (p + 1) * LANES)
        y = _stack_heads(q[:, sl], n_heads, D_B)
        s = _dot_nt(y, k_ref[0:lk, sl]).reshape(n_heads, tq, lk) + bias[None]
        m = jnp.max(s, axis=-1, keepdims=True)
        pr = jnp.exp2(s - m)
        l = jnp.sum(pr, axis=-1, keepdims=True)
        o = _dot(pr.reshape(n_heads * tq, lk).astype(BF16), v_ref[0:lk, sl]) / l.reshape(n_heads * tq, 1)
        o_ref[:, sl] = o if n_heads == 1 else jnp.where(lane < D_B, o[0:tq], o[tq:2 * tq])


def _attention_b(q, iq, iw, k, v, ik2, *, q_off, kv_len, tq, n_sel):
    b, t, _ = q.shape
    lk = k.shape[1]
    granule = tq if (tq % TIE_BLOCK == 0 and lk % tq == 0) else lk
    variants = tuple(range(granule, lk + 1, granule))
    kern = functools.partial(_attn_b_kernel, tq=tq, q_off=q_off, kv_len=kv_len, n_sel=n_sel,
                             variants=variants)
    qspec = lambda w: pl.BlockSpec((None, tq, w), lambda bi, qi: (bi, qi, 0))
    kspec = lambda w: pl.BlockSpec((None, lk, w), lambda bi, qi: (bi, 0, 0))
    return pl.pallas_call(
        kern,
        grid=(b, t // tq),
        in_specs=[qspec(WBP), qspec(WIQ), qspec(LANES), kspec(WBP), kspec(WBP), kspec(2 * D_I)],
        out_specs=qspec(WBP),
        out_shape=jax.ShapeDtypeStruct((b, t, WBP), F32),
        scratch_shapes=[pltpu.VMEM((tq, lk), jnp.int32), pltpu.VMEM((tq, lk), F32)],
        compiler_params=_cparams(2),
        name="mixer_b",
    )(q, iq, iw, k, v, ik2)


HGRN2_CHUNKS_PER_STEP = 2
GROUP = SUBLANES
DECAY_SPLIT_LIMIT = 60.0


def _hgrn2_kernel(q_ref, k_ref, g_ref, v_ref, s0_ref, o_ref, s_out_ref, st_ref, *, c, chunks):
    ci = pl.program_id(1)

    @pl.when(ci == 0)
    def _():
        st_ref[...] = s0_ref[...]

    for i in range(chunks):
        rows = pl.ds(i * c, c)
        _hgrn2_chunk(q_ref.at[rows], k_ref.at[rows], g_ref.at[rows], v_ref.at[rows], o_ref.at[rows], st_ref, c=c)

    @pl.when(ci == pl.num_programs(1) - 1)
    def _():
        s_out_ref[...] = st_ref[...]


def _hgrn2_chunk(q_ref, k_ref, g_ref, v_ref, o_ref, st_ref, *, c):
    q = q_ref[...]
    k = k_ref[...]
    g = g_ref[...]
    v = v_ref[...]
    w = q.shape[-1]
    row = lax.broadcasted_iota(jnp.int32, (c, w), 0)
    lane = lax.broadcasted_iota(jnp.int32, (c, w), 1)

    def segment_scans():
        cs, tots = {1: g}, {1: g}
        cum, tot, m = g, g, 1
        while m < c:
            upper = (row // m) % 2 == 1
            prev_tot = pltpu.roll(tot, m, 0)
            next_tot = pltpu.roll(tot, c - m, 0)
            cum = cum + jnp.where(upper, prev_tot, 0.0)
            tot = tot + jnp.where(upper, prev_tot, next_tot)
            m *= 2
            cs[m], tots[m] = cum, tot
        return cs, tots

    def cumsum_rows():
        cum, m = g, 1
        while m < c:
            cum = cum + jnp.where(row >= m, pltpu.roll(cum, m, 0), 0.0)
            m *= 2
        return cum

    head_masks = [lane // DK_C == h for h in range(w // DK_C)]
    n_heads = H_C
    rq = lax.broadcasted_iota(jnp.int32, (c, c), 0)
    rk = lax.broadcasted_iota(jnp.int32, (c, c), 1)

    def attend(att):
        res = _dot(att.astype(BF16), v.astype(BF16))
        out = jnp.zeros((c, w), F32)
        for h in range(n_heads):
            out = out + jnp.where(head_masks[h], res[h * c:(h + 1) * c], 0.0)
        return out

    def stack_heads(x):
        return jnp.concatenate([jnp.where(head_masks[h], x, 0.0) for h in range(n_heads)], axis=0)

    def finish(o_intra, bcum, blast):
        qe = (q * jnp.exp(bcum)).astype(BF16)
        k2 = (k * jnp.exp(blast - bcum)).astype(BF16)
        decay = jnp.exp(blast[0:1, :])
        pr = lax.broadcasted_iota(jnp.int32, (LANES, LANES), 0)
        pc = lax.broadcasted_iota(jnp.int32, (LANES, LANES), 1)
        diag = pr // DK_C == pc // DK_C
        o_state = []
        for p in range(w // LANES):
            sl = slice(p * LANES, (p + 1) * LANES)
            st = st_ref[p]
            o_state.append(_dot_nt(qe[:, sl], st.astype(BF16)))
            upd = _dot(v[:, sl].T.astype(BF16), k2[:, sl])
            st_ref[p] = st * decay[:, sl] + jnp.where(diag, upd, 0.0)
        o_ref[...] = o_intra + jnp.concatenate(o_state, axis=-1)

    ag = jnp.abs(g)
    half_bound = jnp.maximum(jnp.max(jnp.sum(ag[0:c // 2], axis=0, keepdims=True)),
                             jnp.max(jnp.sum(ag[c // 2:c], axis=0, keepdims=True)))
    single_split = half_bound <= DECAY_SPLIT_LIMIT

    @pl.when(single_split)
    def _():
        bcum = cumsum_rows()
        dmid = bcum - bcum[c // 2 - 1:c // 2, :]
        qt = stack_heads(q * jnp.exp(dmid)).astype(BF16)
        kt = (k * jnp.exp(-dmid)).astype(BF16)
        causal = jnp.concatenate([rq >= rk] * n_heads, axis=0)
        o_intra = attend(jnp.where(causal, _dot_nt(qt, kt), 0.0))
        finish(o_intra, bcum, jnp.broadcast_to(bcum[c - 1:c, :], (c, w)))

    @pl.when(jnp.logical_not(single_split))
    def _():
        cs, tots = segment_scans()
        att = jnp.zeros((n_heads * c, c), F32)
        half = GROUP
        while half < c:
            upper = (row // half) % 2 == 1
            qt = jnp.where(upper, q * jnp.exp(cs[half]), 0.0)
            kt = jnp.where(upper, 0.0, k * jnp.exp(tots[half] - cs[half])).astype(BF16)
            blk = _dot_nt(stack_heads(qt).astype(BF16), kt)
            same = (rq // (2 * half)) == (rk // (2 * half))
            same = jnp.concatenate([same] * n_heads, axis=0)
            att = att + jnp.where(same, blk, 0.0)
            half *= 2
        out = attend(att)

        def group_row(x, j):
            x3 = x.reshape(c // GROUP, GROUP, w)
            return jnp.broadcast_to(x3[:, j:j + 1, :], x3.shape).reshape(c, w)

        c8 = cs[GROUP]
        vals = []
        for j in range(GROUP):
            ok = (row % GROUP) >= j
            e = jnp.where(ok, c8 - group_row(c8, j), 0.0)
            vals.append(jnp.where(ok, q * group_row(k, j) * jnp.exp(e), 0.0))
        r_i = lax.broadcasted_iota(jnp.int32, (w, w), 0)
        c_i = lax.broadcasted_iota(jnp.int32, (w, w), 1)
        head_sum = jnp.where(r_i // DK_C == c_i // DK_C, 1.0, 0.0).astype(BF16)
        wts = _dot(jnp.concatenate(vals, axis=0).astype(BF16), head_sum)
        for j in range(GROUP):
            out = out + wts[j * c:(j + 1) * c] * group_row(v, j)
        finish(out, cs[c], tots[c])


def _hgrn2(q, k, g, v, s0, *, c):
    b, t, w = q.shape
    chunks = HGRN2_CHUNKS_PER_STEP if t % (HGRN2_CHUNKS_PER_STEP * c) == 0 else 1
    kern = functools.partial(_hgrn2_kernel, c=c, chunks=chunks)
    blk = pl.BlockSpec((None, chunks * c, w), lambda bi, ci: (bi, ci, 0))
    sblk = pl.BlockSpec((None, w // LANES, LANES, LANES), lambda bi, ci: (bi, 0, 0, 0))
    return pl.pallas_call(
        kern,
        grid=(b, t // (chunks * c)),
        in_specs=[blk, blk, blk, blk, sblk],
        out_specs=[blk, sblk],
        out_shape=[jax.ShapeDtypeStruct((b, t, w), F32),
                   jax.ShapeDtypeStruct((b, w // LANES, LANES, LANES), F32)],
        scratch_shapes=[pltpu.VMEM((w // LANES, LANES, LANES), F32)],
        compiler_params=_cparams(2),
        name="mixer_c",
    )(q, k, g, v, s0)


def _state_to_pairs(s):
    b = s.shape[0]
    st = jnp.swapaxes(s.astype(F32), -1, -2)
    st = jnp.pad(st, ((0, 0), (0, 2 * NPAIR - H_C), (0, 0), (0, 0)))
    st = st.reshape(b, NPAIR, 2, DV_C, DK_C)
    eye = jnp.eye(2, dtype=F32)
    full = st[:, :, :, :, None, :] * eye[None, None, :, None, :, None]
    return full.reshape(b, NPAIR, 2 * DV_C, 2 * DK_C)


def _pairs_to_state(sp):
    b = sp.shape[0]
    s6 = sp.reshape(b, NPAIR, 2, DV_C, 2, DK_C)
    diag = jnp.stack([s6[:, :, a, :, a, :] for a in range(2)], axis=2)
    return jnp.swapaxes(diag.reshape(b, 2 * NPAIR, DV_C, DK_C)[:, :H_C], -1, -2)


FF_BLOCK = 2816


def _head_norm(y, gain):
    w = y.shape[-1]
    r_i = lax.broadcasted_iota(jnp.int32, (w, w), 0)
    c_i = lax.broadcasted_iota(jnp.int32, (w, w), 1)
    head_sum = jnp.where(r_i // DV_A == c_i // DV_A, 1.0, 0.0).astype(BF16)
    y2 = y * y
    hi = y2.astype(BF16)
    lo = (y2 - hi.astype(F32)).astype(BF16)
    ms = (_dot(hi, head_sum) + _dot(lo, head_sum)) * (1.0 / DV_A)
    return y * lax.rsqrt(ms + EPS) * gain


def _merge_kernel(scal_ref, x_ref, oa_ref, ob_ref, oc_ref, cg_ref, hist_ref, ag_ref, cgn_ref, wo_ref,
                  n2_ref, wup_ref, cw_ref, cb_ref, wdn_ref, fn_ref, *out_and_scratch, d_ff, final):
    if final:
        x_out_ref, fc_ref, y_ref, carry_ref = out_and_scratch
    else:
        x_out_ref, fc_ref, carry_ref = out_and_scratch
    ti = pl.program_id(1)
    tm = x_ref.shape[0]

    @pl.when(ti == 0)
    def _():
        carry_ref[...] = hist_ref[...]

    oa = _head_norm(oa_ref[...], ag_ref[...]) * scal_ref[0]
    cg = cg_ref[...]
    oc = _head_norm(oc_ref[...], cgn_ref[...]) * (cg * jax.nn.sigmoid(cg))
    mixed = (_dot(oa.astype(BF16), wo_ref[0:WA, :])
             + _dot(ob_ref[...].astype(BF16), wo_ref[WA:WA + WBP, :])
             + _dot(oc.astype(BF16), wo_ref[WA + WBP:WA + 2 * WBP, :]))
    x = x_ref[...] + mixed

    ms = jnp.mean(x * x, axis=-1, keepdims=True)
    h = (x * lax.rsqrt(ms + EPS) * n2_ref[...]).astype(BF16)
    row = lax.broadcasted_iota(jnp.int32, (tm, FF_BLOCK), 0)
    acc = jnp.zeros(x.shape, F32)
    for cblk in range(d_ff // FF_BLOCK):
        sl = slice(cblk * FF_BLOCK, (cblk + 1) * FF_BLOCK)
        a = _dot(h, wup_ref[:, sl])
        gate = _dot(h, wup_ref[:, d_ff + cblk * FF_BLOCK:d_ff + (cblk + 1) * FF_BLOCK])
        prev2 = carry_ref[0:1, sl]
        prev1 = carry_ref[1:2, sl]
        a1 = jnp.where(row == 0, prev1, pltpu.roll(a, 1, 0))
        a2 = jnp.where(row == 0, prev2, jnp.where(row == 1, prev1, pltpu.roll(a, 2, 0)))
        conv = cb_ref[:, sl] + a2 * cw_ref[0:1, sl] + a1 * cw_ref[1:2, sl] + a * cw_ref[2:3, sl]
        act = conv * jax.nn.sigmoid(conv) * gate
        acc = acc + _dot(act.astype(BF16), wdn_ref[sl, :])
        carry_ref[:, sl] = a[tm - (CONV_W - 1):, :]
    x = x + acc
    x_out_ref[...] = x
    fc_ref[...] = carry_ref[...]
    if final:
        ms = jnp.mean(x * x, axis=-1, keepdims=True)
        y_ref[...] = x * lax.rsqrt(ms + EPS) * fn_ref[...]


def _merge_ffn(scal, x, oa, ob, oc, cg, hist, a_gain, c_gain, wo, n2, wup, cw, cb, wdn, fnorm, *, tm, final):
    b, t, d = x.shape
    d_ff = wdn.shape[0]
    kern = functools.partial(_merge_kernel, d_ff=d_ff, final=final)
    blk = lambda w: pl.BlockSpec((None, tm, w), lambda bi, ti: (bi, ti, 0))
    per_b = pl.BlockSpec((None, CONV_W - 1, d_ff), lambda bi, ti: (bi, 0, 0))
    out_specs = [blk(d), per_b]
    out_shape = [jax.ShapeDtypeStruct((b, t, d), F32), jax.ShapeDtypeStruct((b, CONV_W - 1, d_ff), F32)]
    if final:
        out_specs.append(blk(d))
        out_shape.append(jax.ShapeDtypeStruct((b, t, d), F32))
    return pl.pallas_call(
        kern,
        grid=(b, t // tm),
        in_specs=[pl.BlockSpec(memory_space=pltpu.SMEM), blk(d), blk(WA), blk(WBP), blk(WBP), blk(WBP), per_b,
                  _const_spec(a_gain.shape), _const_spec(c_gain.shape), _const_spec(wo.shape),
                  _const_spec(n2.shape), _const_spec(wup.shape), _const_spec(cw.shape),
                  _const_spec(cb.shape), _const_spec(wdn.shape), _const_spec(fnorm.shape)],
        out_specs=out_specs,
        out_shape=out_shape,
        scratch_shapes=[pltpu.VMEM((CONV_W - 1, d_ff), F32)],
        compiler_params=_cparams(2),
        name="merge_ffn",
    )(scal, x, oa, ob, oc, cg, hist, a_gain, c_gain, wo, n2, wup, cw, cb, wdn, fnorm)


def _pack_w_in(w):
    parts = jnp.split(w, np.cumsum(IN_SIZES)[:-1].tolist(), axis=-1)
    cols = []
    for (name, width, padded), part in zip(SEGS, parts):
        if name == "ik":
            part = jnp.concatenate([part, part], axis=-1)
            width = 2 * D_I
        cols.append(jnp.pad(part, ((0, 0), (0, padded - width))))
    return jnp.concatenate(cols, axis=-1).astype(BF16)


def _pack_w_out(w):
    wa, wb, wc = w[:WA], w[WA:WA + WB], w[WA + WB:]
    pad = lambda m: jnp.pad(m, ((0, WBP - WB), (0, 0)))
    return jnp.concatenate([wa, pad(wb), pad(wc)], axis=0).astype(BF16)


def _rope_table(pos, dim, width):
    inv_freq = ROPE_THETA ** (-jnp.arange(0, dim, 2, dtype=F32) / dim)
    ang = pos.astype(F32)[:, None] * inv_freq[None, :]
    reps = width // (dim // 2)
    return jnp.tile(jnp.cos(ang), (1, reps)), jnp.tile(jnp.sin(ang), (1, reps))


def _cache_prep_kernel(c_ref, o_ref, *, n_feat, past):
    x = c_ref[...]
    if x.shape[0] < LANES:
        x = jnp.concatenate([x] * (LANES // x.shape[0]), axis=0)
        n_valid = LANES
    else:
        n_valid = n_feat - pl.program_id(1) * LANES
    xt = x.T
    lane = lax.broadcasted_iota(jnp.int32, xt.shape, 1)
    o_ref[0:past, :] = jnp.where(lane < n_valid, xt, 0.0).astype(BF16)
    o_ref[past:, :] = jnp.zeros((o_ref.shape[0] - past, LANES), BF16)


def _cache_prep(cache, rows):
    depth, batch, past = cache.shape[:3]
    n_feat = int(np.prod(cache.shape[3:]))
    nd = cache.ndim
    c = jnp.transpose(cache, (0, 1) + tuple(range(3, nd)) + (2,)).reshape(depth * batch, n_feat, past)
    fblk = min(LANES, n_feat)
    n_blocks = -(-n_feat // LANES)
    out = pl.pallas_call(
        functools.partial(_cache_prep_kernel, n_feat=n_feat, past=past),
        grid=(depth * batch, n_blocks),
        in_specs=[pl.BlockSpec((None, fblk, past), lambda i, j: (i, j, 0))],
        out_specs=pl.BlockSpec((None, rows, LANES), lambda i, j: (i, 0, j)),
        out_shape=jax.ShapeDtypeStruct((depth * batch, rows, n_blocks * LANES), BF16),
        compiler_params=_cparams(2),
        name="cache_prep",
    )(c)
    return out.reshape(depth, batch, rows, n_blocks * LANES)


def _pad_rows(x, rows):
    return jnp.pad(x, ((0, 0), (0, rows - x.shape[1]), (0, 0)))


def _round_up(n, m):
    return (n + m - 1) // m * m


def kernel(x_prompt, x_sample, cache_a_k, cache_a_v, cache_b_k, cache_b_v, cache_b_kidx, state_c, state_ffn_conv, norm1, w_in, lam_q1, lam_k1, lam_q2, lam_k2, a_norm, c_lower, c_norm, w_out, norm2, ffn_up, ffn_conv_w, ffn_conv_b, ffn_down, final_norm):
    depth = w_in.shape[0]
    b_p, t_p, d = x_prompt.shape
    b_s, t_s, _ = x_sample.shape
    past = cache_a_k.shape[2]
    d_ff = ffn_down.shape[1]
    kv_s = past + t_s
    n_sel_p = min(TOPK_MAX, t_p // 4)
    n_sel_s = min(TOPK_MAX, kv_s // 4)
    tk = 256
    lk_s = _round_up(kv_s, tk)

    lb_soft = jax.nn.softmax(c_lower.astype(F32), axis=0)
    lower = jnp.cumsum(lb_soft, axis=0) - lb_soft[0]
    lower = jnp.pad(lower, ((0, 0), (0, WBP - WB)))

    tm_p = min(512, t_p)
    tm_s = min(256, b_s * t_s)
    pos_p = jnp.arange(t_p)
    pos_s = jnp.tile(past + jnp.arange(t_s), tm_s // t_s)
    tabs_p = _rope_table(pos_p, DA, WA) + _rope_table(pos_p, D_B, WBP)
    tabs_s = _rope_table(pos_s, DA, WA) + _rope_table(pos_s, D_B, WBP)

    past_kv = {name: _cache_prep(c, lk_s) for name, c in
               (("ak", cache_a_k), ("av", cache_a_v), ("bk", cache_b_k), ("bv", cache_b_v), ("ik", cache_b_kidx))}

    xp, xs = x_prompt, x_sample
    caches_p, state_p, conv_p = None, [], []
    outs_s = [[] for _ in range(7)]
    y_p = y_s = None
    fnorm = final_norm.reshape(1, d)
    for l in range(depth):
        lam_init = 0.8 - 0.6 * math.exp(-0.3 * l)
        lam = (jnp.exp(jnp.sum(lam_q1[l].astype(F32) * lam_k1[l].astype(F32)))
               - jnp.exp(jnp.sum(lam_q2[l].astype(F32) * lam_k2[l].astype(F32))) + lam_init)
        lam_arr = lam.reshape(1).astype(F32)
        scal = jnp.full((1,), 1.0 - lam_init, F32)
        w_pack = _pack_w_in(w_in[l])
        wo = _pack_w_out(w_out[l])
        gain1 = norm1[l].reshape(1, d)
        gain2 = norm2[l].reshape(1, d)
        lb_row = lower[l].reshape(1, WBP)
        a_gain = jnp.tile(a_norm[l], H_A).reshape(1, WA)
        c_gain = jnp.pad(jnp.tile(c_norm[l], H_C), (0, WBP - WB)).reshape(1, WBP)
        wup = ffn_up[l].astype(BF16)
        wdn = ffn_down[l].astype(BF16)
        cw = ffn_conv_w[l]
        cb = ffn_conv_b[l].reshape(1, d_ff)
        final = l == depth - 1

        u = _in_projection(xp.reshape(b_p * t_p, d), gain1, w_pack, lb_row, tabs_p, tm_p,
                           stacked=(l, depth, b_p, caches_p))
        caches_p = {name: u[name] for name in CACHE_OUTS}
        r3 = lambda a, b=b_p, t=t_p: a.reshape(b, t, a.shape[-1])
        oa = _attention_a(lam_arr, r3(u["aq"]), r3(u["ak16"]), r3(u["av16"]),
                          q_off=0, kv_len=t_p, tq=min(256, t_p), tk=min(512, t_p))
        ob = _attention_b(r3(u["bq"]), r3(u["iq"]), r3(u["iw"]), r3(u["bk16"]), r3(u["bv16"]), r3(u["ik2"]),
                          q_off=0, kv_len=t_p, tq=min(256, t_p), n_sel=n_sel_p)
        s0 = jnp.zeros((b_p, NPAIR, LANES, LANES), F32)
        oc, s_new = _hgrn2(r3(u["cq"]), r3(u["ck"]), r3(u["cgl"]), r3(u["cv"]), s0, c=CHUNK)
        hist0 = jnp.zeros((b_p, CONV_W - 1, d_ff), F32)
        res = _merge_ffn(scal, xp, oa, ob, oc, r3(u["cg"]), hist0, a_gain, c_gain, wo, gain2, wup, cw, cb,
                         wdn, fnorm, tm=min(512, t_p), final=final)
        xp, fc = res[0], res[1]
        if final:
            y_p = res[2]
        state_p.append(_pairs_to_state(s_new))
        conv_p.append(fc)

        u = _in_projection(xs.reshape(b_s * t_s, d), gain1, w_pack, lb_row, tabs_s, tm_s)
        r3 = lambda a, b=b_s, t=t_s: a.reshape(b, t, a.shape[-1])

        def cat(name, new):
            return lax.dynamic_update_slice(past_kv[name][l], r3(new), (0, past, 0))

        oa = _attention_a(lam_arr, r3(u["aq"]), cat("ak", u["ak16"]), cat("av", u["av16"]),
                          q_off=past, kv_len=kv_s, tq=t_s, tk=tk)
        ob = _attention_b(r3(u["bq"]), r3(u["iq"]), r3(u["iw"]), cat("bk", u["bk16"]),
                          cat("bv", u["bv16"]), cat("ik", u["ik2"]),
                          q_off=past, kv_len=kv_s, tq=t_s, n_sel=n_sel_s)
        oc, s_new = _hgrn2(r3(u["cq"]), r3(u["ck"]), r3(u["cgl"]), r3(u["cv"]), _state_to_pairs(state_c[l]),
                           c=t_s)
        res = _merge_ffn(scal, xs, oa, ob, oc, r3(u["cg"]), state_ffn_conv[l].astype(F32), a_gain, c_gain, wo,
                         gain2, wup, cw, cb, wdn, fnorm, tm=t_s, final=final)
        xs, fc = res[0], res[1]
        if final:
            y_s = res[2]
        for lst, val in zip(outs_s, (u["ak"].reshape(b_s, t_s, H_A, 2 * DA), u["av"].reshape(b_s, t_s, H_A, DV_A),
                                     u["bk"].reshape(b_s, t_s, H_B, D_B), u["bv"].reshape(b_s, t_s, H_B, D_B),
                                     u["ik"].reshape(b_s, t_s, D_I), _pairs_to_state(s_new), fc)):
            lst.append(val)

    def frames_major(c, heads):
        c = c.reshape(depth, b_p, heads, c.shape[2] // heads, t_p)
        return jnp.transpose(c, (0, 1, 4, 2, 3))

    outs_p = (frames_major(caches_p["ak"], H_A), frames_major(caches_p["av"], H_A),
              frames_major(caches_p["bk"], H_B), frames_major(caches_p["bv"], H_B),
              jnp.swapaxes(caches_p["ik"], 2, 3), jnp.stack(state_p), jnp.stack(conv_p))
    return (y_p, y_s) + outs_p + tuple(jnp.stack(v) for v in outs_s)
```

```python
import functools
import math

import jax
import jax.numpy as jnp
import numpy as np
from jax import lax
from jax.experimental import pallas as pl
from jax.experimental.pallas import tpu as pltpu

F32 = jnp.float32
BF16 = jnp.bfloat16

CHUNK = 64
ROPE_THETA = 10000.0
EPS = 1e-6
NEG_BIG = -1e30
LB_FLOOR = 1e-20
H_A, DA, DV_A = 6, 32, 64
H_B, D_B = 5, 64
H_I, D_I = 4, 64
TOPK_MAX = 256
H_C, DK_C, DV_C = 5, 64, 64
CONV_W = 3

LANES = 128
SUBLANES = 8
VMEM_LIMIT = 56 * 1024 * 1024

WA = H_A * 2 * DA
WB = H_B * D_B
WBP = 384
WIQ = H_I * D_I
NPAIR = WBP // LANES

SEGS = (("aq", WA, WA), ("ak", WA, WA), ("av", WA, WA),
        ("bq", WB, WBP), ("bk", WB, WBP), ("bv", WB, WBP),
        ("iq", WIQ, WIQ), ("ik", D_I, 2 * D_I), ("iw", H_I, LANES),
        ("cq", WB, WBP), ("cf", WB, WBP), ("ci", WB, WBP), ("cg", WB, WBP))
SEG_OFF = {}
_o = 0
for _n, _w, _p in SEGS:
    SEG_OFF[_n] = (_o, _p)
    _o += _p
W_PACK = _o
PROJ_GROUPS = tuple((SEG_OFF[first][0], SEG_OFF[last][0] + SEG_OFF[last][1] - SEG_OFF[first][0])
                    for first, last in (("aq", "ak"), ("av", "bq"), ("bk", "bv"), ("iq", "iw"),
                                        ("cq", "cf"), ("ci", "cg")))
IN_SIZES = (WA, WA, WA, WB, WB, WB, WIQ, D_I, H_I, WB, WB, WB, WB)

LOG2E = math.log2(math.e)
NT_DIMS = (((1,), (1,)), ((), ()))


def _cparams(n_axes):
    return pltpu.CompilerParams(dimension_semantics=("arbitrary",) * n_axes,
                                vmem_limit_bytes=VMEM_LIMIT)


def _const_spec(shape):
    nd = len(shape)
    return pl.BlockSpec(shape, lambda *_: (0,) * nd, pipeline_mode=pl.Buffered(1))


def _dot(a, b):
    return jnp.dot(a, b, preferred_element_type=F32)


def _dot_nt(a, b):
    return lax.dot_general(a, b, NT_DIMS, preferred_element_type=F32)


def _rope(u, cos, sin, half):
    w = u.shape[-1]
    lane = lax.broadcasted_iota(jnp.int32, u.shape, 1)
    first = (lane % (2 * half)) < half
    rot = jnp.where(first, -pltpu.roll(u, w - half, 1), pltpu.roll(u, half, 1))
    return u * cos + rot * sin


N_PROJ_INPUTS = 8


def _inproj_kernel(*refs, n_aliased, feature_major_caches):
    x_ref, g_ref, w_ref, lb_ref, ca_ref, sa_ref, cb_ref, sb_ref = refs[:N_PROJ_INPUTS]
    (aq_ref, ak_ref, av_ref, ak16_ref, av16_ref, bq_ref, bk_ref, bv_ref, bk16_ref, bv16_ref,
     iq_ref, ik2_ref, ik_ref, iw_ref, cq_ref, ck_ref, cgl_ref, cv_ref, cg_ref) = refs[N_PROJ_INPUTS + n_aliased:]

    def put_cache(ref, val, width):
        ref[...] = val.T[:width, :] if feature_major_caches else val[:, :width]

    x = x_ref[...]
    ms = jnp.mean(x * x, axis=-1, keepdims=True)
    xn = (x * lax.rsqrt(ms + EPS) * g_ref[...]).astype(BF16)

    group_dots = {}

    def seg(name):
        off, width = SEG_OFF[name]
        g_off, g_width = next((o, wd) for o, wd in PROJ_GROUPS if o <= off < o + wd)
        if g_off not in group_dots:
            group_dots[g_off] = _dot(xn, w_ref[:, g_off:g_off + g_width])
        return group_dots[g_off][:, off - g_off:off - g_off + width]

    ca, sa = ca_ref[...], sa_ref[...]
    cb, sb = cb_ref[...], sb_ref[...]
    aq_ref[...] = (_rope(seg("aq"), ca, sa, DA // 2) * (DA ** -0.5 * LOG2E)).astype(BF16)
    ak = _rope(seg("ak"), ca, sa, DA // 2)
    put_cache(ak_ref, ak, WA)
    ak16_ref[...] = ak.astype(BF16)
    av = seg("av")
    put_cache(av_ref, av, WA)
    av16_ref[...] = av.astype(BF16)
    bq_ref[...] = (_rope(seg("bq"), cb, sb, D_B // 2) * (D_B ** -0.5 * LOG2E)).astype(BF16)
    bk = _rope(seg("bk"), cb, sb, D_B // 2)
    put_cache(bk_ref, bk, WB)
    bk16_ref[...] = bk.astype(BF16)
    bv = seg("bv")
    put_cache(bv_ref, bv, WB)
    bv16_ref[...] = bv.astype(BF16)
    iq_ref[...] = (_rope(seg("iq"), cb[:, :WIQ], sb[:, :WIQ], D_I // 2) * (D_I ** -0.5)).astype(BF16)
    ik2 = _rope(seg("ik"), cb[:, :2 * D_I], sb[:, :2 * D_I], D_I // 2)
    ik2_ref[...] = ik2.astype(BF16)
    put_cache(ik_ref, ik2, D_I)
    iw_ref[...] = seg("iw") * (H_I ** -0.5)

    cq = seg("cq")
    cq_ref[...] = cq * jax.nn.sigmoid(cq)
    z = seg("cf")
    lb = lb_ref[...]
    la = jnp.log(jnp.maximum(lb, LB_FLOOR))
    lsig = jnp.minimum(z, 0.0) - jnp.log1p(jnp.exp(-jnp.abs(z)))
    bb = jnp.log1p(-lb) + lsig
    cgl_ref[...] = jnp.maximum(la, bb) + jnp.log1p(jnp.exp(-jnp.abs(la - bb)))
    ck_ref[...] = (1.0 - lb) * jax.nn.sigmoid(-z)
    cv_ref[...] = seg("ci")
    cg_ref[...] = seg("cg")


CACHE_OUTS = ("ak", "av", "bk", "bv", "ik")


def _in_projection(x2d, gain, w_pack, lb_row, tabs, tm, stacked=None):
    n = x2d.shape[0]
    ca, sa, cb, sb = tabs
    period = ca.shape[0] // tm
    row = lambda w: pl.BlockSpec((tm, w), lambda i: (i, 0))
    tab = lambda w: pl.BlockSpec((tm, w), lambda i: (i % period, 0))
    outs = (("aq", WA, BF16), ("ak", WA, F32), ("av", WA, F32), ("ak16", WA, BF16), ("av16", WA, BF16),
            ("bq", WBP, BF16), ("bk", WB, F32), ("bv", WB, F32), ("bk16", WBP, BF16), ("bv16", WBP, BF16),
            ("iq", WIQ, BF16), ("ik2", 2 * D_I, BF16), ("ik", D_I, F32), ("iw", LANES, F32),
            ("cq", WBP, F32), ("ck", WBP, F32), ("cgl", WBP, F32), ("cv", WBP, F32), ("cg", WBP, F32))
    out_specs = {name: row(w) for name, w, _ in outs}
    out_shape = {name: jax.ShapeDtypeStruct((n, w), dt) for name, w, dt in outs}
    inputs = [x2d, gain, w_pack, lb_row, ca, sa, cb, sb]
    in_specs = [row(x2d.shape[1]), _const_spec(gain.shape), _const_spec(w_pack.shape),
                _const_spec(lb_row.shape), tab(WA), tab(WA), tab(WBP), tab(WBP)]
    aliases = {}
    if stacked is not None:
        layer, depth, batch, caches = stacked
        t = n // batch
        tiles = t // tm
        for name, w, _ in outs:
            if name in CACHE_OUTS:
                out_specs[name] = pl.BlockSpec((None, None, w, tm),
                                               lambda i, layer=layer: (layer, i // tiles, 0, i % tiles))
                out_shape[name] = jax.ShapeDtypeStruct((depth, batch, w, t), F32)
        if caches is not None:
            names = [o[0] for o in outs]
            for name in CACHE_OUTS:
                aliases[len(inputs)] = names.index(name)
                inputs.append(caches[name])
                in_specs.append(pl.BlockSpec(memory_space=pl.ANY))
    kern = functools.partial(_inproj_kernel, n_aliased=len(aliases), feature_major_caches=stacked is not None)
    res = pl.pallas_call(
        kern,
        grid=(n // tm,),
        in_specs=in_specs,
        out_specs=[out_specs[o[0]] for o in outs],
        out_shape=[out_shape[o[0]] for o in outs],
        input_output_aliases=aliases,
        compiler_params=_cparams(1),
        name="in_projection",
    )(*inputs)
    return dict(zip([o[0] for o in outs], res))


NARROW_KEY_BLOCK = 256


def _attn_a_kernel(lam_ref, q_ref, k_ref, v_ref, o_ref, q4_ref, m_ref, acc_ref, *, tq, tk, q_off, kv_len):
    qi = pl.program_id(2)
    q = q_ref[...]
    lane = lax.broadcasted_iota(jnp.int32, q.shape, 1)
    for i in range(4):
        q4_ref[i * tq:(i + 1) * tq, :] = jnp.where(lane // DA == i, q, jnp.zeros_like(q))
    m_ref[...] = jnp.full(m_ref.shape, NEG_BIG, F32)
    acc_ref[...] = jnp.zeros(acc_ref.shape, F32)

    q_first = q_off + qi * tq
    q_last = q_first + tq - 1
    n_full = jnp.minimum((q_first // CHUNK + 1) * CHUNK, kv_len) // tk
    lim = jnp.minimum((q_last // CHUNK + 1) * CHUNK, kv_len)
    tkn = min(tk, NARROW_KEY_BLOCK)
    n_narrow = (lim + tkn - 1) // tkn
    n_wide = n_full + (lim - n_full * tk) // tk if tkn < tk else n_narrow

    def step(j, tk, masked):
        start = pl.multiple_of(j * tk, tk)
        kb = k_ref[pl.ds(start, tk), :]
        vb = v_ref[pl.ds(start, tk), :]
        s = _dot_nt(q4_ref[...], kb)
        if masked:
            kpos = start + lax.broadcasted_iota(jnp.int32, (tq, tk), 1)
            qpos = q_first + lax.broadcasted_iota(jnp.int32, (tq, tk), 0)
            valid = (kpos // CHUNK <= qpos // CHUNK) & (kpos < kv_len)
            s = jnp.where(valid[None], s.reshape(4, tq, tk), NEG_BIG).reshape(4 * tq, tk)
        m_old = m_ref[...]
        m_new = jnp.maximum(m_old, jnp.max(s, axis=-1, keepdims=True))
        alpha = jnp.exp2(m_old - m_new)
        p = jnp.exp2(s - jnp.concatenate([m_new] * (tk // LANES), axis=1)).astype(BF16)
        lane_v = lax.broadcasted_iota(jnp.int32, vb.shape, 1)
        ones = jnp.ones_like(vb)
        pv = jnp.concatenate([_dot(p[0:2 * tq], jnp.where(lane_v < DV_A, vb, ones)),
                              _dot(p[2 * tq:4 * tq], jnp.where(lane_v < DV_A, ones, vb))], axis=0)
        acc_ref[...] = alpha * acc_ref[...] + pv
        m_ref[...] = m_new

    def body(tk, masked):
        def run(j, c):
            step(j, tk, masked)
            return c
        return run

    lax.fori_loop(0, n_full, body(tk, False), 0)
    lax.fori_loop(n_full, n_wide, body(tk, True), 0)
    if tkn < tk:
        lax.fori_loop(n_wide * (tk // tkn), n_narrow, body(tkn, True), 0)

    lam = lam_ref[0]
    acc0 = acc_ref[0:2 * tq, :]
    acc1 = acc_ref[2 * tq:4 * tq, :]
    on0 = acc0 / acc0[:, DV_A:DV_A + 1]
    on1 = acc1 / acc1[:, 0:1]
    o0 = on0[0:tq] - lam * on0[tq:2 * tq]
    o1 = on1[0:tq] - lam * on1[tq:2 * tq]
    o_ref[...] = jnp.where(lane < DV_A, o0, o1)


def _attention_a(lam, q, k, v, *, q_off, kv_len, tq, tk):
    b, t, _ = q.shape
    lk = k.shape[1]
    kern = functools.partial(_attn_a_kernel, tq=tq, tk=tk, q_off=q_off, kv_len=kv_len)
    return pl.pallas_call(
        kern,
        grid=(b, WA // LANES, t // tq),
        in_specs=[pl.BlockSpec(memory_space=pltpu.SMEM),
                  pl.BlockSpec((None, tq, LANES), lambda bi, hi, qi: (bi, qi, hi)),
                  pl.BlockSpec((None, lk, LANES), lambda bi, hi, qi: (bi, 0, hi)),
                  pl.BlockSpec((None, lk, LANES), lambda bi, hi, qi: (bi, 0, hi))],
        out_specs=pl.BlockSpec((None, tq, LANES), lambda bi, hi, qi: (bi, qi, hi)),
        out_shape=jax.ShapeDtypeStruct((b, t, WA), F32),
        scratch_shapes=[pltpu.VMEM((4 * tq, LANES), BF16), pltpu.VMEM((4 * tq, LANES), F32),
                        pltpu.VMEM((4 * tq, LANES), F32)],
        compiler_params=_cparams(3),
        name="mixer_a",
    )(lam, q, k, v)


TIE_BLOCK = 256
SEARCH_BITS_PER_TRIP = 3


def _attn_b_kernel(q_ref, iq_ref, iw_ref, k_ref, v_ref, ik_ref, o_ref, key_ref, bias_ref,
                   *, tq, q_off, kv_len, n_sel, variants):
    q_first = q_off + pl.program_id(1) * tq
    need_keys = jnp.minimum(((q_first + tq - 1) // CHUNK + 1) * CHUNK, kv_len)
    lo = 0
    for lk in variants:
        @pl.when((need_keys > lo) & (need_keys <= lk))
        def _(lk=lk):
            _attn_b_body(q_ref, iq_ref, iw_ref, k_ref, v_ref, ik_ref, o_ref, key_ref, bias_ref,
                         tq=tq, q_first=q_first, need_keys=need_keys, kv_len=kv_len, n_sel=n_sel, lk=lk)
        lo = lk


def _stack_heads(x, n_heads, width):
    lane = lax.broadcasted_iota(jnp.int32, x.shape, 1)
    return jnp.concatenate([jnp.where(lane // width == h, x, jnp.zeros_like(x)) for h in range(n_heads)],
                           axis=0)


def _attn_b_body(q_ref, iq_ref, iw_ref, k_ref, v_ref, ik_ref, o_ref, key_ref, bias_ref,
                 *, tq, q_first, need_keys, kv_len, n_sel, lk):
    ik2 = ik_ref[0:lk, :]
    iq = iq_ref[...]
    iw = iw_ref[...]

    score = jnp.zeros((tq, lk), F32)
    for pi in range(H_I // 2):
        y = _stack_heads(iq[:, LANES * pi:LANES * (pi + 1)], 2, D_I)
        d = jnp.maximum(_dot_nt(y, ik2), 0.0)
        score = score + iw[:, 2 * pi:2 * pi + 1] * d[0:tq] + iw[:, 2 * pi + 1:2 * pi + 2] * d[tq:2 * tq]

    kpos = lax.broadcasted_iota(jnp.int32, (tq, lk), 1)
    qpos = q_first + lax.broadcasted_iota(jnp.int32, (tq, lk), 0)
    valid = (kpos // CHUNK <= qpos // CHUNK) & (kpos < kv_len)
    score = jnp.where(valid, score, NEG_BIG)

    bits = lax.bitcast_convert_type(score, jnp.int32)
    key = bits ^ ((bits >> 31) & jnp.int32(0x7FFFFFFF))
    key = jnp.where(key == -1, 0, key)
    key_ref[:, 0:lk] = key

    kf = float(n_sel)

    def count_ge(cand):
        wide = jnp.concatenate([cand] * (lk // LANES), axis=1)
        return jnp.sum(jnp.where(key_ref[:, 0:lk] >= wide, 1.0, 0.0), axis=-1, keepdims=True)

    int_min = jnp.int32(-2 ** 31)
    zero = jnp.zeros((tq, LANES), jnp.int32)
    select_all = need_keys <= n_sel
    cur = jnp.where(select_all | (count_ge(zero) < kf), int_min, zero)

    def try_bit(cur, bit):
        cand = cur | bit
        return jnp.where(count_ge(cand) >= kf, cand, cur)

    cur = jnp.where(select_all, int_min, try_bit(cur, jnp.int32(1 << 30)))

    def bits_body(i, cur):
        for j in range(SEARCH_BITS_PER_TRIP):
            cur = try_bit(cur, jnp.left_shift(jnp.int32(1), 29 - (SEARCH_BITS_PER_TRIP * i + j)))
        return cur

    thr = lax.fori_loop(0, jnp.where(select_all, 0, 30 // SEARCH_BITS_PER_TRIP), bits_body, cur)

    key = key_ref[:, 0:lk]
    thr = jnp.concatenate([thr] * (lk // LANES), axis=1)
    gt = key > thr
    eq = key == thr
    need = kf - jnp.sum(jnp.where(gt, 1.0, 0.0), axis=-1, keepdims=True)
    r_i = lax.broadcasted_iota(jnp.int32, (TIE_BLOCK, TIE_BLOCK), 0)
    c_i = lax.broadcasted_iota(jnp.int32, (TIE_BLOCK, TIE_BLOCK), 1)
    tri = jnp.where(r_i <= c_i, 1.0, 0.0).astype(BF16)
    carry = jnp.zeros((tq, 1), F32)
    for jb in range(lk // TIE_BLOCK):
        sl = slice(jb * TIE_BLOCK, (jb + 1) * TIE_BLOCK)
        eq_b = eq[:, sl]
        pref = _dot(jnp.where(eq_b, 1.0, 0.0).astype(BF16), tri) + carry
        carry = pref[:, TIE_BLOCK - 1:TIE_BLOCK]
        sel = (gt[:, sl] | (eq_b & (pref <= need))) & valid[:, sl]
        bias_ref[:, sl] = jnp.where(sel, 0.0, NEG_BIG)

    q = q_ref[...]
    lane = lax.broadcasted_iota(jnp.int32, (tq, LANES), 1)
    bias = bias_ref[:, 0:lk]
    for p in range(NPAIR):
        n_heads = min(2, H_B - 2 * p)
        sl = slice(p * LANES, (p + 1) * LANES)
        y = _stack_heads(q[:, sl], n_heads, D_B)
        s = _dot_nt(y, k_ref[0:lk, sl]).reshape(n_heads, tq, lk) + bias[None]
        m = jnp.max(s, axis=-1, keepdims=True)
        pr = jnp.exp2(s - m)
        l = jnp.sum(pr, axis=-1, keepdims=True)
        o = _dot(pr.reshape(n_heads * tq, lk).astype(BF16), v_ref[0:lk, sl]) / l.reshape(n_heads * tq, 1)
        o_ref[:, sl] = o if n_heads == 1 else jnp.where(lane < D_B, o[0:tq], o[tq:2 * tq])


def _attention_b(q, iq, iw, k, v, ik2, *, q_off, kv_len, tq, n_sel):
    b, t, _ = q.shape
    lk = k.shape[1]
    granule = tq if (tq % TIE_BLOCK == 0 and lk % (2 * tq) == 0) else lk
    variants = tuple(v for v in range(granule, lk + 1, granule) if v <= lk // 2 or v % (2 * granule) == 0 or v == lk)
    kern = functools.partial(_attn_b_kernel, tq=tq, q_off=q_off, kv_len=kv_len, n_sel=n_sel,
                             variants=variants)
    qspec = lambda w: pl.BlockSpec((None, tq, w), lambda bi, qi: (bi, qi, 0))
    kspec = lambda w: pl.BlockSpec((None, lk, w), lambda bi, qi: (bi, 0, 0))
    return pl.pallas_call(
        kern,
        grid=(b, t // tq),
        in_specs=[qspec(WBP), qspec(WIQ), qspec(LANES), kspec(WBP), kspec(WBP), kspec(2 * D_I)],
        out_specs=qspec(WBP),
        out_shape=jax.ShapeDtypeStruct((b, t, WBP), F32),
        scratch_shapes=[pltpu.VMEM((tq, lk), jnp.int32), pltpu.VMEM((tq, lk), F32)],
        compiler_params=_cparams(2),
        name="mixer_b",
    )(q, iq, iw, k, v, ik2)


HGRN2_CHUNKS_PER_STEP = 2
GROUP = SUBLANES
DECAY_SPLIT_LIMIT = 60.0


def _hgrn2_kernel(q_ref, k_ref, g_ref, v_ref, s0_ref, o_ref, s_out_ref, st_ref, *, c, chunks):
    ci = pl.program_id(1)

    @pl.when(ci == 0)
    def _():
        st_ref[...] = s0_ref[...]

    for i in range(chunks):
        rows = pl.ds(i * c, c)
        _hgrn2_chunk(q_ref.at[rows], k_ref.at[rows], g_ref.at[rows], v_ref.at[rows], o_ref.at[rows], st_ref, c=c)

    @pl.when(ci == pl.num_programs(1) - 1)
    def _():
        s_out_ref[...] = st_ref[...]


def _hgrn2_chunk(q_ref, k_ref, g_ref, v_ref, o_ref, st_ref, *, c):
    q = q_ref[...]
    k = k_ref[...]
    g = g_ref[...]
    v = v_ref[...]
    w = q.shape[-1]
    row = lax.broadcasted_iota(jnp.int32, (c, w), 0)
    lane = lax.broadcasted_iota(jnp.int32, (c, w), 1)

    def segment_scans():
        cs, tots = {1: g}, {1: g}
        cum, tot, m = g, g, 1
        while m < c:
            upper = (row // m) % 2 == 1
            prev_tot = pltpu.roll(tot, m, 0)
            next_tot = pltpu.roll(tot, c - m, 0)
            cum = cum + jnp.where(upper, prev_tot, 0.0)
            tot = tot + jnp.where(upper, prev_tot, next_tot)
            m *= 2
            cs[m], tots[m] = cum, tot
        return cs, tots

    def cumsum_rows():
        cum, m = g, 1
        while m < c:
            cum = cum + jnp.where(row >= m, pltpu.roll(cum, m, 0), 0.0)
            m *= 2
        return cum

    head_masks = [lane // DK_C == h for h in range(w // DK_C)]
    n_heads = H_C
    rq = lax.broadcasted_iota(jnp.int32, (c, c), 0)
    rk = lax.broadcasted_iota(jnp.int32, (c, c), 1)

    def attend(att):
        res = _dot(att.astype(BF16), v.astype(BF16))
        out = jnp.zeros((c, w), F32)
        for h in range(n_heads):
            out = out + jnp.where(head_masks[h], res[h * c:(h + 1) * c], 0.0)
        return out

    def stack_heads(x):
        return jnp.concatenate([jnp.where(head_masks[h], x, 0.0) for h in range(n_heads)], axis=0)

    def finish(o_intra, bcum, blast):
        qe = (q * jnp.exp(bcum)).astype(BF16)
        k2 = (k * jnp.exp(blast - bcum)).astype(BF16)
        decay = jnp.exp(blast[0:1, :])
        pr = lax.broadcasted_iota(jnp.int32, (LANES, LANES), 0)
        pc = lax.broadcasted_iota(jnp.int32, (LANES, LANES), 1)
        diag = pr // DK_C == pc // DK_C
        o_state = []
        for p in range(w // LANES):
            sl = slice(p * LANES, (p + 1) * LANES)
            st = st_ref[p]
            o_state.append(_dot_nt(qe[:, sl], st.astype(BF16)))
            upd = _dot(v[:, sl].T.astype(BF16), k2[:, sl])
            st_ref[p] = st * decay[:, sl] + jnp.where(diag, upd, 0.0)
        o_ref[...] = o_intra + jnp.concatenate(o_state, axis=-1)

    ag = jnp.abs(g)
    half_bound = jnp.maximum(jnp.max(jnp.sum(ag[0:c // 2], axis=0, keepdims=True)),
                             jnp.max(jnp.sum(ag[c // 2:c], axis=0, keepdims=True)))
    single_split = half_bound <= DECAY_SPLIT_LIMIT

    @pl.when(single_split)
    def _():
        bcum = cumsum_rows()
        dmid = bcum - bcum[c // 2 - 1:c // 2, :]
        qt = stack_heads(q * jnp.exp(dmid)).astype(BF16)
        kt = (k * jnp.exp(-dmid)).astype(BF16)
        causal = jnp.concatenate([rq >= rk] * n_heads, axis=0)
        o_intra = attend(jnp.where(causal, _dot_nt(qt, kt), 0.0))
        finish(o_intra, bcum, jnp.broadcast_to(bcum[c - 1:c, :], (c, w)))

    @pl.when(jnp.logical_not(single_split))
    def _():
        cs, tots = segment_scans()
        att = jnp.zeros((n_heads * c, c), F32)
        half = GROUP
        while half < c:
            upper = (row // half) % 2 == 1
            qt = jnp.where(upper, q * jnp.exp(cs[half]), 0.0)
            kt = jnp.where(upper, 0.0, k * jnp.exp(tots[half] - cs[half])).astype(BF16)
            blk = _dot_nt(stack_heads(qt).astype(BF16), kt)
            same = (rq // (2 * half)) == (rk // (2 * half))
            same = jnp.concatenate([same] * n_heads, axis=0)
            att = att + jnp.where(same, blk, 0.0)
            half *= 2
        out = attend(att)

        def group_row(x, j):
            x3 = x.reshape(c // GROUP, GROUP, w)
            return jnp.broadcast_to(x3[:, j:j + 1, :], x3.shape).reshape(c, w)

        c8 = cs[GROUP]
        vals = []
        for j in range(GROUP):
            ok = (row % GROUP) >= j
            e = jnp.where(ok, c8 - group_row(c8, j), 0.0)
            vals.append(jnp.where(ok, q * group_row(k, j) * jnp.exp(e), 0.0))
        r_i = lax.broadcasted_iota(jnp.int32, (w, w), 0)
        c_i = lax.broadcasted_iota(jnp.int32, (w, w), 1)
        head_sum = jnp.where(r_i // DK_C == c_i // DK_C, 1.0, 0.0).astype(BF16)
        wts = _dot(jnp.concatenate(vals, axis=0).astype(BF16), head_sum)
        for j in range(GROUP):
            out = out + wts[j * c:(j + 1) * c] * group_row(v, j)
        finish(out, cs[c], tots[c])


def _hgrn2(q, k, g, v, s0, *, c):
    b, t, w = q.shape
    chunks = HGRN2_CHUNKS_PER_STEP if t % (HGRN2_CHUNKS_PER_STEP * c) == 0 else 1
    kern = functools.partial(_hgrn2_kernel, c=c, chunks=chunks)
    blk = pl.BlockSpec((None, chunks * c, w), lambda bi, ci: (bi, ci, 0))
    sblk = pl.BlockSpec((None, w // LANES, LANES, LANES), lambda bi, ci: (bi, 0, 0, 0))
    return pl.pallas_call(
        kern,
        grid=(b, t // (chunks * c)),
        in_specs=[blk, blk, blk, blk, sblk],
        out_specs=[blk, sblk],
        out_shape=[jax.ShapeDtypeStruct((b, t, w), F32),
                   jax.ShapeDtypeStruct((b, w // LANES, LANES, LANES), F32)],
        scratch_shapes=[pltpu.VMEM((w // LANES, LANES, LANES), F32)],
        compiler_params=_cparams(2),
        name="mixer_c",
    )(q, k, g, v, s0)


def _state_to_pairs(s):
    b = s.shape[0]
    st = jnp.swapaxes(s.astype(F32), -1, -2)
    st = jnp.pad(st, ((0, 0), (0, 2 * NPAIR - H_C), (0, 0), (0, 0)))
    st = st.reshape(b, NPAIR, 2, DV_C, DK_C)
    eye = jnp.eye(2, dtype=F32)
    full = st[:, :, :, :, None, :] * eye[None, None, :, None, :, None]
    return full.reshape(b, NPAIR, 2 * DV_C, 2 * DK_C)


def _pairs_to_state(sp):
    b = sp.shape[0]
    s6 = sp.reshape(b, NPAIR, 2, DV_C, 2, DK_C)
    diag = jnp.stack([s6[:, :, a, :, a, :] for a in range(2)], axis=2)
    return jnp.swapaxes(diag.reshape(b, 2 * NPAIR, DV_C, DK_C)[:, :H_C], -1, -2)


FF_BLOCK = 2816


def _head_norm(y, gain):
    w = y.shape[-1]
    r_i = lax.broadcasted_iota(jnp.int32, (w, w), 0)
    c_i = lax.broadcasted_iota(jnp.int32, (w, w), 1)
    head_sum = jnp.where(r_i // DV_A == c_i // DV_A, 1.0, 0.0).astype(BF16)
    y2 = y * y
    hi = y2.astype(BF16)
    lo = (y2 - hi.astype(F32)).astype(BF16)
    ms = (_dot(hi, head_sum) + _dot(lo, head_sum)) * (1.0 / DV_A)
    return y * lax.rsqrt(ms + EPS) * gain


def _merge_kernel(scal_ref, x_ref, oa_ref, ob_ref, oc_ref, cg_ref, hist_ref, ag_ref, cgn_ref, wo_ref,
                  n2_ref, wup_ref, cw_ref, cb_ref, wdn_ref, fn_ref, *out_and_scratch, d_ff, final):
    if final:
        x_out_ref, fc_ref, y_ref, carry_ref = out_and_scratch
    else:
        x_out_ref, fc_ref, carry_ref = out_and_scratch
    ti = pl.program_id(1)
    tm = x_ref.shape[0]

    @pl.when(ti == 0)
    def _():
        carry_ref[...] = hist_ref[...]

    oa = _head_norm(oa_ref[...], ag_ref[...]) * scal_ref[0]
    cg = cg_ref[...]
    oc = _head_norm(oc_ref[...], cgn_ref[...]) * (cg * jax.nn.sigmoid(cg))
    mixed = (_dot(oa.astype(BF16), wo_ref[0:WA, :])
             + _dot(ob_ref[...].astype(BF16), wo_ref[WA:WA + WBP, :])
             + _dot(oc.astype(BF16), wo_ref[WA + WBP:WA + 2 * WBP, :]))
    x = x_ref[...] + mixed

    ms = jnp.mean(x * x, axis=-1, keepdims=True)
    h = (x * lax.rsqrt(ms + EPS) * n2_ref[...]).astype(BF16)
    row = lax.broadcasted_iota(jnp.int32, (tm, FF_BLOCK), 0)
    acc = jnp.zeros(x.shape, F32)
    for cblk in range(d_ff // FF_BLOCK):
        sl = slice(cblk * FF_BLOCK, (cblk + 1) * FF_BLOCK)
        a = _dot(h, wup_ref[:, sl])
        gate = _dot(h, wup_ref[:, d_ff + cblk * FF_BLOCK:d_ff + (cblk + 1) * FF_BLOCK])
        prev2 = carry_ref[0:1, sl]
        prev1 = carry_ref[1:2, sl]
        a1 = jnp.where(row == 0, prev1, pltpu.roll(a, 1, 0))
        a2 = jnp.where(row == 0, prev2, jnp.where(row == 1, prev1, pltpu.roll(a, 2, 0)))
        conv = cb_ref[:, sl] + a2 * cw_ref[0:1, sl] + a1 * cw_ref[1:2, sl] + a * cw_ref[2:3, sl]
        act = conv * jax.nn.sigmoid(conv) * gate
        acc = acc + _dot(act.astype(BF16), wdn_ref[sl, :])
        carry_ref[:, sl] = a[tm - (CONV_W - 1):, :]
    x = x + acc
    x_out_ref[...] = x
    fc_ref[...] = carry_ref[...]
    if final:
        ms = jnp.mean(x * x, axis=-1, keepdims=True)
        y_ref[...] = x * lax.rsqrt(ms + EPS) * fn_ref[...]


def _merge_ffn(scal, x, oa, ob, oc, cg, hist, a_gain, c_gain, wo, n2, wup, cw, cb, wdn, fnorm, *, tm, final):
    b, t, d = x.shape
    d_ff = wdn.shape[0]
    kern = functools.partial(_merge_kernel, d_ff=d_ff, final=final)
    blk = lambda w: pl.BlockSpec((None, tm, w), lambda bi, ti: (bi, ti, 0))
    per_b = pl.BlockSpec((None, CONV_W - 1, d_ff), lambda bi, ti: (bi, 0, 0))
    out_specs = [blk(d), per_b]
    out_shape = [jax.ShapeDtypeStruct((b, t, d), F32), jax.ShapeDtypeStruct((b, CONV_W - 1, d_ff), F32)]
    if final:
        out_specs.append(blk(d))
        out_shape.append(jax.ShapeDtypeStruct((b, t, d), F32))
    return pl.pallas_call(
        kern,
        grid=(b, t // tm),
        in_specs=[pl.BlockSpec(memory_space=pltpu.SMEM), blk(d), blk(WA), blk(WBP), blk(WBP), blk(WBP), per_b,
                  _const_spec(a_gain.shape), _const_spec(c_gain.shape), _const_spec(wo.shape),
                  _const_spec(n2.shape), _const_spec(wup.shape), _const_spec(cw.shape),
                  _const_spec(cb.shape), _const_spec(wdn.shape), _const_spec(fnorm.shape)],
        out_specs=out_specs,
        out_shape=out_shape,
        scratch_shapes=[pltpu.VMEM((CONV_W - 1, d_ff), F32)],
        compiler_params=_cparams(2),
        name="merge_ffn",
    )(scal, x, oa, ob, oc, cg, hist, a_gain, c_gain, wo, n2, wup, cw, cb, wdn, fnorm)


def _pack_w_in(w):
    parts = jnp.split(w, np.cumsum(IN_SIZES)[:-1].tolist(), axis=-1)
    cols = []
    for (name, width, padded), part in zip(SEGS, parts):
        if name == "ik":
            part = jnp.concatenate([part, part], axis=-1)
            width = 2 * D_I
        cols.append(jnp.pad(part, ((0, 0), (0, padded - width))))
    return jnp.concatenate(cols, axis=-1).astype(BF16)


def _pack_w_out(w):
    wa, wb, wc = w[:WA], w[WA:WA + WB], w[WA + WB:]
    pad = lambda m: jnp.pad(m, ((0, WBP - WB), (0, 0)))
    return jnp.concatenate([wa, pad(wb), pad(wc)], axis=0).astype(BF16)


def _rope_table(pos, dim, width):
    inv_freq = ROPE_THETA ** (-jnp.arange(0, dim, 2, dtype=F32) / dim)
    ang = pos.astype(F32)[:, None] * inv_freq[None, :]
    reps = width // (dim // 2)
    return jnp.tile(jnp.cos(ang), (1, reps)), jnp.tile(jnp.sin(ang), (1, reps))


def _cache_prep_kernel(c_ref, o_ref, *, n_feat, past):
    x = c_ref[...]
    if x.shape[0] < LANES:
        x = jnp.concatenate([x] * (LANES // x.shape[0]), axis=0)
        n_valid = LANES
    else:
        n_valid = n_feat - pl.program_id(1) * LANES
    xt = x.T
    lane = lax.broadcasted_iota(jnp.int32, xt.shape, 1)
    o_ref[0:past, :] = jnp.where(lane < n_valid, xt, 0.0).astype(BF16)
    o_ref[past:, :] = jnp.zeros((o_ref.shape[0] - past, LANES), BF16)


def _cache_prep(cache, rows):
    depth, batch, past = cache.shape[:3]
    n_feat = int(np.prod(cache.shape[3:]))
    nd = cache.ndim
    c = jnp.transpose(cache, (0, 1) + tuple(range(3, nd)) + (2,)).reshape(depth * batch, n_feat, past)
    fblk = min(LANES, n_feat)
    n_blocks = -(-n_feat // LANES)
    out = pl.pallas_call(
        functools.partial(_cache_prep_kernel, n_feat=n_feat, past=past),
        grid=(depth * batch, n_blocks),
        in_specs=[pl.BlockSpec((None, fblk, past), lambda i, j: (i, j, 0))],
        out_specs=pl.BlockSpec((None, rows, LANES), lambda i, j: (i, 0, j)),
        out_shape=jax.ShapeDtypeStruct((depth * batch, rows, n_blocks * LANES), BF16),
        compiler_params=_cparams(2),
        name="cache_prep",
    )(c)
    return out.reshape(depth, batch, rows, n_blocks * LANES)


def _pad_rows(x, rows):
    return jnp.pad(x, ((0, 0), (0, rows - x.shape[1]), (0, 0)))


def _round_up(n, m):
    return (n + m - 1) // m * m


def kernel(x_prompt, x_sample, cache_a_k, cache_a_v, cache_b_k, cache_b_v, cache_b_kidx, state_c, state_ffn_conv, norm1, w_in, lam_q1, lam_k1, lam_q2, lam_k2, a_norm, c_lower, c_norm, w_out, norm2, ffn_up, ffn_conv_w, ffn_conv_b, ffn_down, final_norm):
    depth = w_in.shape[0]
    b_p, t_p, d = x_prompt.shape
    b_s, t_s, _ = x_sample.shape
    past = cache_a_k.shape[2]
    d_ff = ffn_down.shape[1]
    kv_s = past + t_s
    n_sel_p = min(TOPK_MAX, t_p // 4)
    n_sel_s = min(TOPK_MAX, kv_s // 4)
    tk = 256
    lk_s = _round_up(kv_s, tk)

    lb_soft = jax.nn.softmax(c_lower.astype(F32), axis=0)
    lower = jnp.cumsum(lb_soft, axis=0) - lb_soft[0]
    lower = jnp.pad(lower, ((0, 0), (0, WBP - WB)))

    tm_p = min(512, t_p)
    tm_s = min(256, b_s * t_s)
    pos_p = jnp.arange(t_p)
    pos_s = jnp.tile(past + jnp.arange(t_s), tm_s // t_s)
    tabs_p = _rope_table(pos_p, DA, WA) + _rope_table(pos_p, D_B, WBP)
    tabs_s = _rope_table(pos_s, DA, WA) + _rope_table(pos_s, D_B, WBP)

    past_kv = {name: _cache_prep(c, lk_s) for name, c in
               (("ak", cache_a_k), ("av", cache_a_v), ("bk", cache_b_k), ("bv", cache_b_v), ("ik", cache_b_kidx))}

    xp, xs = x_prompt, x_sample
    caches_p, state_p, conv_p = None, [], []
    outs_s = [[] for _ in range(7)]
    y_p = y_s = None
    fnorm = final_norm.reshape(1, d)
    for l in range(depth):
        lam_init = 0.8 - 0.6 * math.exp(-0.3 * l)
        lam = (jnp.exp(jnp.sum(lam_q1[l].astype(F32) * lam_k1[l].astype(F32)))
               - jnp.exp(jnp.sum(lam_q2[l].astype(F32) * lam_k2[l].astype(F32))) + lam_init)
        lam_arr = lam.reshape(1).astype(F32)
        scal = jnp.full((1,), 1.0 - lam_init, F32)
        w_pack = _pack_w_in(w_in[l])
        wo = _pack_w_out(w_out[l])
        gain1 = norm1[l].reshape(1, d)
        gain2 = norm2[l].reshape(1, d)
        lb_row = lower[l].reshape(1, WBP)
        a_gain = jnp.tile(a_norm[l], H_A).reshape(1, WA)
        c_gain = jnp.pad(jnp.tile(c_norm[l], H_C), (0, WBP - WB)).reshape(1, WBP)
        wup = ffn_up[l].astype(BF16)
        wdn = ffn_down[l].astype(BF16)
        cw = ffn_conv_w[l]
        cb = ffn_conv_b[l].reshape(1, d_ff)
        final = l == depth - 1

        u = _in_projection(xp.reshape(b_p * t_p, d), gain1, w_pack, lb_row, tabs_p, tm_p,
                           stacked=(l, depth, b_p, caches_p))
        caches_p = {name: u[name] for name in CACHE_OUTS}
        r3 = lambda a, b=b_p, t=t_p: a.reshape(b, t, a.shape[-1])
        oa = _attention_a(lam_arr, r3(u["aq"]), r3(u["ak16"]), r3(u["av16"]),
                          q_off=0, kv_len=t_p, tq=min(256, t_p), tk=min(512, t_p))
        ob = _attention_b(r3(u["bq"]), r3(u["iq"]), r3(u["iw"]), r3(u["bk16"]), r3(u["bv16"]), r3(u["ik2"]),
                          q_off=0, kv_len=t_p, tq=min(256, t_p), n_sel=n_sel_p)
        s0 = jnp.zeros((b_p, NPAIR, LANES, LANES), F32)
        oc, s_new = _hgrn2(r3(u["cq"]), r3(u["ck"]), r3(u["cgl"]), r3(u["cv"]), s0, c=CHUNK)
        hist0 = jnp.zeros((b_p, CONV_W - 1, d_ff), F32)
        res = _merge_ffn(scal, xp, oa, ob, oc, r3(u["cg"]), hist0, a_gain, c_gain, wo, gain2, wup, cw, cb,
                         wdn, fnorm, tm=min(512, t_p), final=final)
        xp, fc = res[0], res[1]
        if final:
            y_p = res[2]
        state_p.append(_pairs_to_state(s_new))
        conv_p.append(fc)

        u = _in_projection(xs.reshape(b_s * t_s, d), gain1, w_pack, lb_row, tabs_s, tm_s)
        r3 = lambda a, b=b_s, t=t_s: a.reshape(b, t, a.shape[-1])

        def cat(name, new):
            return lax.dynamic_update_slice(past_kv[name][l], r3(new), (0, past, 0))

        oa = _attention_a(lam_arr, r3(u["aq"]), cat("ak", u["ak16"]), cat("av", u["av16"]),
                          q_off=past, kv_len=kv_s, tq=t_s, tk=tk)
        ob = _attention_b(r3(u["bq"]), r3(u["iq"]), r3(u["iw"]), cat("bk", u["bk16"]),
                          cat("bv", u["bv16"]), cat("ik", u["ik2"]),
                          q_off=past, kv_len=kv_s, tq=t_s, n_sel=n_sel_s)
        oc, s_new = _hgrn2(r3(u["cq"]), r3(u["ck"]), r3(u["cgl"]), r3(u["cv"]), _state_to_pairs(state_c[l]),
                           c=t_s)
        res = _merge_ffn(scal, xs, oa, ob, oc, r3(u["cg"]), state_ffn_conv[l].astype(F32), a_gain, c_gain, wo,
                         gain2, wup, cw, cb, wdn, fnorm, tm=t_s, final=final)
        xs, fc = res[0], res[1]
        if final:
            y_s = res[2]
        for lst, val in zip(outs_s, (u["ak"].reshape(b_s, t_s, H_A, 2 * DA), u["av"].reshape(b_s, t_s, H_A, DV_A),
                                     u["bk"].reshape(b_s, t_s, H_B, D_B), u["bv"].reshape(b_s, t_s, H_B, D_B),
                                     u["ik"].reshape(b_s, t_s, D_I), _pairs_to_state(s_new), fc)):
            lst.append(val)

    def frames_major(c, heads):
        c = c.reshape(depth, b_p, heads, c.shape[2] // heads, t_p)
        return jnp.transpose(c, (0, 1, 4, 2, 3))

    outs_p = (frames_major(caches_p["ak"], H_A), frames_major(caches_p["av"], H_A),
              frames_major(caches_p["bk"], H_B), frames_major(caches_p["bv"], H_B),
              jnp.swapaxes(caches_p["ik"], 2, 3), jnp.stack(state_p), jnp.stack(conv_p))
    return (y_p, y_s) + outs_p + tuple(jnp.stack(v) for v in outs_s)
```

```python
import functools
import math

import jax
import jax.numpy as jnp
import numpy as np
from jax import lax
from jax.experimental import pallas as pl
from jax.experimental.pallas import tpu as pltpu

F32 = jnp.float32
BF16 = jnp.bfloat16

CHUNK = 64
ROPE_THETA = 10000.0
EPS = 1e-6
NEG_BIG = -1e30
LB_FLOOR = 1e-20
H_A, DA, DV_A = 6, 32, 64
H_B, D_B = 5, 64
H_I, D_I = 4, 64
TOPK_MAX = 256
H_C, DK_C, DV_C = 5, 64, 64
CONV_W = 3

LANES = 128
SUBLANES = 8
VMEM_LIMIT = 56 * 1024 * 1024

WA = H_A * 2 * DA
WB = H_B * D_B
WBP = 384
WIQ = H_I * D_I
NPAIR = WBP // LANES

SEGS = (("aq", WA, WA), ("ak", WA, WA), ("av", WA, WA),
        ("bq", WB, WBP), ("bk", WB, WBP), ("bv", WB, WBP),
        ("iq", WIQ, WIQ), ("ik", D_I, 2 * D_I), ("iw", H_I, LANES),
        ("cq", WB, WBP), ("cf", WB, WBP), ("ci", WB, WBP), ("cg", WB, WBP))
SEG_OFF = {}
_o = 0
for _n, _w, _p in SEGS:
    SEG_OFF[_n] = (_o, _p)
    _o += _p
W_PACK = _o
PROJ_GROUPS = tuple((SEG_OFF[first][0], SEG_OFF[last][0] + SEG_OFF[last][1] - SEG_OFF[first][0])
                    for first, last in (("aq", "ak"), ("av", "bq"), ("bk", "bv"), ("iq", "iw"),
                                        ("cq", "cf"), ("ci", "cg")))
IN_SIZES = (WA, WA, WA, WB, WB, WB, WIQ, D_I, H_I, WB, WB, WB, WB)

LOG2E = math.log2(math.e)
NT_DIMS = (((1,), (1,)), ((), ()))


def _cparams(n_axes):
    return pltpu.CompilerParams(dimension_semantics=("arbitrary",) * n_axes,
                                vmem_limit_bytes=VMEM_LIMIT)


def _const_spec(shape):
    nd = len(shape)
    return pl.BlockSpec(shape, lambda *_: (0,) * nd, pipeline_mode=pl.Buffered(1))


def _dot(a, b):
    return jnp.dot(a, b, preferred_element_type=F32)


def _dot_nt(a, b):
    return lax.dot_general(a, b, NT_DIMS, preferred_element_type=F32)


def _rope(u, cos, sin, half):
    w = u.shape[-1]
    lane = lax.broadcasted_iota(jnp.int32, u.shape, 1)
    first = (lane % (2 * half)) < half
    rot = jnp.where(first, -pltpu.roll(u, w - half, 1), pltpu.roll(u, half, 1))
    return u * cos + rot * sin


N_PROJ_INPUTS = 8


def _inproj_kernel(*refs, n_aliased, feature_major_caches):
    x_ref, g_ref, w_ref, lb_ref, ca_ref, sa_ref, cb_ref, sb_ref = refs[:N_PROJ_INPUTS]
    (aq_ref, ak_ref, av_ref, ak16_ref, av16_ref, bq_ref, bk_ref, bv_ref, bk16_ref, bv16_ref,
     iq_ref, ik2_ref, ik_ref, iw_ref, cq_ref, ck_ref, cgl_ref, cv_ref, cg_ref) = refs[N_PROJ_INPUTS + n_aliased:]

    def put_cache(ref, val, width):
        ref[...] = val.T[:width, :] if feature_major_caches else val[:, :width]

    x = x_ref[...]
    ms = jnp.mean(x * x, axis=-1, keepdims=True)
    xn = (x * lax.rsqrt(ms + EPS) * g_ref[...]).astype(BF16)

    group_dots = {}

    def seg(name):
        off, width = SEG_OFF[name]
        g_off, g_width = next((o, wd) for o, wd in PROJ_GROUPS if o <= off < o + wd)
        if g_off not in group_dots:
            group_dots[g_off] = _dot(xn, w_ref[:, g_off:g_off + g_width])
        return group_dots[g_off][:, off - g_off:off - g_off + width]

    ca, sa = ca_ref[...], sa_ref[...]
    cb, sb = cb_ref[...], sb_ref[...]
    aq_ref[...] = (_rope(seg("aq"), ca, sa, DA // 2) * (DA ** -0.5 * LOG2E)).astype(BF16)
    ak = _rope(seg("ak"), ca, sa, DA // 2)
    put_cache(ak_ref, ak, WA)
    ak16_ref[...] = ak.astype(BF16)
    av = seg("av")
    put_cache(av_ref, av, WA)
    av16_ref[...] = av.astype(BF16)
    bq_ref[...] = (_rope(seg("bq"), cb, sb, D_B // 2) * (D_B ** -0.5 * LOG2E)).astype(BF16)
    bk = _rope(seg("bk"), cb, sb, D_B // 2)
    put_cache(bk_ref, bk, WB)
    bk16_ref[...] = bk.astype(BF16)
    bv = seg("bv")
    put_cache(bv_ref, bv, WB)
    bv16_ref[...] = bv.astype(BF16)
    iq_ref[...] = (_rope(seg("iq"), cb[:, :WIQ], sb[:, :WIQ], D_I // 2) * (D_I ** -0.5)).astype(BF16)
    ik2 = _rope(seg("ik"), cb[:, :2 * D_I], sb[:, :2 * D_I], D_I // 2)
    ik2_ref[...] = ik2.astype(BF16)
    put_cache(ik_ref, ik2, D_I)
    iw_ref[...] = seg("iw") * (H_I ** -0.5)

    cq = seg("cq")
    cq_ref[...] = cq * jax.nn.sigmoid(cq)
    z = seg("cf")
    lb = lb_ref[...]
    la = jnp.log(jnp.maximum(lb, LB_FLOOR))
    lsig = jnp.minimum(z, 0.0) - jnp.log1p(jnp.exp(-jnp.abs(z)))
    bb = jnp.log1p(-lb) + lsig
    cgl_ref[...] = jnp.maximum(la, bb) + jnp.log1p(jnp.exp(-jnp.abs(la - bb)))
    ck_ref[...] = (1.0 - lb) * jax.nn.sigmoid(-z)
    cv_ref[...] = seg("ci")
    cg_ref[...] = seg("cg")


CACHE_OUTS = ("ak", "av", "bk", "bv", "ik")


def _in_projection(x2d, gain, w_pack, lb_row, tabs, tm, stacked=None):
    n = x2d.shape[0]
    ca, sa, cb, sb = tabs
    period = ca.shape[0] // tm
    row = lambda w: pl.BlockSpec((tm, w), lambda i: (i, 0))
    tab = lambda w: pl.BlockSpec((tm, w), lambda i: (i % period, 0))
    outs = (("aq", WA, BF16), ("ak", WA, F32), ("av", WA, F32), ("ak16", WA, BF16), ("av16", WA, BF16),
            ("bq", WBP, BF16), ("bk", WB, F32), ("bv", WB, F32), ("bk16", WBP, BF16), ("bv16", WBP, BF16),
            ("iq", WIQ, BF16), ("ik2", 2 * D_I, BF16), ("ik", D_I, F32), ("iw", LANES, F32),
            ("cq", WBP, F32), ("ck", WBP, F32), ("cgl", WBP, F32), ("cv", WBP, F32), ("cg", WBP, F32))
    out_specs = {name: row(w) for name, w, _ in outs}
    out_shape = {name: jax.ShapeDtypeStruct((n, w), dt) for name, w, dt in outs}
    inputs = [x2d, gain, w_pack, lb_row, ca, sa, cb, sb]
    in_specs = [row(x2d.shape[1]), _const_spec(gain.shape), _const_spec(w_pack.shape),
                _const_spec(lb_row.shape), tab(WA), tab(WA), tab(WBP), tab(WBP)]
    aliases = {}
    if stacked is not None:
        layer, depth, batch, caches = stacked
        t = n // batch
        tiles = t // tm
        for name, w, _ in outs:
            if name in CACHE_OUTS:
                out_specs[name] = pl.BlockSpec((None, None, w, tm),
                                               lambda i, layer=layer: (layer, i // tiles, 0, i % tiles))
                out_shape[name] = jax.ShapeDtypeStruct((depth, batch, w, t), F32)
        if caches is not None:
            names = [o[0] for o in outs]
            for name in CACHE_OUTS:
                aliases[len(inputs)] = names.index(name)
                inputs.append(caches[name])
                in_specs.append(pl.BlockSpec(memory_space=pl.ANY))
    kern = functools.partial(_inproj_kernel, n_aliased=len(aliases), feature_major_caches=stacked is not None)
    res = pl.pallas_call(
        kern,
        grid=(n // tm,),
        in_specs=in_specs,
        out_specs=[out_specs[o[0]] for o in outs],
        out_shape=[out_shape[o[0]] for o in outs],
        input_output_aliases=aliases,
        compiler_params=_cparams(1),
        name="in_projection",
    )(*inputs)
    return dict(zip([o[0] for o in outs], res))


NARROW_KEY_BLOCK = 256


def _attn_a_kernel(lam_ref, q_ref, k_ref, v_ref, o_ref, q4_ref, m_ref, acc_ref, *, tq, tk, q_off, kv_len):
    qi = pl.program_id(2)
    q = q_ref[...]
    lane = lax.broadcasted_iota(jnp.int32, q.shape, 1)
    for i in range(4):
        q4_ref[i * tq:(i + 1) * tq, :] = jnp.where(lane // DA == i, q, jnp.zeros_like(q))
    m_ref[...] = jnp.full(m_ref.shape, NEG_BIG, F32)
    acc_ref[...] = jnp.zeros(acc_ref.shape, F32)

    q_first = q_off + qi * tq
    q_last = q_first + tq - 1
    n_full = jnp.minimum((q_first // CHUNK + 1) * CHUNK, kv_len) // tk
    lim = jnp.minimum((q_last // CHUNK + 1) * CHUNK, kv_len)
    tkn = min(tk, NARROW_KEY_BLOCK)
    n_narrow = (lim + tkn - 1) // tkn
    n_wide = n_full + (lim - n_full * tk) // tk if tkn < tk else n_narrow

    def step(j, tk, masked):
        start = pl.multiple_of(j * tk, tk)
        kb = k_ref[pl.ds(start, tk), :]
        vb = v_ref[pl.ds(start, tk), :]
        s = _dot_nt(q4_ref[...], kb)
        if masked:
            kpos = start + lax.broadcasted_iota(jnp.int32, (tq, tk), 1)
            qpos = q_first + lax.broadcasted_iota(jnp.int32, (tq, tk), 0)
            valid = (kpos // CHUNK <= qpos // CHUNK) & (kpos < kv_len)
            s = jnp.where(valid[None], s.reshape(4, tq, tk), NEG_BIG).reshape(4 * tq, tk)
        m_old = m_ref[...]
        m_new = jnp.maximum(m_old, jnp.max(s, axis=-1, keepdims=True))
        alpha = jnp.exp2(m_old - m_new)
        p = jnp.exp2(s - jnp.concatenate([m_new] * (tk // LANES), axis=1)).astype(BF16)
        lane_v = lax.broadcasted_iota(jnp.int32, vb.shape, 1)
        ones = jnp.ones_like(vb)
        pv = jnp.concatenate([_dot(p[0:2 * tq], jnp.where(lane_v < DV_A, vb, ones)),
                              _dot(p[2 * tq:4 * tq], jnp.where(lane_v < DV_A, ones, vb))], axis=0)
        acc_ref[...] = alpha * acc_ref[...] + pv
        m_ref[...] = m_new

    def body(tk, masked):
        def run(j, c):
            step(j, tk, masked)
            return c
        return run

    lax.fori_loop(0, n_full, body(tk, False), 0)
    lax.fori_loop(n_full, n_wide, body(tk, True), 0)
    if tkn < tk:
        lax.fori_loop(n_wide * (tk // tkn), n_narrow, body(tkn, True), 0)

    lam = lam_ref[0]
    acc0 = acc_ref[0:2 * tq, :]
    acc1 = acc_ref[2 * tq:4 * tq, :]
    on0 = acc0 / acc0[:, DV_A:DV_A + 1]
    on1 = acc1 / acc1[:, 0:1]
    o0 = on0[0:tq] - lam * on0[tq:2 * tq]
    o1 = on1[0:tq] - lam * on1[tq:2 * tq]
    o_ref[...] = jnp.where(lane < DV_A, o0, o1)


def _attention_a(lam, q, k, v, *, q_off, kv_len, tq, tk):
    b, t, _ = q.shape
    lk = k.shape[1]
    kern = functools.partial(_attn_a_kernel, tq=tq, tk=tk, q_off=q_off, kv_len=kv_len)
    return pl.pallas_call(
        kern,
        grid=(b, WA // LANES, t // tq),
        in_specs=[pl.BlockSpec(memory_space=pltpu.SMEM),
                  pl.BlockSpec((None, tq, LANES), lambda bi, hi, qi: (bi, qi, hi)),
                  pl.BlockSpec((None, lk, LANES), lambda bi, hi, qi: (bi, 0, hi)),
                  pl.BlockSpec((None, lk, LANES), lambda bi, hi, qi: (bi, 0, hi))],
        out_specs=pl.BlockSpec((None, tq, LANES), lambda bi, hi, qi: (bi, qi, hi)),
        out_shape=jax.ShapeDtypeStruct((b, t, WA), F32),
        scratch_shapes=[pltpu.VMEM((4 * tq, LANES), BF16), pltpu.VMEM((4 * tq, LANES), F32),
                        pltpu.VMEM((4 * tq, LANES), F32)],
        compiler_params=_cparams(3),
        name="mixer_a",
    )(lam, q, k, v)


TIE_BLOCK = 256
SEARCH_BITS_PER_TRIP = 3


def _attn_b_kernel(q_ref, iq_ref, iw_ref, k_ref, v_ref, ik_ref, o_ref, key_ref, bias_ref,
                   *, tq, q_off, kv_len, n_sel, variants):
    q_first = q_off + pl.program_id(1) * tq
    need_keys = jnp.minimum(((q_first + tq - 1) // CHUNK + 1) * CHUNK, kv_len)
    lo = 0
    for lk in variants:
        @pl.when((need_keys > lo) & (need_keys <= lk))
        def _(lk=lk):
            _attn_b_body(q_ref, iq_ref, iw_ref, k_ref, v_ref, ik_ref, o_ref, key_ref, bias_ref,
                         tq=tq, q_first=q_first, need_keys=need_keys, kv_len=kv_len, n_sel=n_sel, lk=lk)
        lo = lk


def _stack_heads(x, n_heads, width):
    lane = lax.broadcasted_iota(jnp.int32, x.shape, 1)
    return jnp.concatenate([jnp.where(lane // width == h, x, jnp.zeros_like(x)) for h in range(n_heads)],
                           axis=0)


def _attn_b_body(q_ref, iq_ref, iw_ref, k_ref, v_ref, ik_ref, o_ref, key_ref, bias_ref,
                 *, tq, q_first, need_keys, kv_len, n_sel, lk):
    ik2 = ik_ref[0:lk, :]
    iq = iq_ref[...]
    iw = iw_ref[...]

    score = jnp.zeros((tq, lk), F32)
    for pi in range(H_I // 2):
        y = _stack_heads(iq[:, LANES * pi:LANES * (pi + 1)], 2, D_I)
        d = jnp.maximum(_dot_nt(y, ik2), 0.0)
        score = score + iw[:, 2 * pi:2 * pi + 1] * d[0:tq] + iw[:, 2 * pi + 1:2 * pi + 2] * d[tq:2 * tq]

    kpos = lax.broadcasted_iota(jnp.int32, (tq, lk), 1)
    qpos = q_first + lax.broadcasted_iota(jnp.int32, (tq, lk), 0)
    valid = (kpos // CHUNK <= qpos // CHUNK) & (kpos < kv_len)
    score = jnp.where(valid, score, NEG_BIG)

    bits = lax.bitcast_convert_type(score, jnp.int32)
    key = bits ^ ((bits >> 31) & jnp.int32(0x7FFFFFFF))
    key = jnp.where(key == -1, 0, key)
    key_ref[:, 0:lk] = key

    kf = float(n_sel)

    def count_ge(cand):
        wide = jnp.concatenate([cand] * (lk // LANES), axis=1)
        return jnp.sum(jnp.where(key_ref[:, 0:lk] >= wide, 1.0, 0.0), axis=-1, keepdims=True)

    int_min = jnp.int32(-2 ** 31)
    zero = jnp.zeros((tq, LANES), jnp.int32)
    select_all = need_keys <= n_sel
    cur = jnp.where(select_all | (count_ge(zero) < kf), int_min, zero)

    def try_bit(cur, bit):
        cand = cur | bit
        return jnp.where(count_ge(cand) >= kf, cand, cur)

    cur = jnp.where(select_all, int_min, try_bit(cur, jnp.int32(1 << 30)))

    def bits_body(i, cur):
        for j in range(SEARCH_BITS_PER_TRIP):
            cur = try_bit(cur, jnp.left_shift(jnp.int32(1), 29 - (SEARCH_BITS_PER_TRIP * i + j)))
        return cur

    thr = lax.fori_loop(0, jnp.where(select_all, 0, 30 // SEARCH_BITS_PER_TRIP), bits_body, cur)

    key = key_ref[:, 0:lk]
    thr = jnp.concatenate([thr] * (lk // LANES), axis=1)
    gt = key > thr
    eq = key == thr
    need = kf - jnp.sum(jnp.where(gt, 1.0, 0.0), axis=-1, keepdims=True)
    r_i = lax.broadcasted_iota(jnp.int32, (TIE_BLOCK, TIE_BLOCK), 0)
    c_i = lax.broadcasted_iota(jnp.int32, (TIE_BLOCK, TIE_BLOCK), 1)
    tri = jnp.where(r_i <= c_i, 1.0, 0.0).astype(BF16)
    carry = jnp.zeros((tq, 1), F32)
    for jb in range(lk // TIE_BLOCK):
        sl = slice(jb * TIE_BLOCK, (jb + 1) * TIE_BLOCK)
        eq_b = eq[:, sl]
        pref = _dot(jnp.where(eq_b, 1.0, 0.0).astype(BF16), tri) + carry
        carry = pref[:, TIE_BLOCK - 1:TIE_BLOCK]
        sel = (gt[:, sl] | (eq_b & (pref <= need))) & valid[:, sl]
        bias_ref[:, sl] = jnp.where(sel, 0.0, NEG_BIG)

    q = q_ref[...]
    lane = lax.broadcasted_iota(jnp.int32, (tq, LANES), 1)
    bias = bias_ref[:, 0:lk]
    for p in range(NPAIR):
        n_heads = min(2, H_B - 2 * p)
        sl = slice(p * LANES, (p + 1) * LANES)
        y = _stack_heads(q[:, sl], n_heads, D_B)
        s = _dot_nt(y, k_ref[0:lk, sl]).reshape(n_heads, tq, lk) + bias[None]
        m = jnp.max(s, axis=-1, keepdims=True)
        pr = jnp.exp2(s - m)
        l = jnp.sum(pr, axis=-1, keepdims=True)
        o = _dot(pr.reshape(n_heads * tq, lk).astype(BF16), v_ref[0:lk, sl]) / l.reshape(n_heads * tq, 1)
        o_ref[:, sl] = o if n_heads == 1 else jnp.where(lane < D_B, o[0:tq], o[tq:2 * tq])


def _attention_b(q, iq, iw, k, v, ik2, *, q_off, kv_len, tq, n_sel):
    b, t, _ = q.shape
    lk = k.shape[1]
    granule = tq if (tq % TIE_BLOCK == 0 and lk % (2 * tq) == 0) else lk
    variants = tuple(v for v in range(granule, lk + 1, granule) if v <= lk // 2 or v % (2 * granule) == 0 or v == lk)
    kern = functools.partial(_attn_b_kernel, tq=tq, q_off=q_off, kv_len=kv_len, n_sel=n_sel,
                             variants=variants)
    qspec = lambda w: pl.BlockSpec((None, tq, w), lambda bi, qi: (bi, qi, 0))
    kspec = lambda w: pl.BlockSpec((None, lk, w), lambda bi, qi: (bi, 0, 0))
    return pl.pallas_call(
        kern,
        grid=(b, t // tq),
        in_specs=[qspec(WBP), qspec(WIQ), qspec(LANES), kspec(WBP), kspec(WBP), kspec(2 * D_I)],
        out_specs=qspec(WBP),
        out_shape=jax.ShapeDtypeStruct((b, t, WBP), F32),
        scratch_shapes=[pltpu.VMEM((tq, lk), jnp.int32), pltpu.VMEM((tq, lk), F32)],
        compiler_params=_cparams(2),
        name="mixer_b",
    )(q, iq, iw, k, v, ik2)


HGRN2_CHUNKS_PER_STEP = 2
GROUP = SUBLANES
DECAY_SPLIT_LIMIT = 60.0


def _hgrn2_kernel(q_ref, k_ref, g_ref, v_ref, s0_ref, o_ref, s_out_ref, st_ref, *, c, chunks):
    ci = pl.program_id(1)

    @pl.when(ci == 0)
    def _():
        st_ref[...] = s0_ref[...]

    for i in range(chunks):
        rows = pl.ds(i * c, c)
        _hgrn2_chunk(q_ref.at[rows], k_ref.at[rows], g_ref.at[rows], v_ref.at[rows], o_ref.at[rows], st_ref, c=c)

    @pl.when(ci == pl.num_programs(1) - 1)
    def _():
        s_out_ref[...] = st_ref[...]


def _hgrn2_chunk(q_ref, k_ref, g_ref, v_ref, o_ref, st_ref, *, c):
    q = q_ref[...]
    k = k_ref[...]
    g = g_ref[...]
    v = v_ref[...]
    w = q.shape[-1]
    row = lax.broadcasted_iota(jnp.int32, (c, w), 0)
    lane = lax.broadcasted_iota(jnp.int32, (c, w), 1)

    def segment_scans():
        cs, tots = {1: g}, {1: g}
        cum, tot, m = g, g, 1
        while m < c:
            upper = (row // m) % 2 == 1
            prev_tot = pltpu.roll(tot, m, 0)
            next_tot = pltpu.roll(tot, c - m, 0)
            cum = cum + jnp.where(upper, prev_tot, 0.0)
            tot = tot + jnp.where(upper, prev_tot, next_tot)
            m *= 2
            cs[m], tots[m] = cum, tot
        return cs, tots

    def cumsum_rows():
        cum, m = g, 1
        while m < c:
            cum = cum + jnp.where(row >= m, pltpu.roll(cum, m, 0), 0.0)
            m *= 2
        return cum

    head_masks = [lane // DK_C == h for h in range(w // DK_C)]
    n_heads = H_C
    rq = lax.broadcasted_iota(jnp.int32, (c, c), 0)
    rk = lax.broadcasted_iota(jnp.int32, (c, c), 1)

    def attend(att):
        res = _dot(att.astype(BF16), v.astype(BF16))
        out = jnp.zeros((c, w), F32)
        for h in range(n_heads):
            out = out + jnp.where(head_masks[h], res[h * c:(h + 1) * c], 0.0)
        return out

    def stack_heads(x):
        return jnp.concatenate([jnp.where(head_masks[h], x, 0.0) for h in range(n_heads)], axis=0)

    def finish(o_intra, bcum, blast):
        qe = (q * jnp.exp(bcum)).astype(BF16)
        k2 = (k * jnp.exp(blast - bcum)).astype(BF16)
        decay = jnp.exp(blast[0:1, :])
        pr = lax.broadcasted_iota(jnp.int32, (LANES, LANES), 0)
        pc = lax.broadcasted_iota(jnp.int32, (LANES, LANES), 1)
        diag = pr // DK_C == pc // DK_C
        o_state = []
        for p in range(w // LANES):
            sl = slice(p * LANES, (p + 1) * LANES)
            st = st_ref[p]
            o_state.append(_dot_nt(qe[:, sl], st.astype(BF16)))
            upd = _dot(v[:, sl].T.astype(BF16), k2[:, sl])
            st_ref[p] = st * decay[:, sl] + jnp.where(diag, upd, 0.0)
        o_ref[...] = o_intra + jnp.concatenate(o_state, axis=-1)

    ag = jnp.abs(g)
    half_bound = jnp.maximum(jnp.max(jnp.sum(ag[0:c // 2], axis=0, keepdims=True)),
                             jnp.max(jnp.sum(ag[c // 2:c], axis=0, keepdims=True)))
    single_split = half_bound <= DECAY_SPLIT_LIMIT

    @pl.when(single_split)
    def _():
        bcum = cumsum_rows()
        dmid = bcum - bcum[c // 2 - 1:c // 2, :]
        qt = stack_heads(q * jnp.exp(dmid)).astype(BF16)
        kt = (k * jnp.exp(-dmid)).astype(BF16)
        causal = jnp.concatenate([rq >= rk] * n_heads, axis=0)
        o_intra = attend(jnp.where(causal, _dot_nt(qt, kt), 0.0))
        finish(o_intra, bcum, jnp.broadcast_to(bcum[c - 1:c, :], (c, w)))

    @pl.when(jnp.logical_not(single_split))
    def _():
        cs, tots = segment_scans()
        att = jnp.zeros((n_heads * c, c), F32)
        half = GROUP
        while half < c:
            upper = (row // half) % 2 == 1
            qt = jnp.where(upper, q * jnp.exp(cs[half]), 0.0)
            kt = jnp.where(upper, 0.0, k * jnp.exp(tots[half] - cs[half])).astype(BF16)
            blk = _dot_nt(stack_heads(qt).astype(BF16), kt)
            same = (rq // (2 * half)) == (rk // (2 * half))
            same = jnp.concatenate([same] * n_heads, axis=0)
            att = att + jnp.where(same, blk, 0.0)
            half *= 2
        out = attend(att)

        def group_row(x, j):
            x3 = x.reshape(c // GROUP, GROUP, w)
            return jnp.broadcast_to(x3[:, j:j + 1, :], x3.shape).reshape(c, w)

        c8 = cs[GROUP]
        vals = []
        for j in range(GROUP):
            ok = (row % GROUP) >= j
            e = jnp.where(ok, c8 - group_row(c8, j), 0.0)
            vals.append(jnp.where(ok, q * group_row(k, j) * jnp.exp(e), 0.0))
        r_i = lax.broadcasted_iota(jnp.int32, (w, w), 0)
        c_i = lax.broadcasted_iota(jnp.int32, (w, w), 1)
        head_sum = jnp.where(r_i // DK_C == c_i // DK_C, 1.0, 0.0).astype(BF16)
        wts = _dot(jnp.concatenate(vals, axis=0).astype(BF16), head_sum)
        for j in range(GROUP):
            out = out + wts[j * c:(j + 1) * c] * group_row(v, j)
        finish(out, cs[c], tots[c])


def _hgrn2(q, k, g, v, s0, *, c):
    b, t, w = q.shape
    chunks = HGRN2_CHUNKS_PER_STEP if t % (HGRN2_CHUNKS_PER_STEP * c) == 0 else 1
    kern = functools.partial(_hgrn2_kernel, c=c, chunks=chunks)
    blk = pl.BlockSpec((None, chunks * c, w), lambda bi, ci: (bi, ci, 0))
    sblk = pl.BlockSpec((None, w // LANES, LANES, LANES), lambda bi, ci: (bi, 0, 0, 0))
    return pl.pallas_call(
        kern,
        grid=(b, t // (chunks * c)),
        in_specs=[blk, blk, blk, blk, sblk],
        out_specs=[blk, sblk],
        out_shape=[jax.ShapeDtypeStruct((b, t, w), F32),
                   jax.ShapeDtypeStruct((b, w // LANES, LANES, LANES), F32)],
        scratch_shapes=[pltpu.VMEM((w // LANES, LANES, LANES), F32)],
        compiler_params=_cparams(2),
        name="mixer_c",
    )(q, k, g, v, s0)


def _state_to_pairs(s):
    b = s.shape[0]
    st = jnp.swapaxes(s.astype(F32), -1, -2)
    st = jnp.pad(st, ((0, 0), (0, 2 * NPAIR - H_C), (0, 0), (0, 0)))
    st = st.reshape(b, NPAIR, 2, DV_C, DK_C)
    eye = jnp.eye(2, dtype=F32)
    full = st[:, :, :, :, None, :] * eye[None, None, :, None, :, None]
    return full.reshape(b, NPAIR, 2 * DV_C, 2 * DK_C)


def _pairs_to_state(sp):
    b = sp.shape[0]
    s6 = sp.reshape(b, NPAIR, 2, DV_C, 2, DK_C)
    diag = jnp.stack([s6[:, :, a, :, a, :] for a in range(2)], axis=2)
    return jnp.swapaxes(diag.reshape(b, 2 * NPAIR, DV_C, DK_C)[:, :H_C], -1, -2)


FF_BLOCK = 2816


def _head_norm(y, gain):
    w = y.shape[-1]
    r_i = lax.broadcasted_iota(jnp.int32, (w, w), 0)
    c_i = lax.broadcasted_iota(jnp.int32, (w, w), 1)
    head_sum = jnp.where(r_i // DV_A == c_i // DV_A, 1.0, 0.0).astype(BF16)
    y2 = y * y
    hi = y2.astype(BF16)
    lo = (y2 - hi.astype(F32)).astype(BF16)
    ms = (_dot(hi, head_sum) + _dot(lo, head_sum)) * (1.0 / DV_A)
    return y * lax.rsqrt(ms + EPS) * gain


def _merge_kernel(scal_ref, x_ref, oa_ref, ob_ref, oc_ref, cg_ref, hist_ref, ag_ref, cgn_ref, wo_ref,
                  n2_ref, wup_ref, cw_ref, cb_ref, wdn_ref, fn_ref, *out_and_scratch, d_ff, final):
    if final:
        x_out_ref, fc_ref, y_ref, carry_ref = out_and_scratch
    else:
        x_out_ref, fc_ref, carry_ref = out_and_scratch
    ti = pl.program_id(1)
    tm = x_ref.shape[0]

    @pl.when(ti == 0)
    def _():
        carry_ref[...] = hist_ref[...]

    oa = _head_norm(oa_ref[...], ag_ref[...]) * scal_ref[0]
    cg = cg_ref[...]
    oc = _head_norm(oc_ref[...], cgn_ref[...]) * (cg * jax.nn.sigmoid(cg))
    mixed = (_dot(oa.astype(BF16), wo_ref[0:WA, :])
             + _dot(ob_ref[...].astype(BF16), wo_ref[WA:WA + WBP, :])
             + _dot(oc.astype(BF16), wo_ref[WA + WBP:WA + 2 * WBP, :]))
    x = x_ref[...] + mixed

    ms = jnp.mean(x * x, axis=-1, keepdims=True)
    h = (x * lax.rsqrt(ms + EPS) * n2_ref[...]).astype(BF16)
    row = lax.broadcasted_iota(jnp.int32, (tm, FF_BLOCK), 0)
    acc = jnp.zeros(x.shape, F32)
    for cblk in range(d_ff // FF_BLOCK):
        sl = slice(cblk * FF_BLOCK, (cblk + 1) * FF_BLOCK)
        a = _dot(h, wup_ref[:, sl])
        gate = _dot(h, wup_ref[:, d_ff + cblk * FF_BLOCK:d_ff + (cblk + 1) * FF_BLOCK])
        prev2 = carry_ref[0:1, sl]
        prev1 = carry_ref[1:2, sl]
        a1 = jnp.where(row == 0, prev1, pltpu.roll(a, 1, 0))
        a2 = jnp.where(row == 0, prev2, jnp.where(row == 1, prev1, pltpu.roll(a, 2, 0)))
        conv = cb_ref[:, sl] + a2 * cw_ref[0:1, sl] + a1 * cw_ref[1:2, sl] + a * cw_ref[2:3, sl]
        act = conv * jax.nn.sigmoid(conv) * gate
        acc = acc + _dot(act.astype(BF16), wdn_ref[sl, :])
        carry_ref[:, sl] = a[tm - (CONV_W - 1):, :]
    x = x + acc
    x_out_ref[...] = x
    fc_ref[...] = carry_ref[...]
    if final:
        ms = jnp.mean(x * x, axis=-1, keepdims=True)
        y_ref[...] = x * lax.rsqrt(ms + EPS) * fn_ref[...]


def _merge_ffn(scal, x, oa, ob, oc, cg, hist, a_gain, c_gain, wo, n2, wup, cw, cb, wdn, fnorm, *, tm, final):
    b, t, d = x.shape
    d_ff = wdn.shape[0]
    kern = functools.partial(_merge_kernel, d_ff=d_ff, final=final)
    blk = lambda w: pl.BlockSpec((None, tm, w), lambda bi, ti: (bi, ti, 0))
    per_b = pl.BlockSpec((None, CONV_W - 1, d_ff), lambda bi, ti: (bi, 0, 0))
    out_specs = [blk(d), per_b]
    out_shape = [jax.ShapeDtypeStruct((b, t, d), F32), jax.ShapeDtypeStruct((b, CONV_W - 1, d_ff), F32)]
    if final:
        out_specs.append(blk(d))
        out_shape.append(jax.ShapeDtypeStruct((b, t, d), F32))
    return pl.pallas_call(
        kern,
        grid=(b, t // tm),
        in_specs=[pl.BlockSpec(memory_space=pltpu.SMEM), blk(d), blk(WA), blk(WBP), blk(WBP), blk(WBP), per_b,
                  _const_spec(a_gain.shape), _const_spec(c_gain.shape), _const_spec(wo.shape),
                  _const_spec(n2.shape), _const_spec(wup.shape), _const_spec(cw.shape),
                  _const_spec(cb.shape), _const_spec(wdn.shape), _const_spec(fnorm.shape)],
        out_specs=out_specs,
        out_shape=out_shape,
        scratch_shapes=[pltpu.VMEM((CONV_W - 1, d_ff), F32)],
        compiler_params=_cparams(2),
        name="merge_ffn",
    )(scal, x, oa, ob, oc, cg, hist, a_gain, c_gain, wo, n2, wup, cw, cb, wdn, fnorm)


def _pack_w_in(w):
    parts = jnp.split(w, np.cumsum(IN_SIZES)[:-1].tolist(), axis=-1)
    cols = []
    for (name, width, padded), part in zip(SEGS, parts):
        if name == "ik":
            part = jnp.concatenate([part, part], axis=-1)
            width = 2 * D_I
        cols.append(jnp.pad(part, ((0, 0), (0, padded - width))))
    return jnp.concatenate(cols, axis=-1).astype(BF16)


def _pack_w_out(w):
    wa, wb, wc = w[:WA], w[WA:WA + WB], w[WA + WB:]
    pad = lambda m: jnp.pad(m, ((0, WBP - WB), (0, 0)))
    return jnp.concatenate([wa, pad(wb), pad(wc)], axis=0).astype(BF16)


def _rope_table(pos, dim, width):
    inv_freq = ROPE_THETA ** (-jnp.arange(0, dim, 2, dtype=F32) / dim)
    ang = pos.astype(F32)[:, None] * inv_freq[None, :]
    reps = width // (dim // 2)
    return jnp.tile(jnp.cos(ang), (1, reps)), jnp.tile(jnp.sin(ang), (1, reps))


def _cache_prep_kernel(c_ref, o_ref, *, past):
    x = c_ref[...]
    n_feat = x.shape[0]
    if n_feat < LANES:
        x = jnp.concatenate([x] * (LANES // n_feat), axis=0)
    elif n_feat % LANES:
        x = jnp.concatenate([x, jnp.zeros((LANES - n_feat % LANES, past), x.dtype)], axis=0)
    for j in range(x.shape[0] // LANES):
        o_ref[0:past, j * LANES:(j + 1) * LANES] = x[j * LANES:(j + 1) * LANES].T.astype(BF16)
    o_ref[past:, :] = jnp.zeros((o_ref.shape[0] - past, o_ref.shape[1]), BF16)


def _cache_prep(cache, rows):
    depth, batch, past = cache.shape[:3]
    n_feat = int(np.prod(cache.shape[3:]))
    nd = cache.ndim
    c = jnp.transpose(cache, (0, 1) + tuple(range(3, nd)) + (2,)).reshape(depth * batch, n_feat, past)
    width = _round_up(n_feat, LANES)
    out = pl.pallas_call(
        functools.partial(_cache_prep_kernel, past=past),
        grid=(depth * batch,),
        in_specs=[pl.BlockSpec((None, n_feat, past), lambda i: (i, 0, 0))],
        out_specs=pl.BlockSpec((None, rows, width), lambda i: (i, 0, 0)),
        out_shape=jax.ShapeDtypeStruct((depth * batch, rows, width), BF16),
        compiler_params=_cparams(1),
        name="cache_prep",
    )(c)
    return out.reshape(depth, batch, rows, width)


def _round_up(n, m):
    return (n + m - 1) // m * m


def kernel(x_prompt, x_sample, cache_a_k, cache_a_v, cache_b_k, cache_b_v, cache_b_kidx, state_c, state_ffn_conv, norm1, w_in, lam_q1, lam_k1, lam_q2, lam_k2, a_norm, c_lower, c_norm, w_out, norm2, ffn_up, ffn_conv_w, ffn_conv_b, ffn_down, final_norm):
    depth = w_in.shape[0]
    b_p, t_p, d = x_prompt.shape
    b_s, t_s, _ = x_sample.shape
    past = cache_a_k.shape[2]
    d_ff = ffn_down.shape[1]
    kv_s = past + t_s
    n_sel_p = min(TOPK_MAX, t_p // 4)
    n_sel_s = min(TOPK_MAX, kv_s // 4)
    tk = 256
    lk_s = _round_up(kv_s, tk)

    lb_soft = jax.nn.softmax(c_lower.astype(F32), axis=0)
    lower = jnp.cumsum(lb_soft, axis=0) - lb_soft[0]
    lower = jnp.pad(lower, ((0, 0), (0, WBP - WB)))

    tm_p = min(512, t_p)
    tm_s = min(256, b_s * t_s)
    pos_p = jnp.arange(t_p)
    pos_s = jnp.tile(past + jnp.arange(t_s), tm_s // t_s)
    tabs_p = _rope_table(pos_p, DA, WA) + _rope_table(pos_p, D_B, WBP)
    tabs_s = _rope_table(pos_s, DA, WA) + _rope_table(pos_s, D_B, WBP)

    past_kv = {name: _cache_prep(c, lk_s) for name, c in
               (("ak", cache_a_k), ("av", cache_a_v), ("bk", cache_b_k), ("bv", cache_b_v), ("ik", cache_b_kidx))}

    xp, xs = x_prompt, x_sample
    caches_p, state_p, conv_p = None, [], []
    outs_s = [[] for _ in range(7)]
    y_p = y_s = None
    fnorm = final_norm.reshape(1, d)
    for l in range(depth):
        lam_init = 0.8 - 0.6 * math.exp(-0.3 * l)
        lam = (jnp.exp(jnp.sum(lam_q1[l].astype(F32) * lam_k1[l].astype(F32)))
               - jnp.exp(jnp.sum(lam_q2[l].astype(F32) * lam_k2[l].astype(F32))) + lam_init)
        lam_arr = lam.reshape(1).astype(F32)
        scal = jnp.full((1,), 1.0 - lam_init, F32)
        w_pack = _pack_w_in(w_in[l])
        wo = _pack_w_out(w_out[l])
        gain1 = norm1[l].reshape(1, d)
        gain2 = norm2[l].reshape(1, d)
        lb_row = lower[l].reshape(1, WBP)
        a_gain = jnp.tile(a_norm[l], H_A).reshape(1, WA)
        c_gain = jnp.pad(jnp.tile(c_norm[l], H_C), (0, WBP - WB)).reshape(1, WBP)
        wup = ffn_up[l].astype(BF16)
        wdn = ffn_down[l].astype(BF16)
        cw = ffn_conv_w[l]
        cb = ffn_conv_b[l].reshape(1, d_ff)
        final = l == depth - 1

        u = _in_projection(xp.reshape(b_p * t_p, d), gain1, w_pack, lb_row, tabs_p, tm_p,
                           stacked=(l, depth, b_p, caches_p))
        caches_p = {name: u[name] for name in CACHE_OUTS}
        r3 = lambda a, b=b_p, t=t_p: a.reshape(b, t, a.shape[-1])
        oa = _attention_a(lam_arr, r3(u["aq"]), r3(u["ak16"]), r3(u["av16"]),
                          q_off=0, kv_len=t_p, tq=min(256, t_p), tk=min(512, t_p))
        ob = _attention_b(r3(u["bq"]), r3(u["iq"]), r3(u["iw"]), r3(u["bk16"]), r3(u["bv16"]), r3(u["ik2"]),
                          q_off=0, kv_len=t_p, tq=min(256, t_p), n_sel=n_sel_p)
        s0 = jnp.zeros((b_p, NPAIR, LANES, LANES), F32)
        oc, s_new = _hgrn2(r3(u["cq"]), r3(u["ck"]), r3(u["cgl"]), r3(u["cv"]), s0, c=CHUNK)
        hist0 = jnp.zeros((b_p, CONV_W - 1, d_ff), F32)
        res = _merge_ffn(scal, xp, oa, ob, oc, r3(u["cg"]), hist0, a_gain, c_gain, wo, gain2, wup, cw, cb,
                         wdn, fnorm, tm=min(512, t_p), final=final)
        xp, fc = res[0], res[1]
        if final:
            y_p = res[2]
        state_p.append(_pairs_to_state(s_new))
        conv_p.append(fc)

        u = _in_projection(xs.reshape(b_s * t_s, d), gain1, w_pack, lb_row, tabs_s, tm_s)
        r3 = lambda a, b=b_s, t=t_s: a.reshape(b, t, a.shape[-1])

        def cat(name, new):
            return lax.dynamic_update_slice(past_kv[name][l], r3(new), (0, past, 0))

        oa = _attention_a(lam_arr, r3(u["aq"]), cat("ak", u["ak16"]), cat("av", u["av16"]),
                          q_off=past, kv_len=kv_s, tq=t_s, tk=tk)
        ob = _attention_b(r3(u["bq"]), r3(u["iq"]), r3(u["iw"]), cat("bk", u["bk16"]),
                          cat("bv", u["bv16"]), cat("ik", u["ik2"]),
                          q_off=past, kv_len=kv_s, tq=t_s, n_sel=n_sel_s)
        oc, s_new = _hgrn2(r3(u["cq"]), r3(u["ck"]), r3(u["cgl"]), r3(u["cv"]), _state_to_pairs(state_c[l]),
                           c=t_s)
        res = _merge_ffn(scal, xs, oa, ob, oc, r3(u["cg"]), state_ffn_conv[l].astype(F32), a_gain, c_gain, wo,
                         gain2, wup, cw, cb, wdn, fnorm, tm=t_s, final=final)
        xs, fc = res[0], res[1]
        if final:
            y_s = res[2]
        for lst, val in zip(outs_s, (u["ak"].reshape(b_s, t_s, H_A, 2 * DA), u["av"].reshape(b_s, t_s, H_A, DV_A),
                                     u["bk"].reshape(b_s, t_s, H_B, D_B), u["bv"].reshape(b_s, t_s, H_B, D_B),
                                     u["ik"].reshape(b_s, t_s, D_I), _pairs_to_state(s_new), fc)):
            lst.append(val)

    def frames_major(c, heads):
        c = c.reshape(depth, b_p, heads, c.shape[2] // heads, t_p)
        return jnp.transpose(c, (0, 1, 4, 2, 3))

    outs_p = (frames_major(caches_p["ak"], H_A), frames_major(caches_p["av"], H_A),
              frames_major(caches_p["bk"], H_B), frames_major(caches_p["bv"], H_B),
              jnp.swapaxes(caches_p["ik"], 2, 3), jnp.stack(state_p), jnp.stack(conv_p))
    return (y_p, y_s) + outs_p + tuple(jnp.stack(v) for v in outs_s)
```

```python
import functools
import math

import jax
import jax.numpy as jnp
import numpy as np
from jax import lax
from jax.experimental import pallas as pl
from jax.experimental.pallas import tpu as pltpu

F32 = jnp.float32
BF16 = jnp.bfloat16

CHUNK = 64
ROPE_THETA = 10000.0
EPS = 1e-6
NEG_BIG = -1e30
LB_FLOOR = 1e-20
H_A, DA, DV_A = 6, 32, 64
H_B, D_B = 5, 64
H_I, D_I = 4, 64
TOPK_MAX = 256
H_C, DK_C, DV_C = 5, 64, 64
CONV_W = 3

LANES = 128
SUBLANES = 8
VMEM_LIMIT = 56 * 1024 * 1024

WA = H_A * 2 * DA
WB = H_B * D_B
WBP = 384
WIQ = H_I * D_I
NPAIR = WBP // LANES

SEGS = (("aq", WA, WA), ("ak", WA, WA), ("av", WA, WA),
        ("bq", WB, WBP), ("bk", WB, WBP), ("bv", WB, WBP),
        ("iq", WIQ, WIQ), ("ik", D_I, 2 * D_I), ("iw", H_I, LANES),
        ("cq", WB, WBP), ("cf", WB, WBP), ("ci", WB, WBP), ("cg", WB, WBP))
SEG_OFF = {}
_o = 0
for _n, _w, _p in SEGS:
    SEG_OFF[_n] = (_o, _p)
    _o += _p
W_PACK = _o
PROJ_GROUPS = tuple((SEG_OFF[first][0], SEG_OFF[last][0] + SEG_OFF[last][1] - SEG_OFF[first][0])
                    for first, last in (("aq", "ak"), ("av", "bq"), ("bk", "bv"), ("iq", "iw"),
                                        ("cq", "cf"), ("ci", "cg")))
IN_SIZES = (WA, WA, WA, WB, WB, WB, WIQ, D_I, H_I, WB, WB, WB, WB)

LOG2E = math.log2(math.e)
NT_DIMS = (((1,), (1,)), ((), ()))


def _cparams(n_axes):
    return pltpu.CompilerParams(dimension_semantics=("arbitrary",) * n_axes,
                                vmem_limit_bytes=VMEM_LIMIT)


def _const_spec(shape):
    nd = len(shape)
    return pl.BlockSpec(shape, lambda *_: (0,) * nd, pipeline_mode=pl.Buffered(1))


def _dot(a, b):
    return jnp.dot(a, b, preferred_element_type=F32)


def _dot_nt(a, b):
    return lax.dot_general(a, b, NT_DIMS, preferred_element_type=F32)


def _rope(u, cos, sin, half):
    w = u.shape[-1]
    lane = lax.broadcasted_iota(jnp.int32, u.shape, 1)
    first = (lane % (2 * half)) < half
    rot = jnp.where(first, -pltpu.roll(u, w - half, 1), pltpu.roll(u, half, 1))
    return u * cos + rot * sin


N_PROJ_INPUTS = 8


def _inproj_kernel(*refs, n_aliased, feature_major_caches):
    x_ref, g_ref, w_ref, lb_ref, ca_ref, sa_ref, cb_ref, sb_ref = refs[:N_PROJ_INPUTS]
    (aq_ref, ak_ref, av_ref, ak16_ref, av16_ref, bq_ref, bk_ref, bv_ref, bk16_ref, bv16_ref,
     iq_ref, ik2_ref, ik_ref, iw_ref, cq_ref, ck_ref, cgl_ref, cv_ref, cg_ref) = refs[N_PROJ_INPUTS + n_aliased:]

    def put_cache(ref, val, width):
        ref[...] = val.T[:width, :] if feature_major_caches else val[:, :width]

    x = x_ref[...]
    ms = jnp.mean(x * x, axis=-1, keepdims=True)
    xn = (x * lax.rsqrt(ms + EPS) * g_ref[...]).astype(BF16)

    group_dots = {}

    def seg(name):
        off, width = SEG_OFF[name]
        g_off, g_width = next((o, wd) for o, wd in PROJ_GROUPS if o <= off < o + wd)
        if g_off not in group_dots:
            group_dots[g_off] = _dot(xn, w_ref[:, g_off:g_off + g_width])
        return group_dots[g_off][:, off - g_off:off - g_off + width]

    ca, sa = ca_ref[...], sa_ref[...]
    cb, sb = cb_ref[...], sb_ref[...]
    aq_ref[...] = (_rope(seg("aq"), ca, sa, DA // 2) * (DA ** -0.5 * LOG2E)).astype(BF16)
    ak = _rope(seg("ak"), ca, sa, DA // 2)
    put_cache(ak_ref, ak, WA)
    ak16_ref[...] = ak.astype(BF16)
    av = seg("av")
    put_cache(av_ref, av, WA)
    av16_ref[...] = av.astype(BF16)
    bq_ref[...] = (_rope(seg("bq"), cb, sb, D_B // 2) * (D_B ** -0.5 * LOG2E)).astype(BF16)
    bk = _rope(seg("bk"), cb, sb, D_B // 2)
    put_cache(bk_ref, bk, WB)
    bk16_ref[...] = bk.astype(BF16)
    bv = seg("bv")
    put_cache(bv_ref, bv, WB)
    bv16_ref[...] = bv.astype(BF16)
    iq_ref[...] = (_rope(seg("iq"), cb[:, :WIQ], sb[:, :WIQ], D_I // 2) * (D_I ** -0.5)).astype(BF16)
    ik2 = _rope(seg("ik"), cb[:, :2 * D_I], sb[:, :2 * D_I], D_I // 2)
    ik2_ref[...] = ik2.astype(BF16)
    put_cache(ik_ref, ik2, D_I)
    iw_ref[...] = seg("iw") * (H_I ** -0.5)

    cq = seg("cq")
    cq_ref[...] = cq * jax.nn.sigmoid(cq)
    z = seg("cf")
    lb = lb_ref[...]
    la = jnp.log(jnp.maximum(lb, LB_FLOOR))
    lsig = jnp.minimum(z, 0.0) - jnp.log1p(jnp.exp(-jnp.abs(z)))
    bb = jnp.log1p(-lb) + lsig
    cgl_ref[...] = jnp.maximum(la, bb) + jnp.log1p(jnp.exp(-jnp.abs(la - bb)))
    ck_ref[...] = (1.0 - lb) * jax.nn.sigmoid(-z)
    cv_ref[...] = seg("ci")
    cg_ref[...] = seg("cg")


CACHE_OUTS = ("ak", "av", "bk", "bv", "ik")


def _in_projection(x2d, gain, w_pack, lb_row, tabs, tm, stacked=None):
    n = x2d.shape[0]
    ca, sa, cb, sb = tabs
    period = ca.shape[0] // tm
    row = lambda w: pl.BlockSpec((tm, w), lambda i: (i, 0))
    tab = lambda w: pl.BlockSpec((tm, w), lambda i: (i % period, 0))
    outs = (("aq", WA, BF16), ("ak", WA, F32), ("av", WA, F32), ("ak16", WA, BF16), ("av16", WA, BF16),
            ("bq", WBP, BF16), ("bk", WB, F32), ("bv", WB, F32), ("bk16", WBP, BF16), ("bv16", WBP, BF16),
            ("iq", WIQ, BF16), ("ik2", 2 * D_I, BF16), ("ik", D_I, F32), ("iw", LANES, F32),
            ("cq", WBP, F32), ("ck", WBP, F32), ("cgl", WBP, F32), ("cv", WBP, F32), ("cg", WBP, F32))
    out_specs = {name: row(w) for name, w, _ in outs}
    out_shape = {name: jax.ShapeDtypeStruct((n, w), dt) for name, w, dt in outs}
    inputs = [x2d, gain, w_pack, lb_row, ca, sa, cb, sb]
    in_specs = [row(x2d.shape[1]), _const_spec(gain.shape), _const_spec(w_pack.shape),
                _const_spec(lb_row.shape), tab(WA), tab(WA), tab(WBP), tab(WBP)]
    aliases = {}
    if stacked is not None:
        layer, depth, batch, caches = stacked
        t = n // batch
        tiles = t // tm
        for name, w, _ in outs:
            if name in CACHE_OUTS:
                out_specs[name] = pl.BlockSpec((None, None, w, tm),
                                               lambda i, layer=layer: (layer, i // tiles, 0, i % tiles))
                out_shape[name] = jax.ShapeDtypeStruct((depth, batch, w, t), F32)
        if caches is not None:
            names = [o[0] for o in outs]
            for name in CACHE_OUTS:
                aliases[len(inputs)] = names.index(name)
                inputs.append(caches[name])
                in_specs.append(pl.BlockSpec(memory_space=pl.ANY))
    kern = functools.partial(_inproj_kernel, n_aliased=len(aliases), feature_major_caches=stacked is not None)
    res = pl.pallas_call(
        kern,
        grid=(n // tm,),
        in_specs=in_specs,
        out_specs=[out_specs[o[0]] for o in outs],
        out_shape=[out_shape[o[0]] for o in outs],
        input_output_aliases=aliases,
        compiler_params=_cparams(1),
        name="in_projection",
    )(*inputs)
    return dict(zip([o[0] for o in outs], res))


NARROW_KEY_BLOCK = 256


def _attn_a_kernel(lam_ref, q_ref, k_ref, v_ref, o_ref, q4_ref, m_ref, acc_ref, *, tq, tk, q_off, kv_len):
    qi = pl.program_id(2)
    q = q_ref[...]
    lane = lax.broadcasted_iota(jnp.int32, q.shape, 1)
    for i in range(4):
        q4_ref[i * tq:(i + 1) * tq, :] = jnp.where(lane // DA == i, q, jnp.zeros_like(q))
    m_ref[...] = jnp.full(m_ref.shape, NEG_BIG, F32)
    acc_ref[...] = jnp.zeros(acc_ref.shape, F32)

    q_first = q_off + qi * tq
    q_last = q_first + tq - 1
    n_full = jnp.minimum((q_first // CHUNK + 1) * CHUNK, kv_len) // tk
    lim = jnp.minimum((q_last // CHUNK + 1) * CHUNK, kv_len)
    tkn = min(tk, NARROW_KEY_BLOCK)
    n_narrow = (lim + tkn - 1) // tkn
    n_wide = n_full + (lim - n_full * tk) // tk if tkn < tk else n_narrow

    def step(j, tk, masked):
        start = pl.multiple_of(j * tk, tk)
        kb = k_ref[pl.ds(start, tk), :]
        vb = v_ref[pl.ds(start, tk), :]
        s = _dot_nt(q4_ref[...], kb)
        if masked:
            kpos = start + lax.broadcasted_iota(jnp.int32, (tq, tk), 1)
            qpos = q_first + lax.broadcasted_iota(jnp.int32, (tq, tk), 0)
            valid = (kpos // CHUNK <= qpos // CHUNK) & (kpos < kv_len)
            s = jnp.where(valid[None], s.reshape(4, tq, tk), NEG_BIG).reshape(4 * tq, tk)
        m_old = m_ref[...]
        m_new = jnp.maximum(m_old, jnp.max(s, axis=-1, keepdims=True))
        alpha = jnp.exp2(m_old - m_new)
        p = jnp.exp2(s - jnp.concatenate([m_new] * (tk // LANES), axis=1)).astype(BF16)
        lane_v = lax.broadcasted_iota(jnp.int32, vb.shape, 1)
        ones = jnp.ones_like(vb)
        pv = jnp.concatenate([_dot(p[0:2 * tq], jnp.where(lane_v < DV_A, vb, ones)),
                              _dot(p[2 * tq:4 * tq], jnp.where(lane_v < DV_A, ones, vb))], axis=0)
        acc_ref[...] = alpha * acc_ref[...] + pv
        m_ref[...] = m_new

    def body(tk, masked):
        def run(j, c):
            step(j, tk, masked)
            return c
        return run

    lax.fori_loop(0, n_full, body(tk, False), 0)
    lax.fori_loop(n_full, n_wide, body(tk, True), 0)
    if tkn < tk:
        lax.fori_loop(n_wide * (tk // tkn), n_narrow, body(tkn, True), 0)

    lam = lam_ref[0]
    acc0 = acc_ref[0:2 * tq, :]
    acc1 = acc_ref[2 * tq:4 * tq, :]
    on0 = acc0 / acc0[:, DV_A:DV_A + 1]
    on1 = acc1 / acc1[:, 0:1]
    o0 = on0[0:tq] - lam * on0[tq:2 * tq]
    o1 = on1[0:tq] - lam * on1[tq:2 * tq]
    o_ref[...] = jnp.where(lane < DV_A, o0, o1)


def _attention_a(lam, q, k, v, *, q_off, kv_len, tq, tk):
    b, t, _ = q.shape
    lk = k.shape[1]
    kern = functools.partial(_attn_a_kernel, tq=tq, tk=tk, q_off=q_off, kv_len=kv_len)
    return pl.pallas_call(
        kern,
        grid=(b, WA // LANES, t // tq),
        in_specs=[pl.BlockSpec(memory_space=pltpu.SMEM),
                  pl.BlockSpec((None, tq, LANES), lambda bi, hi, qi: (bi, qi, hi)),
                  pl.BlockSpec((None, lk, LANES), lambda bi, hi, qi: (bi, 0, hi)),
                  pl.BlockSpec((None, lk, LANES), lambda bi, hi, qi: (bi, 0, hi))],
        out_specs=pl.BlockSpec((None, tq, LANES), lambda bi, hi, qi: (bi, qi, hi)),
        out_shape=jax.ShapeDtypeStruct((b, t, WA), F32),
        scratch_shapes=[pltpu.VMEM((4 * tq, LANES), BF16), pltpu.VMEM((4 * tq, LANES), F32),
                        pltpu.VMEM((4 * tq, LANES), F32)],
        compiler_params=_cparams(3),
        name="mixer_a",
    )(lam, q, k, v)


TIE_BLOCK = 256
SEARCH_BITS_PER_TRIP = 3


def _attn_b_kernel(q_ref, iq_ref, iw_ref, k_ref, v_ref, ik_ref, o_ref, key_ref, bias_ref,
                   *, tq, q_off, kv_len, n_sel, variants):
    q_first = q_off + pl.program_id(1) * tq
    need_keys = jnp.minimum(((q_first + tq - 1) // CHUNK + 1) * CHUNK, kv_len)
    lo = 0
    for lk in variants:
        @pl.when((need_keys > lo) & (need_keys <= lk))
        def _(lk=lk):
            _attn_b_body(q_ref, iq_ref, iw_ref, k_ref, v_ref, ik_ref, o_ref, key_ref, bias_ref,
                         tq=tq, q_first=q_first, need_keys=need_keys, kv_len=kv_len, n_sel=n_sel, lk=lk)
        lo = lk


def _stack_heads(x, n_heads, width):
    lane = lax.broadcasted_iota(jnp.int32, x.shape, 1)
    return jnp.concatenate([jnp.where(lane // width == h, x, jnp.zeros_like(x)) for h in range(n_heads)],
                           axis=0)


def _attn_b_body(q_ref, iq_ref, iw_ref, k_ref, v_ref, ik_ref, o_ref, key_ref, bias_ref,
                 *, tq, q_first, need_keys, kv_len, n_sel, lk):
    ik2 = ik_ref[0:lk, :]
    iq = iq_ref[...]
    iw = iw_ref[...]

    score = jnp.zeros((tq, lk), F32)
    for pi in range(H_I // 2):
        y = _stack_heads(iq[:, LANES * pi:LANES * (pi + 1)], 2, D_I)
        d = jnp.maximum(_dot_nt(y, ik2), 0.0)
        score = score + iw[:, 2 * pi:2 * pi + 1] * d[0:tq] + iw[:, 2 * pi + 1:2 * pi + 2] * d[tq:2 * tq]

    kpos = lax.broadcasted_iota(jnp.int32, (tq, lk), 1)
    qpos = q_first + lax.broadcasted_iota(jnp.int32, (tq, lk), 0)
    valid = (kpos // CHUNK <= qpos // CHUNK) & (kpos < kv_len)
    score = jnp.where(valid, score, NEG_BIG)

    bits = lax.bitcast_convert_type(score, jnp.int32)
    key = bits ^ ((bits >> 31) & jnp.int32(0x7FFFFFFF))
    key = jnp.where(key == -1, 0, key)
    key_ref[:, 0:lk] = key

    kf = float(n_sel)

    def count_ge(cand):
        wide = jnp.concatenate([cand] * (lk // LANES), axis=1)
        return jnp.sum(jnp.where(key_ref[:, 0:lk] >= wide, 1.0, 0.0), axis=-1, keepdims=True)

    int_min = jnp.int32(-2 ** 31)
    zero = jnp.zeros((tq, LANES), jnp.int32)
    select_all = need_keys <= n_sel
    cur = jnp.where(select_all | (count_ge(zero) < kf), int_min, zero)

    def try_bit(cur, bit):
        cand = cur | bit
        return jnp.where(count_ge(cand) >= kf, cand, cur)

    cur = jnp.where(select_all, int_min, try_bit(cur, jnp.int32(1 << 30)))

    def bits_body(i, cur):
        for j in range(SEARCH_BITS_PER_TRIP):
            cur = try_bit(cur, jnp.left_shift(jnp.int32(1), 29 - (SEARCH_BITS_PER_TRIP * i + j)))
        return cur

    thr = lax.fori_loop(0, jnp.where(select_all, 0, 30 // SEARCH_BITS_PER_TRIP), bits_body, cur)

    key = key_ref[:, 0:lk]
    thr = jnp.concatenate([thr] * (lk // LANES), axis=1)
    gt = key > thr
    eq = key == thr
    need = kf - jnp.sum(jnp.where(gt, 1.0, 0.0), axis=-1, keepdims=True)
    r_i = lax.broadcasted_iota(jnp.int32, (TIE_BLOCK, TIE_BLOCK), 0)
    c_i = lax.broadcasted_iota(jnp.int32, (TIE_BLOCK, TIE_BLOCK), 1)
    tri = jnp.where(r_i <= c_i, 1.0, 0.0).astype(BF16)
    carry = jnp.zeros((tq, 1), F32)
    for jb in range(lk // TIE_BLOCK):
        sl = slice(jb * TIE_BLOCK, (jb + 1) * TIE_BLOCK)
        eq_b = eq[:, sl]
        pref = _dot(jnp.where(eq_b, 1.0, 0.0).astype(BF16), tri) + carry
        carry = pref[:, TIE_BLOCK - 1:TIE_BLOCK]
        sel = (gt[:, sl] | (eq_b & (pref <= need))) & valid[:, sl]
        bias_ref[:, sl] = jnp.where(sel, 0.0, NEG_BIG)

    q = q_ref[...]
    lane = lax.broadcasted_iota(jnp.int32, (tq, LANES), 1)
    bias = bias_ref[:, 0:lk]
    for p in range(NPAIR):
        n_heads = min(2, H_B - 2 * p)
        sl = slice(p * LANES, (p + 1) * LANES)
        y = _stack_heads(q[:, sl], n_heads, D_B)
        s = _dot_nt(y, k_ref[0:lk, sl]).reshape(n_heads, tq, lk) + bias[None]
        m = jnp.max(s, axis=-1, keepdims=True)
        pr = jnp.exp2(s - m)
        l = jnp.sum(pr, axis=-1, keepdims=True)
        o = _dot(pr.reshape(n_heads * tq, lk).astype(BF16), v_ref[0:lk, sl]) / l.reshape(n_heads * tq, 1)
        o_ref[:, sl] = o if n_heads == 1 else jnp.where(lane < D_B, o[0:tq], o[tq:2 * tq])


def _attention_b(q, iq, iw, k, v, ik2, *, q_off, kv_len, tq, n_sel):
    b, t, _ = q.shape
    lk = k.shape[1]
    granule = tq if (tq % TIE_BLOCK == 0 and lk % (2 * tq) == 0) else lk
    variants = tuple(v for v in range(granule, lk + 1, granule) if v <= lk // 2 or v % (2 * granule) == 0 or v == lk)
    kern = functools.partial(_attn_b_kernel, tq=tq, q_off=q_off, kv_len=kv_len, n_sel=n_sel,
                             variants=variants)
    qspec = lambda w: pl.BlockSpec((None, tq, w), lambda bi, qi: (bi, qi, 0))
    kspec = lambda w: pl.BlockSpec((None, lk, w), lambda bi, qi: (bi, 0, 0))
    return pl.pallas_call(
        kern,
        grid=(b, t // tq),
        in_specs=[qspec(WBP), qspec(WIQ), qspec(LANES), kspec(WBP), kspec(WBP), kspec(2 * D_I)],
        out_specs=qspec(WBP),
        out_shape=jax.ShapeDtypeStruct((b, t, WBP), F32),
        scratch_shapes=[pltpu.VMEM((tq, lk), jnp.int32), pltpu.VMEM((tq, lk), F32)],
        compiler_params=_cparams(2),
        name="mixer_b",
    )(q, iq, iw, k, v, ik2)


HGRN2_CHUNKS_PER_STEP = 4
GROUP = SUBLANES
DECAY_SPLIT_LIMIT = 60.0


def _hgrn2_kernel(q_ref, k_ref, g_ref, v_ref, s0_ref, o_ref, s_out_ref, st_ref, *, c, chunks):
    ci = pl.program_id(1)

    @pl.when(ci == 0)
    def _():
        st_ref[...] = s0_ref[...]

    views = [tuple(r.at[pl.ds(i * c, c)] for r in (q_ref, k_ref, g_ref, v_ref, o_ref)) for i in range(chunks)]
    bound = functools.reduce(jnp.maximum, [_half_chunk_bound(view[2][...], c) for view in views])
    single_split = bound <= DECAY_SPLIT_LIMIT

    @pl.when(single_split)
    def _():
        for view in views:
            _hgrn2_chunk(*view, st_ref, c=c, single_split=True)

    @pl.when(jnp.logical_not(single_split))
    def _():
        for view in views:
            _hgrn2_chunk(*view, st_ref, c=c, single_split=False)

    @pl.when(ci == pl.num_programs(1) - 1)
    def _():
        s_out_ref[...] = st_ref[...]


def _half_chunk_bound(g, c):
    ag = jnp.abs(g)
    return jnp.maximum(jnp.max(jnp.sum(ag[0:c // 2], axis=0, keepdims=True)),
                       jnp.max(jnp.sum(ag[c // 2:c], axis=0, keepdims=True)))


def _hgrn2_chunk(q_ref, k_ref, g_ref, v_ref, o_ref, st_ref, *, c, single_split):
    q = q_ref[...]
    k = k_ref[...]
    g = g_ref[...]
    v = v_ref[...]
    w = q.shape[-1]
    row = lax.broadcasted_iota(jnp.int32, (c, w), 0)
    lane = lax.broadcasted_iota(jnp.int32, (c, w), 1)

    def segment_scans():
        cs, tots = {1: g}, {1: g}
        cum, tot, m = g, g, 1
        while m < c:
            upper = (row // m) % 2 == 1
            prev_tot = pltpu.roll(tot, m, 0)
            next_tot = pltpu.roll(tot, c - m, 0)
            cum = cum + jnp.where(upper, prev_tot, 0.0)
            tot = tot + jnp.where(upper, prev_tot, next_tot)
            m *= 2
            cs[m], tots[m] = cum, tot
        return cs, tots

    def cumsum_rows():
        cum, m = g, 1
        while m < c:
            cum = cum + jnp.where(row >= m, pltpu.roll(cum, m, 0), 0.0)
            m *= 2
        return cum

    head_masks = [lane // DK_C == h for h in range(w // DK_C)]
    n_heads = H_C
    rq = lax.broadcasted_iota(jnp.int32, (c, c), 0)
    rk = lax.broadcasted_iota(jnp.int32, (c, c), 1)

    def attend(att):
        res = _dot(att.astype(BF16), v.astype(BF16))
        out = jnp.zeros((c, w), F32)
        for h in range(n_heads):
            out = out + jnp.where(head_masks[h], res[h * c:(h + 1) * c], 0.0)
        return out

    def stack_heads(x):
        return jnp.concatenate([jnp.where(head_masks[h], x, 0.0) for h in range(n_heads)], axis=0)

    def finish(o_intra, bcum, blast):
        qe = (q * jnp.exp(bcum)).astype(BF16)
        k2 = (k * jnp.exp(blast - bcum)).astype(BF16)
        decay = jnp.exp(blast[0:1, :])
        pr = lax.broadcasted_iota(jnp.int32, (LANES, LANES), 0)
        pc = lax.broadcasted_iota(jnp.int32, (LANES, LANES), 1)
        diag = pr // DK_C == pc // DK_C
        o_state = []
        for p in range(w // LANES):
            sl = slice(p * LANES, (p + 1) * LANES)
            st = st_ref[p]
            o_state.append(_dot_nt(qe[:, sl], st.astype(BF16)))
            upd = _dot(v[:, sl].T.astype(BF16), k2[:, sl])
            st_ref[p] = st * decay[:, sl] + jnp.where(diag, upd, 0.0)
        o_ref[...] = o_intra + jnp.concatenate(o_state, axis=-1)

    if single_split:
        bcum = cumsum_rows()
        dmid = bcum - bcum[c // 2 - 1:c // 2, :]
        qt = stack_heads(q * jnp.exp(dmid)).astype(BF16)
        kt = (k * jnp.exp(-dmid)).astype(BF16)
        causal = jnp.concatenate([rq >= rk] * n_heads, axis=0)
        o_intra = attend(jnp.where(causal, _dot_nt(qt, kt), 0.0))
        finish(o_intra, bcum, jnp.broadcast_to(bcum[c - 1:c, :], (c, w)))

    else:
        cs, tots = segment_scans()
        att = jnp.zeros((n_heads * c, c), F32)
        half = GROUP
        while half < c:
            upper = (row // half) % 2 == 1
            qt = jnp.where(upper, q * jnp.exp(cs[half]), 0.0)
            kt = jnp.where(upper, 0.0, k * jnp.exp(tots[half] - cs[half])).astype(BF16)
            blk = _dot_nt(stack_heads(qt).astype(BF16), kt)
            same = (rq // (2 * half)) == (rk // (2 * half))
            same = jnp.concatenate([same] * n_heads, axis=0)
            att = att + jnp.where(same, blk, 0.0)
            half *= 2
        out = attend(att)

        def group_row(x, j):
            x3 = x.reshape(c // GROUP, GROUP, w)
            return jnp.broadcast_to(x3[:, j:j + 1, :], x3.shape).reshape(c, w)

        c8 = cs[GROUP]
        vals = []
        for j in range(GROUP):
            ok = (row % GROUP) >= j
            e = jnp.where(ok, c8 - group_row(c8, j), 0.0)
            vals.append(jnp.where(ok, q * group_row(k, j) * jnp.exp(e), 0.0))
        r_i = lax.broadcasted_iota(jnp.int32, (w, w), 0)
        c_i = lax.broadcasted_iota(jnp.int32, (w, w), 1)
        head_sum = jnp.where(r_i // DK_C == c_i // DK_C, 1.0, 0.0).astype(BF16)
        wts = _dot(jnp.concatenate(vals, axis=0).astype(BF16), head_sum)
        for j in range(GROUP):
            out = out + wts[j * c:(j + 1) * c] * group_row(v, j)
        finish(out, cs[c], tots[c])


def _hgrn2(q, k, g, v, s0, *, c):
    b, t, w = q.shape
    chunks = HGRN2_CHUNKS_PER_STEP if t % (HGRN2_CHUNKS_PER_STEP * c) == 0 else 1
    kern = functools.partial(_hgrn2_kernel, c=c, chunks=chunks)
    blk = pl.BlockSpec((None, chunks * c, w), lambda bi, ci: (bi, ci, 0))
    sblk = pl.BlockSpec((None, w // LANES, LANES, LANES), lambda bi, ci: (bi, 0, 0, 0))
    return pl.pallas_call(
        kern,
        grid=(b, t // (chunks * c)),
        in_specs=[blk, blk, blk, blk, sblk],
        out_specs=[blk, sblk],
        out_shape=[jax.ShapeDtypeStruct((b, t, w), F32),
                   jax.ShapeDtypeStruct((b, w // LANES, LANES, LANES), F32)],
        scratch_shapes=[pltpu.VMEM((w // LANES, LANES, LANES), F32)],
        compiler_params=_cparams(2),
        name="mixer_c",
    )(q, k, g, v, s0)


def _state_to_pairs(s):
    b = s.shape[0]
    st = jnp.swapaxes(s.astype(F32), -1, -2)
    st = jnp.pad(st, ((0, 0), (0, 2 * NPAIR - H_C), (0, 0), (0, 0)))
    st = st.reshape(b, NPAIR, 2, DV_C, DK_C)
    eye = jnp.eye(2, dtype=F32)
    full = st[:, :, :, :, None, :] * eye[None, None, :, None, :, None]
    return full.reshape(b, NPAIR, 2 * DV_C, 2 * DK_C)


def _pairs_to_state(sp):
    b = sp.shape[0]
    s6 = sp.reshape(b, NPAIR, 2, DV_C, 2, DK_C)
    diag = jnp.stack([s6[:, :, a, :, a, :] for a in range(2)], axis=2)
    return jnp.swapaxes(diag.reshape(b, 2 * NPAIR, DV_C, DK_C)[:, :H_C], -1, -2)


FF_BLOCK = 2816


def _head_norm(y, gain):
    w = y.shape[-1]
    r_i = lax.broadcasted_iota(jnp.int32, (w, w), 0)
    c_i = lax.broadcasted_iota(jnp.int32, (w, w), 1)
    head_sum = jnp.where(r_i // DV_A == c_i // DV_A, 1.0, 0.0).astype(BF16)
    y2 = y * y
    hi = y2.astype(BF16)
    lo = (y2 - hi.astype(F32)).astype(BF16)
    ms = (_dot(hi, head_sum) + _dot(lo, head_sum)) * (1.0 / DV_A)
    return y * lax.rsqrt(ms + EPS) * gain


def _merge_kernel(scal_ref, x_ref, oa_ref, ob_ref, oc_ref, cg_ref, hist_ref, ag_ref, cgn_ref, wo_ref,
                  n2_ref, wup_ref, cw_ref, cb_ref, wdn_ref, fn_ref, *out_and_scratch, d_ff, final):
    if final:
        x_out_ref, fc_ref, y_ref, carry_ref = out_and_scratch
    else:
        x_out_ref, fc_ref, carry_ref = out_and_scratch
    ti = pl.program_id(1)
    tm = x_ref.shape[0]

    @pl.when(ti == 0)
    def _():
        carry_ref[...] = hist_ref[...]

    oa = _head_norm(oa_ref[...], ag_ref[...]) * scal_ref[0]
    cg = cg_ref[...]
    oc = _head_norm(oc_ref[...], cgn_ref[...]) * (cg * jax.nn.sigmoid(cg))
    mixed = (_dot(oa.astype(BF16), wo_ref[0:WA, :])
             + _dot(ob_ref[...].astype(BF16), wo_ref[WA:WA + WBP, :])
             + _dot(oc.astype(BF16), wo_ref[WA + WBP:WA + 2 * WBP, :]))
    x = x_ref[...] + mixed

    ms = jnp.mean(x * x, axis=-1, keepdims=True)
    h = (x * lax.rsqrt(ms + EPS) * n2_ref[...]).astype(BF16)
    row = lax.broadcasted_iota(jnp.int32, (tm, FF_BLOCK), 0)
    acc = jnp.zeros(x.shape, F32)
    for cblk in range(d_ff // FF_BLOCK):
        sl = slice(cblk * FF_BLOCK, (cblk + 1) * FF_BLOCK)
        a = _dot(h, wup_ref[:, sl])
        gate = _dot(h, wup_ref[:, d_ff + cblk * FF_BLOCK:d_ff + (cblk + 1) * FF_BLOCK])
        prev2 = carry_ref[0:1, sl]
        prev1 = carry_ref[1:2, sl]
        a1 = jnp.where(row == 0, prev1, pltpu.roll(a, 1, 0))
        a2 = jnp.where(row == 0, prev2, jnp.where(row == 1, prev1, pltpu.roll(a, 2, 0)))
        conv = cb_ref[:, sl] + a2 * cw_ref[0:1, sl] + a1 * cw_ref[1:2, sl] + a * cw_ref[2:3, sl]
        act = conv * jax.nn.sigmoid(conv) * gate
        acc = acc + _dot(act.astype(BF16), wdn_ref[sl, :])
        carry_ref[:, sl] = a[tm - (CONV_W - 1):, :]
    x = x + acc
    x_out_ref[...] = x
    fc_ref[...] = carry_ref[...]
    if final:
        ms = jnp.mean(x * x, axis=-1, keepdims=True)
        y_ref[...] = x * lax.rsqrt(ms + EPS) * fn_ref[...]


def _merge_ffn(scal, x, oa, ob, oc, cg, hist, a_gain, c_gain, wo, n2, wup, cw, cb, wdn, fnorm, *, tm, final):
    b, t, d = x.shape
    d_ff = wdn.shape[0]
    kern = functools.partial(_merge_kernel, d_ff=d_ff, final=final)
    blk = lambda w: pl.BlockSpec((None, tm, w), lambda bi, ti: (bi, ti, 0))
    per_b = pl.BlockSpec((None, CONV_W - 1, d_ff), lambda bi, ti: (bi, 0, 0))
    out_specs = [blk(d), per_b]
    out_shape = [jax.ShapeDtypeStruct((b, t, d), F32), jax.ShapeDtypeStruct((b, CONV_W - 1, d_ff), F32)]
    if final:
        out_specs.append(blk(d))
        out_shape.append(jax.ShapeDtypeStruct((b, t, d), F32))
    return pl.pallas_call(
        kern,
        grid=(b, t // tm),
        in_specs=[pl.BlockSpec(memory_space=pltpu.SMEM), blk(d), blk(WA), blk(WBP), blk(WBP), blk(WBP), per_b,
                  _const_spec(a_gain.shape), _const_spec(c_gain.shape), _const_spec(wo.shape),
                  _const_spec(n2.shape), _const_spec(wup.shape), _const_spec(cw.shape),
                  _const_spec(cb.shape), _const_spec(wdn.shape), _const_spec(fnorm.shape)],
        out_specs=out_specs,
        out_shape=out_shape,
        scratch_shapes=[pltpu.VMEM((CONV_W - 1, d_ff), F32)],
        compiler_params=_cparams(2),
        name="merge_ffn",
    )(scal, x, oa, ob, oc, cg, hist, a_gain, c_gain, wo, n2, wup, cw, cb, wdn, fnorm)


def _pack_w_in(w):
    parts = jnp.split(w, np.cumsum(IN_SIZES)[:-1].tolist(), axis=-1)
    cols = []
    for (name, width, padded), part in zip(SEGS, parts):
        if name == "ik":
            part = jnp.concatenate([part, part], axis=-1)
            width = 2 * D_I
        cols.append(jnp.pad(part, ((0, 0), (0, padded - width))))
    return jnp.concatenate(cols, axis=-1).astype(BF16)


def _pack_w_out(w):
    wa, wb, wc = w[:WA], w[WA:WA + WB], w[WA + WB:]
    pad = lambda m: jnp.pad(m, ((0, WBP - WB), (0, 0)))
    return jnp.concatenate([wa, pad(wb), pad(wc)], axis=0).astype(BF16)


def _rope_table(pos, dim, width):
    inv_freq = ROPE_THETA ** (-jnp.arange(0, dim, 2, dtype=F32) / dim)
    ang = pos.astype(F32)[:, None] * inv_freq[None, :]
    reps = width // (dim // 2)
    return jnp.tile(jnp.cos(ang), (1, reps)), jnp.tile(jnp.sin(ang), (1, reps))


def _cache_prep_kernel(c_ref, o_ref, *, past):
    x = c_ref[...]
    n_feat = x.shape[0]
    if n_feat < LANES:
        x = jnp.concatenate([x] * (LANES // n_feat), axis=0)
    elif n_feat % LANES:
        x = jnp.concatenate([x, jnp.zeros((LANES - n_feat % LANES, past), x.dtype)], axis=0)
    for j in range(x.shape[0] // LANES):
        o_ref[0:past, j * LANES:(j + 1) * LANES] = x[j * LANES:(j + 1) * LANES].T.astype(BF16)
    o_ref[past:, :] = jnp.zeros((o_ref.shape[0] - past, o_ref.shape[1]), BF16)


def _cache_prep(cache, rows):
    depth, batch, past = cache.shape[:3]
    n_feat = int(np.prod(cache.shape[3:]))
    nd = cache.ndim
    c = jnp.transpose(cache, (0, 1) + tuple(range(3, nd)) + (2,)).reshape(depth * batch, n_feat, past)
    width = _round_up(n_feat, LANES)
    out = pl.pallas_call(
        functools.partial(_cache_prep_kernel, past=past),
        grid=(depth * batch,),
        in_specs=[pl.BlockSpec((None, n_feat, past), lambda i: (i, 0, 0))],
        out_specs=pl.BlockSpec((None, rows, width), lambda i: (i, 0, 0)),
        out_shape=jax.ShapeDtypeStruct((depth * batch, rows, width), BF16),
        compiler_params=_cparams(1),
        name="cache_prep",
    )(c)
    return out.reshape(depth, batch, rows, width)


def _round_up(n, m):
    return (n + m - 1) // m * m


def kernel(x_prompt, x_sample, cache_a_k, cache_a_v, cache_b_k, cache_b_v, cache_b_kidx, state_c, state_ffn_conv, norm1, w_in, lam_q1, lam_k1, lam_q2, lam_k2, a_norm, c_lower, c_norm, w_out, norm2, ffn_up, ffn_conv_w, ffn_conv_b, ffn_down, final_norm):
    depth = w_in.shape[0]
    b_p, t_p, d = x_prompt.shape
    b_s, t_s, _ = x_sample.shape
    past = cache_a_k.shape[2]
    d_ff = ffn_down.shape[1]
    kv_s = past + t_s
    n_sel_p = min(TOPK_MAX, t_p // 4)
    n_sel_s = min(TOPK_MAX, kv_s // 4)
    tk = 256
    lk_s = _round_up(kv_s, tk)

    lb_soft = jax.nn.softmax(c_lower.astype(F32), axis=0)
    lower = jnp.cumsum(lb_soft, axis=0) - lb_soft[0]
    lower = jnp.pad(lower, ((0, 0), (0, WBP - WB)))

    tm_p = min(512, t_p)
    tm_s = min(256, b_s * t_s)
    pos_p = jnp.arange(t_p)
    pos_s = jnp.tile(past + jnp.arange(t_s), tm_s // t_s)
    tabs_p = _rope_table(pos_p, DA, WA) + _rope_table(pos_p, D_B, WBP)
    tabs_s = _rope_table(pos_s, DA, WA) + _rope_table(pos_s, D_B, WBP)

    past_kv = {name: _cache_prep(c, lk_s) for name, c in
               (("ak", cache_a_k), ("av", cache_a_v), ("bk", cache_b_k), ("bv", cache_b_v), ("ik", cache_b_kidx))}

    xp, xs = x_prompt, x_sample
    caches_p, state_p, conv_p = None, [], []
    outs_s = [[] for _ in range(7)]
    y_p = y_s = None
    fnorm = final_norm.reshape(1, d)
    for l in range(depth):
        lam_init = 0.8 - 0.6 * math.exp(-0.3 * l)
        lam = (jnp.exp(jnp.sum(lam_q1[l].astype(F32) * lam_k1[l].astype(F32)))
               - jnp.exp(jnp.sum(lam_q2[l].astype(F32) * lam_k2[l].astype(F32))) + lam_init)
        lam_arr = lam.reshape(1).astype(F32)
        scal = jnp.full((1,), 1.0 - lam_init, F32)
        w_pack = _pack_w_in(w_in[l])
        wo = _pack_w_out(w_out[l])
        gain1 = norm1[l].reshape(1, d)
        gain2 = norm2[l].reshape(1, d)
        lb_row = lower[l].reshape(1, WBP)
        a_gain = jnp.tile(a_norm[l], H_A).reshape(1, WA)
        c_gain = jnp.pad(jnp.tile(c_norm[l], H_C), (0, WBP - WB)).reshape(1, WBP)
        wup = ffn_up[l].astype(BF16)
        wdn = ffn_down[l].astype(BF16)
        cw = ffn_conv_w[l]
        cb = ffn_conv_b[l].reshape(1, d_ff)
        final = l == depth - 1

        u = _in_projection(xp.reshape(b_p * t_p, d), gain1, w_pack, lb_row, tabs_p, tm_p,
                           stacked=(l, depth, b_p, caches_p))
        caches_p = {name: u[name] for name in CACHE_OUTS}
        r3 = lambda a, b=b_p, t=t_p: a.reshape(b, t, a.shape[-1])
        oa = _attention_a(lam_arr, r3(u["aq"]), r3(u["ak16"]), r3(u["av16"]),
                          q_off=0, kv_len=t_p, tq=min(256, t_p), tk=min(512, t_p))
        ob = _attention_b(r3(u["bq"]), r3(u["iq"]), r3(u["iw"]), r3(u["bk16"]), r3(u["bv16"]), r3(u["ik2"]),
                          q_off=0, kv_len=t_p, tq=min(256, t_p), n_sel=n_sel_p)
        s0 = jnp.zeros((b_p, NPAIR, LANES, LANES), F32)
        oc, s_new = _hgrn2(r3(u["cq"]), r3(u["ck"]), r3(u["cgl"]), r3(u["cv"]), s0, c=CHUNK)
        hist0 = jnp.zeros((b_p, CONV_W - 1, d_ff), F32)
        res = _merge_ffn(scal, xp, oa, ob, oc, r3(u["cg"]), hist0, a_gain, c_gain, wo, gain2, wup, cw, cb,
                         wdn, fnorm, tm=min(512, t_p), final=final)
        xp, fc = res[0], res[1]
        if final:
            y_p = res[2]
        state_p.append(_pairs_to_state(s_new))
        conv_p.append(fc)

        u = _in_projection(xs.reshape(b_s * t_s, d), gain1, w_pack, lb_row, tabs_s, tm_s)
        r3 = lambda a, b=b_s, t=t_s: a.reshape(b, t, a.shape[-1])

        def cat(name, new):
            return lax.dynamic_update_slice(past_kv[name][l], r3(new), (0, past, 0))

        oa = _attention_a(lam_arr, r3(u["aq"]), cat("ak", u["ak16"]), cat("av", u["av16"]),
                          q_off=past, kv_len=kv_s, tq=t_s, tk=tk)
        ob = _attention_b(r3(u["bq"]), r3(u["iq"]), r3(u["iw"]), cat("bk", u["bk16"]),
                          cat("bv", u["bv16"]), cat("ik", u["ik2"]),
                          q_off=past, kv_len=kv_s, tq=t_s, n_sel=n_sel_s)
        oc, s_new = _hgrn2(r3(u["cq"]), r3(u["ck"]), r3(u["cgl"]), r3(u["cv"]), _state_to_pairs(state_c[l]),
                           c=t_s)
        res = _merge_ffn(scal, xs, oa, ob, oc, r3(u["cg"]), state_ffn_conv[l].astype(F32), a_gain, c_gain, wo,
                         gain2, wup, cw, cb, wdn, fnorm, tm=t_s, final=final)
        xs, fc = res[0], res[1]
        if final:
            y_s = res[2]
        for lst, val in zip(outs_s, (u["ak"].reshape(b_s, t_s, H_A, 2 * DA), u["av"].reshape(b_s, t_s, H_A, DV_A),
                                     u["bk"].reshape(b_s, t_s, H_B, D_B), u["bv"].reshape(b_s, t_s, H_B, D_B),
                                     u["ik"].reshape(b_s, t_s, D_I), _pairs_to_state(s_new), fc)):
            lst.append(val)

    def frames_major(c, heads):
        c = c.reshape(depth, b_p, heads, c.shape[2] // heads, t_p)
        return jnp.transpose(c, (0, 1, 4, 2, 3))

    outs_p = (frames_major(caches_p["ak"], H_A), frames_major(caches_p["av"], H_A),
              frames_major(caches_p["bk"], H_B), frames_major(caches_p["bv"], H_B),
              jnp.swapaxes(caches_p["ik"], 2, 3), jnp.stack(state_p), jnp.stack(conv_p))
    return (y_p, y_s) + outs_p + tuple(jnp.stack(v) for v in outs_s)
```

```python
import functools
import math

import jax
import jax.numpy as jnp
import numpy as np
from jax import lax
from jax.experimental import pallas as pl
from jax.experimental.pallas import tpu as pltpu

F32 = jnp.float32
BF16 = jnp.bfloat16

CHUNK = 64
ROPE_THETA = 10000.0
EPS = 1e-6
NEG_BIG = -1e30
LB_FLOOR = 1e-20
H_A, DA, DV_A = 6, 32, 64
H_B, D_B = 5, 64
H_I, D_I = 4, 64
TOPK_MAX = 256
H_C, DK_C, DV_C = 5, 64, 64
CONV_W = 3

LANES = 128
SUBLANES = 8
VMEM_LIMIT = 56 * 1024 * 1024

WA = H_A * 2 * DA
WB = H_B * D_B
WBP = 384
WIQ = H_I * D_I
NPAIR = WBP // LANES

SEGS = (("aq", WA, WA), ("ak", WA, WA), ("av", WA, WA),
        ("bq", WB, WBP), ("bk", WB, WBP), ("bv", WB, WBP),
        ("iq", WIQ, WIQ), ("ik", D_I, 2 * D_I), ("iw", H_I, LANES),
        ("cq", WB, WBP), ("cf", WB, WBP), ("ci", WB, WBP), ("cg", WB, WBP))
SEG_OFF = {}
_o = 0
for _n, _w, _p in SEGS:
    SEG_OFF[_n] = (_o, _p)
    _o += _p
W_PACK = _o
PROJ_GROUPS = tuple((SEG_OFF[first][0], SEG_OFF[last][0] + SEG_OFF[last][1] - SEG_OFF[first][0])
                    for first, last in (("aq", "ak"), ("av", "bq"), ("bk", "bv"), ("iq", "iw"),
                                        ("cq", "cf"), ("ci", "cg")))
IN_SIZES = (WA, WA, WA, WB, WB, WB, WIQ, D_I, H_I, WB, WB, WB, WB)

LOG2E = math.log2(math.e)
NT_DIMS = (((1,), (1,)), ((), ()))


def _cparams(n_axes):
    return pltpu.CompilerParams(dimension_semantics=("arbitrary",) * n_axes,
                                vmem_limit_bytes=VMEM_LIMIT)


def _const_spec(shape):
    nd = len(shape)
    return pl.BlockSpec(shape, lambda *_: (0,) * nd, pipeline_mode=pl.Buffered(1))


def _dot(a, b):
    return jnp.dot(a, b, preferred_element_type=F32)


def _dot_nt(a, b):
    return lax.dot_general(a, b, NT_DIMS, preferred_element_type=F32)


def _rope(u, cos, sin, half):
    w = u.shape[-1]
    lane = lax.broadcasted_iota(jnp.int32, u.shape, 1)
    first = (lane % (2 * half)) < half
    rot = jnp.where(first, -pltpu.roll(u, w - half, 1), pltpu.roll(u, half, 1))
    return u * cos + rot * sin


N_PROJ_INPUTS = 8


def _inproj_kernel(*refs, n_aliased, feature_major_caches):
    x_ref, g_ref, w_ref, lb_ref, ca_ref, sa_ref, cb_ref, sb_ref = refs[:N_PROJ_INPUTS]
    (aq_ref, ak_ref, av_ref, ak16_ref, av16_ref, bq_ref, bk_ref, bv_ref, bk16_ref, bv16_ref,
     iq_ref, ik2_ref, ik_ref, iw_ref, cq_ref, ck_ref, cgl_ref, cv_ref, cg_ref) = refs[N_PROJ_INPUTS + n_aliased:]

    def put_cache(ref, val, width):
        ref[...] = val.T[:width, :] if feature_major_caches else val[:, :width]

    x = x_ref[...]
    ms = jnp.mean(x * x, axis=-1, keepdims=True)
    xn = (x * lax.rsqrt(ms + EPS) * g_ref[...]).astype(BF16)

    group_dots = {}

    def seg(name):
        off, width = SEG_OFF[name]
        g_off, g_width = next((o, wd) for o, wd in PROJ_GROUPS if o <= off < o + wd)
        if g_off not in group_dots:
            group_dots[g_off] = _dot(xn, w_ref[:, g_off:g_off + g_width])
        return group_dots[g_off][:, off - g_off:off - g_off + width]

    ca, sa = ca_ref[...], sa_ref[...]
    cb, sb = cb_ref[...], sb_ref[...]
    aq_ref[...] = (_rope(seg("aq"), ca, sa, DA // 2) * (DA ** -0.5 * LOG2E)).astype(BF16)
    ak = _rope(seg("ak"), ca, sa, DA // 2)
    put_cache(ak_ref, ak, WA)
    ak16_ref[...] = ak.astype(BF16)
    av = seg("av")
    put_cache(av_ref, av, WA)
    av16_ref[...] = av.astype(BF16)
    bq_ref[...] = (_rope(seg("bq"), cb, sb, D_B // 2) * (D_B ** -0.5 * LOG2E)).astype(BF16)
    bk = _rope(seg("bk"), cb, sb, D_B // 2)
    put_cache(bk_ref, bk, WB)
    bk16_ref[...] = bk.astype(BF16)
    bv = seg("bv")
    put_cache(bv_ref, bv, WB)
    bv16_ref[...] = bv.astype(BF16)
    iq_ref[...] = (_rope(seg("iq"), cb[:, :WIQ], sb[:, :WIQ], D_I // 2) * (D_I ** -0.5)).astype(BF16)
    ik2 = _rope(seg("ik"), cb[:, :2 * D_I], sb[:, :2 * D_I], D_I // 2)
    ik2_ref[...] = ik2.astype(BF16)
    put_cache(ik_ref, ik2, D_I)
    iw_ref[...] = seg("iw") * (H_I ** -0.5)

    cq = seg("cq")
    cq_ref[...] = cq * jax.nn.sigmoid(cq)
    z = seg("cf")
    lb = lb_ref[...]
    la = jnp.log(jnp.maximum(lb, LB_FLOOR))
    lsig = jnp.minimum(z, 0.0) - jnp.log1p(jnp.exp(-jnp.abs(z)))
    bb = jnp.log1p(-lb) + lsig
    cgl_ref[...] = jnp.maximum(la, bb) + jnp.log1p(jnp.exp(-jnp.abs(la - bb)))
    ck_ref[...] = (1.0 - lb) * jax.nn.sigmoid(-z)
    cv_ref[...] = seg("ci")
    cg_ref[...] = seg("cg")


CACHE_OUTS = ("ak", "av", "bk", "bv", "ik")


def _in_projection(x2d, gain, w_pack, lb_row, tabs, tm, stacked=None):
    n = x2d.shape[0]
    ca, sa, cb, sb = tabs
    period = ca.shape[0] // tm
    row = lambda w: pl.BlockSpec((tm, w), lambda i: (i, 0))
    tab = lambda w: pl.BlockSpec((tm, w), lambda i: (i % period, 0))
    outs = (("aq", WA, BF16), ("ak", WA, F32), ("av", WA, F32), ("ak16", WA, BF16), ("av16", WA, BF16),
            ("bq", WBP, BF16), ("bk", WB, F32), ("bv", WB, F32), ("bk16", WBP, BF16), ("bv16", WBP, BF16),
            ("iq", WIQ, BF16), ("ik2", 2 * D_I, BF16), ("ik", D_I, F32), ("iw", LANES, F32),
            ("cq", WBP, F32), ("ck", WBP, F32), ("cgl", WBP, F32), ("cv", WBP, F32), ("cg", WBP, F32))
    out_specs = {name: row(w) for name, w, _ in outs}
    out_shape = {name: jax.ShapeDtypeStruct((n, w), dt) for name, w, dt in outs}
    inputs = [x2d, gain, w_pack, lb_row, ca, sa, cb, sb]
    in_specs = [row(x2d.shape[1]), _const_spec(gain.shape), _const_spec(w_pack.shape),
                _const_spec(lb_row.shape), tab(WA), tab(WA), tab(WBP), tab(WBP)]
    aliases = {}
    if stacked is not None:
        layer, depth, batch, caches = stacked
        t = n // batch
        tiles = t // tm
        for name, w, _ in outs:
            if name in CACHE_OUTS:
                out_specs[name] = pl.BlockSpec((None, None, w, tm),
                                               lambda i, layer=layer: (layer, i // tiles, 0, i % tiles))
                out_shape[name] = jax.ShapeDtypeStruct((depth, batch, w, t), F32)
        if caches is not None:
            names = [o[0] for o in outs]
            for name in CACHE_OUTS:
                aliases[len(inputs)] = names.index(name)
                inputs.append(caches[name])
                in_specs.append(pl.BlockSpec(memory_space=pl.ANY))
    kern = functools.partial(_inproj_kernel, n_aliased=len(aliases), feature_major_caches=stacked is not None)
    res = pl.pallas_call(
        kern,
        grid=(n // tm,),
        in_specs=in_specs,
        out_specs=[out_specs[o[0]] for o in outs],
        out_shape=[out_shape[o[0]] for o in outs],
        input_output_aliases=aliases,
        compiler_params=_cparams(1),
        name="in_projection",
    )(*inputs)
    return dict(zip([o[0] for o in outs], res))


NARROW_KEY_BLOCK = 256


def _attn_a_kernel(lam_ref, q_ref, k_ref, v_ref, o_ref, q4_ref, m_ref, acc_ref, *, tq, tk, q_off, kv_len):
    qi = pl.program_id(2)
    q = q_ref[...]
    lane = lax.broadcasted_iota(jnp.int32, q.shape, 1)
    for i in range(4):
        q4_ref[i * tq:(i + 1) * tq, :] = jnp.where(lane // DA == i, q, jnp.zeros_like(q))
    m_ref[...] = jnp.full(m_ref.shape, NEG_BIG, F32)
    acc_ref[...] = jnp.zeros(acc_ref.shape, F32)

    q_first = q_off + qi * tq
    q_last = q_first + tq - 1
    n_full = jnp.minimum((q_first // CHUNK + 1) * CHUNK, kv_len) // tk
    lim = jnp.minimum((q_last // CHUNK + 1) * CHUNK, kv_len)
    tkn = min(tk, NARROW_KEY_BLOCK)
    n_narrow = (lim + tkn - 1) // tkn
    n_wide = n_full + (lim - n_full * tk) // tk if tkn < tk else n_narrow

    def step(j, tk, masked):
        start = pl.multiple_of(j * tk, tk)
        kb = k_ref[pl.ds(start, tk), :]
        vb = v_ref[pl.ds(start, tk), :]
        s = _dot_nt(q4_ref[...], kb)
        if masked:
            kpos = start + lax.broadcasted_iota(jnp.int32, (tq, tk), 1)
            qpos = q_first + lax.broadcasted_iota(jnp.int32, (tq, tk), 0)
            valid = (kpos // CHUNK <= qpos // CHUNK) & (kpos < kv_len)
            s = jnp.where(valid[None], s.reshape(4, tq, tk), NEG_BIG).reshape(4 * tq, tk)
        m_old = m_ref[...]
        m_new = jnp.maximum(m_old, jnp.max(s, axis=-1, keepdims=True))
        alpha = jnp.exp2(m_old - m_new)
        p = jnp.exp2(s - jnp.concatenate([m_new] * (tk // LANES), axis=1)).astype(BF16)
        lane_v = lax.broadcasted_iota(jnp.int32, vb.shape, 1)
        ones = jnp.ones_like(vb)
        pv = jnp.concatenate([_dot(p[0:2 * tq], jnp.where(lane_v < DV_A, vb, ones)),
                              _dot(p[2 * tq:4 * tq], jnp.where(lane_v < DV_A, ones, vb))], axis=0)
        acc_ref[...] = alpha * acc_ref[...] + pv
        m_ref[...] = m_new

    def body(tk, masked):
        def run(j, c):
            step(j, tk, masked)
            return c
        return run

    lax.fori_loop(0, n_full, body(tk, False), 0)
    lax.fori_loop(n_full, n_wide, body(tk, True), 0)
    if tkn < tk:
        lax.fori_loop(n_wide * (tk // tkn), n_narrow, body(tkn, True), 0)

    lam = lam_ref[0]
    acc0 = acc_ref[0:2 * tq, :]
    acc1 = acc_ref[2 * tq:4 * tq, :]
    on0 = acc0 / acc0[:, DV_A:DV_A + 1]
    on1 = acc1 / acc1[:, 0:1]
    o0 = on0[0:tq] - lam * on0[tq:2 * tq]
    o1 = on1[0:tq] - lam * on1[tq:2 * tq]
    o_ref[...] = jnp.where(lane < DV_A, o0, o1)


def _attention_a(lam, q, k, v, *, q_off, kv_len, tq, tk):
    b, t, _ = q.shape
    lk = k.shape[1]
    kern = functools.partial(_attn_a_kernel, tq=tq, tk=tk, q_off=q_off, kv_len=kv_len)
    return pl.pallas_call(
        kern,
        grid=(b, WA // LANES, t // tq),
        in_specs=[pl.BlockSpec(memory_space=pltpu.SMEM),
                  pl.BlockSpec((None, tq, LANES), lambda bi, hi, qi: (bi, qi, hi)),
                  pl.BlockSpec((None, lk, LANES), lambda bi, hi, qi: (bi, 0, hi)),
                  pl.BlockSpec((None, lk, LANES), lambda bi, hi, qi: (bi, 0, hi))],
        out_specs=pl.BlockSpec((None, tq, LANES), lambda bi, hi, qi: (bi, qi, hi)),
        out_shape=jax.ShapeDtypeStruct((b, t, WA), F32),
        scratch_shapes=[pltpu.VMEM((4 * tq, LANES), BF16), pltpu.VMEM((4 * tq, LANES), F32),
                        pltpu.VMEM((4 * tq, LANES), F32)],
        compiler_params=_cparams(3),
        name="mixer_a",
    )(lam, q, k, v)


TIE_BLOCK = 256
SEARCH_BITS_PER_TRIP = 3


def _attn_b_kernel(q_ref, iq_ref, iw_ref, k_ref, v_ref, ik_ref, o_ref, key_ref, bias_ref,
                   *, tq, q_off, kv_len, n_sel, variants):
    q_first = q_off + pl.program_id(1) * tq
    need_keys = jnp.minimum(((q_first + tq - 1) // CHUNK + 1) * CHUNK, kv_len)
    lo = 0
    for lk in variants:
        @pl.when((need_keys > lo) & (need_keys <= lk))
        def _(lk=lk):
            _attn_b_body(q_ref, iq_ref, iw_ref, k_ref, v_ref, ik_ref, o_ref, key_ref, bias_ref,
                         tq=tq, q_first=q_first, need_keys=need_keys, kv_len=kv_len, n_sel=n_sel, lk=lk)
        lo = lk


def _stack_heads(x, n_heads, width):
    lane = lax.broadcasted_iota(jnp.int32, x.shape, 1)
    return jnp.concatenate([jnp.where(lane // width == h, x, jnp.zeros_like(x)) for h in range(n_heads)],
                           axis=0)


def _attn_b_body(q_ref, iq_ref, iw_ref, k_ref, v_ref, ik_ref, o_ref, key_ref, bias_ref,
                 *, tq, q_first, need_keys, kv_len, n_sel, lk):
    ik2 = ik_ref[0:lk, :]
    iq = iq_ref[...]
    iw = iw_ref[...]

    score = jnp.zeros((tq, lk), F32)
    for pi in range(H_I // 2):
        y = _stack_heads(iq[:, LANES * pi:LANES * (pi + 1)], 2, D_I)
        d = jnp.maximum(_dot_nt(y, ik2), 0.0)
        score = score + iw[:, 2 * pi:2 * pi + 1] * d[0:tq] + iw[:, 2 * pi + 1:2 * pi + 2] * d[tq:2 * tq]

    kpos = lax.broadcasted_iota(jnp.int32, (tq, lk), 1)
    qpos = q_first + lax.broadcasted_iota(jnp.int32, (tq, lk), 0)
    valid = (kpos // CHUNK <= qpos // CHUNK) & (kpos < kv_len)
    score = jnp.where(valid, score, NEG_BIG)

    bits = lax.bitcast_convert_type(score, jnp.int32)
    key = bits ^ ((bits >> 31) & jnp.int32(0x7FFFFFFF))
    key = jnp.where(key == -1, 0, key)
    key_ref[:, 0:lk] = key

    kf = float(n_sel)

    def count_ge(cand):
        wide = jnp.concatenate([cand] * (lk // LANES), axis=1)
        return jnp.sum(jnp.where(key_ref[:, 0:lk] >= wide, 1.0, 0.0), axis=-1, keepdims=True)

    int_min = jnp.int32(-2 ** 31)
    zero = jnp.zeros((tq, LANES), jnp.int32)
    select_all = need_keys <= n_sel
    cur = jnp.where(select_all | (count_ge(zero) < kf), int_min, zero)

    def try_bit(cur, bit):
        cand = cur | bit
        return jnp.where(count_ge(cand) >= kf, cand, cur)

    cur = jnp.where(select_all, int_min, try_bit(cur, jnp.int32(1 << 30)))

    def bits_body(i, cur):
        for j in range(SEARCH_BITS_PER_TRIP):
            cur = try_bit(cur, jnp.left_shift(jnp.int32(1), 29 - (SEARCH_BITS_PER_TRIP * i + j)))
        return cur

    thr = lax.fori_loop(0, jnp.where(select_all, 0, 30 // SEARCH_BITS_PER_TRIP), bits_body, cur)

    key = key_ref[:, 0:lk]
    thr = jnp.concatenate([thr] * (lk // LANES), axis=1)
    gt = key > thr
    eq = key == thr
    need = kf - jnp.sum(jnp.where(gt, 1.0, 0.0), axis=-1, keepdims=True)
    r_i = lax.broadcasted_iota(jnp.int32, (TIE_BLOCK, TIE_BLOCK), 0)
    c_i = lax.broadcasted_iota(jnp.int32, (TIE_BLOCK, TIE_BLOCK), 1)
    tri = jnp.where(r_i <= c_i, 1.0, 0.0).astype(BF16)
    carry = jnp.zeros((tq, 1), F32)
    for jb in range(lk // TIE_BLOCK):
        sl = slice(jb * TIE_BLOCK, (jb + 1) * TIE_BLOCK)
        eq_b = eq[:, sl]
        pref = _dot(jnp.where(eq_b, 1.0, 0.0).astype(BF16), tri) + carry
        carry = pref[:, TIE_BLOCK - 1:TIE_BLOCK]
        sel = (gt[:, sl] | (eq_b & (pref <= need))) & valid[:, sl]
        bias_ref[:, sl] = jnp.where(sel, 0.0, NEG_BIG)

    q = q_ref[...]
    lane = lax.broadcasted_iota(jnp.int32, (tq, LANES), 1)
    bias = bias_ref[:, 0:lk]
    for p in range(NPAIR):
        n_heads = min(2, H_B - 2 * p)
        sl = slice(p * LANES, (p + 1) * LANES)
        y = _stack_heads(q[:, sl], n_heads, D_B)
        s = _dot_nt(y, k_ref[0:lk, sl]).reshape(n_heads, tq, lk) + bias[None]
        m = jnp.max(s, axis=-1, keepdims=True)
        pr = jnp.exp2(s - m)
        l = jnp.sum(pr, axis=-1, keepdims=True)
        o = _dot(pr.reshape(n_heads * tq, lk).astype(BF16), v_ref[0:lk, sl]) / l.reshape(n_heads * tq, 1)
        o_ref[:, sl] = o if n_heads == 1 else jnp.where(lane < D_B, o[0:tq], o[tq:2 * tq])


def _attention_b(q, iq, iw, k, v, ik2, *, q_off, kv_len, tq, n_sel):
    b, t, _ = q.shape
    lk = k.shape[1]
    granule = tq if (tq % TIE_BLOCK == 0 and lk % (2 * tq) == 0) else lk
    variants = tuple(v for v in range(granule, lk + 1, granule) if v <= lk // 2 or v % (2 * granule) == 0 or v == lk)
    kern = functools.partial(_attn_b_kernel, tq=tq, q_off=q_off, kv_len=kv_len, n_sel=n_sel,
                             variants=variants)
    qspec = lambda w: pl.BlockSpec((None, tq, w), lambda bi, qi: (bi, qi, 0))
    kspec = lambda w: pl.BlockSpec((None, lk, w), lambda bi, qi: (bi, 0, 0))
    return pl.pallas_call(
        kern,
        grid=(b, t // tq),
        in_specs=[qspec(WBP), qspec(WIQ), qspec(LANES), kspec(WBP), kspec(WBP), kspec(2 * D_I)],
        out_specs=qspec(WBP),
        out_shape=jax.ShapeDtypeStruct((b, t, WBP), F32),
        scratch_shapes=[pltpu.VMEM((tq, lk), jnp.int32), pltpu.VMEM((tq, lk), F32)],
        compiler_params=_cparams(2),
        name="mixer_b",
    )(q, iq, iw, k, v, ik2)


HGRN2_CHUNKS_PER_STEP = 8
GROUP = SUBLANES
DECAY_SPLIT_LIMIT = 60.0


def _hgrn2_kernel(q_ref, k_ref, g_ref, v_ref, s0_ref, o_ref, s_out_ref, st_ref, *, c, chunks):
    ci = pl.program_id(1)

    @pl.when(ci == 0)
    def _():
        st_ref[...] = s0_ref[...]

    views = [tuple(r.at[pl.ds(i * c, c)] for r in (q_ref, k_ref, g_ref, v_ref, o_ref)) for i in range(chunks)]
    bound = functools.reduce(jnp.maximum, [_half_chunk_bound(view[2][...], c) for view in views])
    single_split = bound <= DECAY_SPLIT_LIMIT

    @pl.when(single_split)
    def _():
        for view in views:
            _hgrn2_chunk(*view, st_ref, c=c, single_split=True)

    @pl.when(jnp.logical_not(single_split))
    def _():
        for view in views:
            _hgrn2_chunk(*view, st_ref, c=c, single_split=False)

    @pl.when(ci == pl.num_programs(1) - 1)
    def _():
        s_out_ref[...] = st_ref[...]


def _half_chunk_bound(g, c):
    ag = jnp.abs(g)
    return jnp.maximum(jnp.max(jnp.sum(ag[0:c // 2], axis=0, keepdims=True)),
                       jnp.max(jnp.sum(ag[c // 2:c], axis=0, keepdims=True)))


def _hgrn2_chunk(q_ref, k_ref, g_ref, v_ref, o_ref, st_ref, *, c, single_split):
    q = q_ref[...]
    k = k_ref[...]
    g = g_ref[...]
    v = v_ref[...]
    w = q.shape[-1]
    row = lax.broadcasted_iota(jnp.int32, (c, w), 0)
    lane = lax.broadcasted_iota(jnp.int32, (c, w), 1)

    def segment_scans():
        cs, tots = {1: g}, {1: g}
        cum, tot, m = g, g, 1
        while m < c:
            upper = (row // m) % 2 == 1
            prev_tot = pltpu.roll(tot, m, 0)
            next_tot = pltpu.roll(tot, c - m, 0)
            cum = cum + jnp.where(upper, prev_tot, 0.0)
            tot = tot + jnp.where(upper, prev_tot, next_tot)
            m *= 2
            cs[m], tots[m] = cum, tot
        return cs, tots

    def cumsum_rows():
        cum, m = g, 1
        while m < c:
            cum = cum + jnp.where(row >= m, pltpu.roll(cum, m, 0), 0.0)
            m *= 2
        return cum

    head_masks = [lane // DK_C == h for h in range(w // DK_C)]
    n_heads = H_C
    rq = lax.broadcasted_iota(jnp.int32, (c, c), 0)
    rk = lax.broadcasted_iota(jnp.int32, (c, c), 1)

    def attend(att):
        res = _dot(att.astype(BF16), v.astype(BF16))
        out = jnp.zeros((c, w), F32)
        for h in range(n_heads):
            out = out + jnp.where(head_masks[h], res[h * c:(h + 1) * c], 0.0)
        return out

    def stack_heads(x):
        return jnp.concatenate([jnp.where(head_masks[h], x, 0.0) for h in range(n_heads)], axis=0)

    def finish(o_intra, bcum, blast):
        qe = (q * jnp.exp(bcum)).astype(BF16)
        k2 = (k * jnp.exp(blast - bcum)).astype(BF16)
        decay = jnp.exp(blast[0:1, :])
        pr = lax.broadcasted_iota(jnp.int32, (LANES, LANES), 0)
        pc = lax.broadcasted_iota(jnp.int32, (LANES, LANES), 1)
        diag = pr // DK_C == pc // DK_C
        o_state = []
        for p in range(w // LANES):
            sl = slice(p * LANES, (p + 1) * LANES)
            st = st_ref[p]
            o_state.append(_dot_nt(qe[:, sl], st.astype(BF16)))
            upd = _dot(v[:, sl].T.astype(BF16), k2[:, sl])
            st_ref[p] = st * decay[:, sl] + jnp.where(diag, upd, 0.0)
        o_ref[...] = o_intra + jnp.concatenate(o_state, axis=-1)

    if single_split:
        bcum = cumsum_rows()
        dmid = bcum - bcum[c // 2 - 1:c // 2, :]
        qt = stack_heads(q * jnp.exp(dmid)).astype(BF16)
        kt = (k * jnp.exp(-dmid)).astype(BF16)
        causal = jnp.concatenate([rq >= rk] * n_heads, axis=0)
        o_intra = attend(jnp.where(causal, _dot_nt(qt, kt), 0.0))
        finish(o_intra, bcum, jnp.broadcast_to(bcum[c - 1:c, :], (c, w)))

    else:
        cs, tots = segment_scans()
        att = jnp.zeros((n_heads * c, c), F32)
        half = GROUP
        while half < c:
            upper = (row // half) % 2 == 1
            qt = jnp.where(upper, q * jnp.exp(cs[half]), 0.0)
            kt = jnp.where(upper, 0.0, k * jnp.exp(tots[half] - cs[half])).astype(BF16)
            blk = _dot_nt(stack_heads(qt).astype(BF16), kt)
            same = (rq // (2 * half)) == (rk // (2 * half))
            same = jnp.concatenate([same] * n_heads, axis=0)
            att = att + jnp.where(same, blk, 0.0)
            half *= 2
        out = attend(att)

        def group_row(x, j):
            x3 = x.reshape(c // GROUP, GROUP, w)
            return jnp.broadcast_to(x3[:, j:j + 1, :], x3.shape).reshape(c, w)

        c8 = cs[GROUP]
        vals = []
        for j in range(GROUP):
            ok = (row % GROUP) >= j
            e = jnp.where(ok, c8 - group_row(c8, j), 0.0)
            vals.append(jnp.where(ok, q * group_row(k, j) * jnp.exp(e), 0.0))
        r_i = lax.broadcasted_iota(jnp.int32, (w, w), 0)
        c_i = lax.broadcasted_iota(jnp.int32, (w, w), 1)
        head_sum = jnp.where(r_i // DK_C == c_i // DK_C, 1.0, 0.0).astype(BF16)
        wts = _dot(jnp.concatenate(vals, axis=0).astype(BF16), head_sum)
        for j in range(GROUP):
            out = out + wts[j * c:(j + 1) * c] * group_row(v, j)
        finish(out, cs[c], tots[c])


def _hgrn2(q, k, g, v, s0, *, c):
    b, t, w = q.shape
    chunks = HGRN2_CHUNKS_PER_STEP if t % (HGRN2_CHUNKS_PER_STEP * c) == 0 else 1
    kern = functools.partial(_hgrn2_kernel, c=c, chunks=chunks)
    blk = pl.BlockSpec((None, chunks * c, w), lambda bi, ci: (bi, ci, 0))
    sblk = pl.BlockSpec((None, w // LANES, LANES, LANES), lambda bi, ci: (bi, 0, 0, 0))
    return pl.pallas_call(
        kern,
        grid=(b, t // (chunks * c)),
        in_specs=[blk, blk, blk, blk, sblk],
        out_specs=[blk, sblk],
        out_shape=[jax.ShapeDtypeStruct((b, t, w), F32),
                   jax.ShapeDtypeStruct((b, w // LANES, LANES, LANES), F32)],
        scratch_shapes=[pltpu.VMEM((w // LANES, LANES, LANES), F32)],
        compiler_params=_cparams(2),
        name="mixer_c",
    )(q, k, g, v, s0)


def _state_to_pairs(s):
    b = s.shape[0]
    st = jnp.swapaxes(s.astype(F32), -1, -2)
    st = jnp.pad(st, ((0, 0), (0, 2 * NPAIR - H_C), (0, 0), (0, 0)))
    st = st.reshape(b, NPAIR, 2, DV_C, DK_C)
    eye = jnp.eye(2, dtype=F32)
    full = st[:, :, :, :, None, :] * eye[None, None, :, None, :, None]
    return full.reshape(b, NPAIR, 2 * DV_C, 2 * DK_C)


def _pairs_to_state(sp):
    b = sp.shape[0]
    s6 = sp.reshape(b, NPAIR, 2, DV_C, 2, DK_C)
    diag = jnp.stack([s6[:, :, a, :, a, :] for a in range(2)], axis=2)
    return jnp.swapaxes(diag.reshape(b, 2 * NPAIR, DV_C, DK_C)[:, :H_C], -1, -2)


FF_BLOCK = 2816


def _head_norm(y, gain):
    w = y.shape[-1]
    r_i = lax.broadcasted_iota(jnp.int32, (w, w), 0)
    c_i = lax.broadcasted_iota(jnp.int32, (w, w), 1)
    head_sum = jnp.where(r_i // DV_A == c_i // DV_A, 1.0, 0.0).astype(BF16)
    y2 = y * y
    hi = y2.astype(BF16)
    lo = (y2 - hi.astype(F32)).astype(BF16)
    ms = (_dot(hi, head_sum) + _dot(lo, head_sum)) * (1.0 / DV_A)
    return y * lax.rsqrt(ms + EPS) * gain


def _merge_kernel(scal_ref, x_ref, oa_ref, ob_ref, oc_ref, cg_ref, hist_ref, ag_ref, cgn_ref, wo_ref,
                  n2_ref, wup_ref, cw_ref, cb_ref, wdn_ref, fn_ref, *out_and_scratch, d_ff, final):
    if final:
        x_out_ref, fc_ref, y_ref, carry_ref = out_and_scratch
    else:
        x_out_ref, fc_ref, carry_ref = out_and_scratch
    ti = pl.program_id(1)
    tm = x_ref.shape[0]

    @pl.when(ti == 0)
    def _():
        carry_ref[...] = hist_ref[...]

    oa = _head_norm(oa_ref[...], ag_ref[...]) * scal_ref[0]
    cg = cg_ref[...]
    oc = _head_norm(oc_ref[...], cgn_ref[...]) * (cg * jax.nn.sigmoid(cg))
    mixed = (_dot(oa.astype(BF16), wo_ref[0:WA, :])
             + _dot(ob_ref[...].astype(BF16), wo_ref[WA:WA + WBP, :])
             + _dot(oc.astype(BF16), wo_ref[WA + WBP:WA + 2 * WBP, :]))
    x = x_ref[...] + mixed

    ms = jnp.mean(x * x, axis=-1, keepdims=True)
    h = (x * lax.rsqrt(ms + EPS) * n2_ref[...]).astype(BF16)
    row = lax.broadcasted_iota(jnp.int32, (tm, FF_BLOCK), 0)
    acc = jnp.zeros(x.shape, F32)
    for cblk in range(d_ff // FF_BLOCK):
        sl = slice(cblk * FF_BLOCK, (cblk + 1) * FF_BLOCK)
        a = _dot(h, wup_ref[:, sl])
        gate = _dot(h, wup_ref[:, d_ff + cblk * FF_BLOCK:d_ff + (cblk + 1) * FF_BLOCK])
        prev2 = carry_ref[0:1, sl]
        prev1 = carry_ref[1:2, sl]
        a1 = jnp.where(row == 0, prev1, pltpu.roll(a, 1, 0))
        a2 = jnp.where(row == 0, prev2, jnp.where(row == 1, prev1, pltpu.roll(a, 2, 0)))
        conv = cb_ref[:, sl] + a2 * cw_ref[0:1, sl] + a1 * cw_ref[1:2, sl] + a * cw_ref[2:3, sl]
        act = conv * jax.nn.sigmoid(conv) * gate
        acc = acc + _dot(act.astype(BF16), wdn_ref[sl, :])
        carry_ref[:, sl] = a[tm - (CONV_W - 1):, :]
    x = x + acc
    x_out_ref[...] = x
    fc_ref[...] = carry_ref[...]
    if final:
        ms = jnp.mean(x * x, axis=-1, keepdims=True)
        y_ref[...] = x * lax.rsqrt(ms + EPS) * fn_ref[...]


def _merge_ffn(scal, x, oa, ob, oc, cg, hist, a_gain, c_gain, wo, n2, wup, cw, cb, wdn, fnorm, *, tm, final):
    b, t, d = x.shape
    d_ff = wdn.shape[0]
    kern = functools.partial(_merge_kernel, d_ff=d_ff, final=final)
    blk = lambda w: pl.BlockSpec((None, tm, w), lambda bi, ti: (bi, ti, 0))
    per_b = pl.BlockSpec((None, CONV_W - 1, d_ff), lambda bi, ti: (bi, 0, 0))
    out_specs = [blk(d), per_b]
    out_shape = [jax.ShapeDtypeStruct((b, t, d), F32), jax.ShapeDtypeStruct((b, CONV_W - 1, d_ff), F32)]
    if final:
        out_specs.append(blk(d))
        out_shape.append(jax.ShapeDtypeStruct((b, t, d), F32))
    return pl.pallas_call(
        kern,
        grid=(b, t // tm),
        in_specs=[pl.BlockSpec(memory_space=pltpu.SMEM), blk(d), blk(WA), blk(WBP), blk(WBP), blk(WBP), per_b,
                  _const_spec(a_gain.shape), _const_spec(c_gain.shape), _const_spec(wo.shape),
                  _const_spec(n2.shape), _const_spec(wup.shape), _const_spec(cw.shape),
                  _const_spec(cb.shape), _const_spec(wdn.shape), _const_spec(fnorm.shape)],
        out_specs=out_specs,
        out_shape=out_shape,
        scratch_shapes=[pltpu.VMEM((CONV_W - 1, d_ff), F32)],
        compiler_params=_cparams(2),
        name="merge_ffn",
    )(scal, x, oa, ob, oc, cg, hist, a_gain, c_gain, wo, n2, wup, cw, cb, wdn, fnorm)


def _pack_w_in(w):
    parts = jnp.split(w, np.cumsum(IN_SIZES)[:-1].tolist(), axis=-1)
    cols = []
    for (name, width, padded), part in zip(SEGS, parts):
        if name == "ik":
            part = jnp.concatenate([part, part], axis=-1)
            width = 2 * D_I
        cols.append(jnp.pad(part, ((0, 0), (0, padded - width))))
    return jnp.concatenate(cols, axis=-1).astype(BF16)


def _pack_w_out(w):
    wa, wb, wc = w[:WA], w[WA:WA + WB], w[WA + WB:]
    pad = lambda m: jnp.pad(m, ((0, WBP - WB), (0, 0)))
    return jnp.concatenate([wa, pad(wb), pad(wc)], axis=0).astype(BF16)


def _rope_table(pos, dim, width):
    inv_freq = ROPE_THETA ** (-jnp.arange(0, dim, 2, dtype=F32) / dim)
    ang = pos.astype(F32)[:, None] * inv_freq[None, :]
    reps = width // (dim // 2)
    return jnp.tile(jnp.cos(ang), (1, reps)), jnp.tile(jnp.sin(ang), (1, reps))


def _cache_prep_kernel(c_ref, *o_refs, past):
    for layer, o_ref in enumerate(o_refs):
        x = c_ref[layer]
        n_feat = x.shape[0]
        if n_feat < LANES:
            x = jnp.concatenate([x] * (LANES // n_feat), axis=0)
        elif n_feat % LANES:
            x = jnp.concatenate([x, jnp.zeros((LANES - n_feat % LANES, past), x.dtype)], axis=0)
        for j in range(x.shape[0] // LANES):
            o_ref[0:past, j * LANES:(j + 1) * LANES] = x[j * LANES:(j + 1) * LANES].T.astype(BF16)
        o_ref[past:, :] = jnp.zeros((o_ref.shape[0] - past, o_ref.shape[1]), BF16)


def _cache_prep(cache, rows):
    depth, batch, past = cache.shape[:3]
    n_feat = int(np.prod(cache.shape[3:]))
    nd = cache.ndim
    c = jnp.transpose(cache, (0, 1) + tuple(range(3, nd)) + (2,)).reshape(depth, batch, n_feat, past)
    width = _round_up(n_feat, LANES)
    return pl.pallas_call(
        functools.partial(_cache_prep_kernel, past=past),
        grid=(batch,),
        in_specs=[pl.BlockSpec((depth, None, n_feat, past), lambda i: (0, i, 0, 0))],
        out_specs=[pl.BlockSpec((None, rows, width), lambda i: (i, 0, 0))] * depth,
        out_shape=[jax.ShapeDtypeStruct((batch, rows, width), BF16)] * depth,
        compiler_params=_cparams(1),
        name="cache_prep",
    )(c)


def _round_up(n, m):
    return (n + m - 1) // m * m


def kernel(x_prompt, x_sample, cache_a_k, cache_a_v, cache_b_k, cache_b_v, cache_b_kidx, state_c, state_ffn_conv, norm1, w_in, lam_q1, lam_k1, lam_q2, lam_k2, a_norm, c_lower, c_norm, w_out, norm2, ffn_up, ffn_conv_w, ffn_conv_b, ffn_down, final_norm):
    depth = w_in.shape[0]
    b_p, t_p, d = x_prompt.shape
    b_s, t_s, _ = x_sample.shape
    past = cache_a_k.shape[2]
    d_ff = ffn_down.shape[1]
    kv_s = past + t_s
    n_sel_p = min(TOPK_MAX, t_p // 4)
    n_sel_s = min(TOPK_MAX, kv_s // 4)
    tk = 256
    lk_s = _round_up(kv_s, tk)

    lb_soft = jax.nn.softmax(c_lower.astype(F32), axis=0)
    lower = jnp.cumsum(lb_soft, axis=0) - lb_soft[0]
    lower = jnp.pad(lower, ((0, 0), (0, WBP - WB)))

    tm_p = min(512, t_p)
    tm_s = min(256, b_s * t_s)
    pos_p = jnp.arange(t_p)
    pos_s = jnp.tile(past + jnp.arange(t_s), tm_s // t_s)
    tabs_p = _rope_table(pos_p, DA, WA) + _rope_table(pos_p, D_B, WBP)
    tabs_s = _rope_table(pos_s, DA, WA) + _rope_table(pos_s, D_B, WBP)

    past_kv = {name: _cache_prep(c, lk_s) for name, c in
               (("ak", cache_a_k), ("av", cache_a_v), ("bk", cache_b_k), ("bv", cache_b_v), ("ik", cache_b_kidx))}

    xp, xs = x_prompt, x_sample
    caches_p, state_p, conv_p = None, [], []
    outs_s = [[] for _ in range(7)]
    y_p = y_s = None
    fnorm = final_norm.reshape(1, d)
    for l in range(depth):
        lam_init = 0.8 - 0.6 * math.exp(-0.3 * l)
        lam = (jnp.exp(jnp.sum(lam_q1[l].astype(F32) * lam_k1[l].astype(F32)))
               - jnp.exp(jnp.sum(lam_q2[l].astype(F32) * lam_k2[l].astype(F32))) + lam_init)
        lam_arr = lam.reshape(1).astype(F32)
        scal = jnp.full((1,), 1.0 - lam_init, F32)
        w_pack = _pack_w_in(w_in[l])
        wo = _pack_w_out(w_out[l])
        gain1 = norm1[l].reshape(1, d)
        gain2 = norm2[l].reshape(1, d)
        lb_row = lower[l].reshape(1, WBP)
        a_gain = jnp.tile(a_norm[l], H_A).reshape(1, WA)
        c_gain = jnp.pad(jnp.tile(c_norm[l], H_C), (0, WBP - WB)).reshape(1, WBP)
        wup = ffn_up[l].astype(BF16)
        wdn = ffn_down[l].astype(BF16)
        cw = ffn_conv_w[l]
        cb = ffn_conv_b[l].reshape(1, d_ff)
        final = l == depth - 1

        u = _in_projection(xp.reshape(b_p * t_p, d), gain1, w_pack, lb_row, tabs_p, tm_p,
                           stacked=(l, depth, b_p, caches_p))
        caches_p = {name: u[name] for name in CACHE_OUTS}
        r3 = lambda a, b=b_p, t=t_p: a.reshape(b, t, a.shape[-1])
        oa = _attention_a(lam_arr, r3(u["aq"]), r3(u["ak16"]), r3(u["av16"]),
                          q_off=0, kv_len=t_p, tq=min(256, t_p), tk=min(512, t_p))
        ob = _attention_b(r3(u["bq"]), r3(u["iq"]), r3(u["iw"]), r3(u["bk16"]), r3(u["bv16"]), r3(u["ik2"]),
                          q_off=0, kv_len=t_p, tq=min(256, t_p), n_sel=n_sel_p)
        s0 = jnp.zeros((b_p, NPAIR, LANES, LANES), F32)
        oc, s_new = _hgrn2(r3(u["cq"]), r3(u["ck"]), r3(u["cgl"]), r3(u["cv"]), s0, c=CHUNK)
        hist0 = jnp.zeros((b_p, CONV_W - 1, d_ff), F32)
        res = _merge_ffn(scal, xp, oa, ob, oc, r3(u["cg"]), hist0, a_gain, c_gain, wo, gain2, wup, cw, cb,
                         wdn, fnorm, tm=min(512, t_p), final=final)
        xp, fc = res[0], res[1]
        if final:
            y_p = res[2]
        state_p.append(_pairs_to_state(s_new))
        conv_p.append(fc)

        u = _in_projection(xs.reshape(b_s * t_s, d), gain1, w_pack, lb_row, tabs_s, tm_s)
        r3 = lambda a, b=b_s, t=t_s: a.reshape(b, t, a.shape[-1])

        def cat(name, new):
            return lax.dynamic_update_slice(past_kv[name][l], r3(new), (0, past, 0))

        oa = _attention_a(lam_arr, r3(u["aq"]), cat("ak", u["ak16"]), cat("av", u["av16"]),
                          q_off=past, kv_len=kv_s, tq=t_s, tk=tk)
        ob = _attention_b(r3(u["bq"]), r3(u["iq"]), r3(u["iw"]), cat("bk", u["bk16"]),
                          cat("bv", u["bv16"]), cat("ik", u["ik2"]),
                          q_off=past, kv_len=kv_s, tq=t_s, n_sel=n_sel_s)
        oc, s_new = _hgrn2(r3(u["cq"]), r3(u["ck"]), r3(u["cgl"]), r3(u["cv"]), _state_to_pairs(state_c[l]),
                           c=t_s)
        res = _merge_ffn(scal, xs, oa, ob, oc, r3(u["cg"]), state_ffn_conv[l].astype(F32), a_gain, c_gain, wo,
                         gain2, wup, cw, cb, wdn, fnorm, tm=t_s, final=final)
        xs, fc = res[0], res[1]
        if final:
            y_s = res[2]
        for lst, val in zip(outs_s, (u["ak"].reshape(b_s, t_s, H_A, 2 * DA), u["av"].reshape(b_s, t_s, H_A, DV_A),
                                     u["bk"].reshape(b_s, t_s, H_B, D_B), u["bv"].reshape(b_s, t_s, H_B, D_B),
                                     u["ik"].reshape(b_s, t_s, D_I), _pairs_to_state(s_new), fc)):
            lst.append(val)

    def frames_major(c, heads):
        c = c.reshape(depth, b_p, heads, c.shape[2] // heads, t_p)
        return jnp.transpose(c, (0, 1, 4, 2, 3))

    outs_p = (frames_major(caches_p["ak"], H_A), frames_major(caches_p["av"], H_A),
              frames_major(caches_p["bk"], H_B), frames_major(caches_p["bv"], H_B),
              jnp.swapaxes(caches_p["ik"], 2, 3), jnp.stack(state_p), jnp.stack(conv_p))
    return (y_p, y_s) + outs_p + tuple(jnp.stack(v) for v in outs_s)
```

```python
import functools
import math

import jax
import jax.numpy as jnp
import numpy as np
from jax import lax
from jax.experimental import pallas as pl
from jax.experimental.pallas import tpu as pltpu

F32 = jnp.float32
BF16 = jnp.bfloat16

CHUNK = 64
ROPE_THETA = 10000.0
EPS = 1e-6
NEG_BIG = -1e30
LB_FLOOR = 1e-20
H_A, DA, DV_A = 6, 32, 64
H_B, D_B = 5, 64
H_I, D_I = 4, 64
TOPK_MAX = 256
H_C, DK_C, DV_C = 5, 64, 64
CONV_W = 3

LANES = 128
SUBLANES = 8
VMEM_LIMIT = 56 * 1024 * 1024

WA = H_A * 2 * DA
WB = H_B * D_B
WBP = 384
WIQ = H_I * D_I
NPAIR = WBP // LANES

SEGS = (("aq", WA, WA), ("ak", WA, WA), ("av", WA, WA),
        ("bq", WB, WBP), ("bk", WB, WBP), ("bv", WB, WBP),
        ("iq", WIQ, WIQ), ("ik", D_I, 2 * D_I), ("iw", H_I, LANES),
        ("cq", WB, WBP), ("cf", WB, WBP), ("ci", WB, WBP), ("cg", WB, WBP))
SEG_OFF = {}
_o = 0
for _n, _w, _p in SEGS:
    SEG_OFF[_n] = (_o, _p)
    _o += _p
W_PACK = _o
PROJ_GROUPS = tuple((SEG_OFF[first][0], SEG_OFF[last][0] + SEG_OFF[last][1] - SEG_OFF[first][0])
                    for first, last in (("aq", "ak"), ("av", "bq"), ("bk", "bv"), ("iq", "iw"),
                                        ("cq", "cf"), ("ci", "cg")))
IN_SIZES = (WA, WA, WA, WB, WB, WB, WIQ, D_I, H_I, WB, WB, WB, WB)

LOG2E = math.log2(math.e)
NT_DIMS = (((1,), (1,)), ((), ()))


def _cparams(n_axes):
    return pltpu.CompilerParams(dimension_semantics=("arbitrary",) * n_axes,
                                vmem_limit_bytes=VMEM_LIMIT)


def _const_spec(shape):
    nd = len(shape)
    return pl.BlockSpec(shape, lambda *_: (0,) * nd, pipeline_mode=pl.Buffered(1))


def _dot(a, b):
    return jnp.dot(a, b, preferred_element_type=F32)


def _dot_nt(a, b):
    return lax.dot_general(a, b, NT_DIMS, preferred_element_type=F32)


def _rope(u, cos, sin, half):
    w = u.shape[-1]
    lane = lax.broadcasted_iota(jnp.int32, u.shape, 1)
    first = (lane % (2 * half)) < half
    rot = jnp.where(first, -pltpu.roll(u, w - half, 1), pltpu.roll(u, half, 1))
    return u * cos + rot * sin


N_PROJ_INPUTS = 8


def _inproj_kernel(*refs, n_aliased, feature_major_caches):
    x_ref, g_ref, w_ref, lb_ref, ca_ref, sa_ref, cb_ref, sb_ref = refs[:N_PROJ_INPUTS]
    (aq_ref, ak_ref, av_ref, ak16_ref, av16_ref, bq_ref, bk_ref, bv_ref, bk16_ref, bv16_ref,
     iq_ref, ik2_ref, ik_ref, iw_ref, cq_ref, ck_ref, cgl_ref, cv_ref, cg_ref) = refs[N_PROJ_INPUTS + n_aliased:]

    def put_cache(ref, val, width):
        ref[...] = val.T[:width, :] if feature_major_caches else val[:, :width]

    x = x_ref[...]
    ms = jnp.mean(x * x, axis=-1, keepdims=True)
    xn = (x * lax.rsqrt(ms + EPS) * g_ref[...]).astype(BF16)

    group_dots = {}

    def seg(name):
        off, width = SEG_OFF[name]
        g_off, g_width = next((o, wd) for o, wd in PROJ_GROUPS if o <= off < o + wd)
        if g_off not in group_dots:
            group_dots[g_off] = _dot(xn, w_ref[:, g_off:g_off + g_width])
        return group_dots[g_off][:, off - g_off:off - g_off + width]

    ca, sa = ca_ref[...], sa_ref[...]
    cb, sb = cb_ref[...], sb_ref[...]
    aq_ref[...] = (_rope(seg("aq"), ca, sa, DA // 2) * (DA ** -0.5 * LOG2E)).astype(BF16)
    ak = _rope(seg("ak"), ca, sa, DA // 2)
    put_cache(ak_ref, ak, WA)
    ak16_ref[...] = ak.astype(BF16)
    av = seg("av")
    put_cache(av_ref, av, WA)
    av16_ref[...] = av.astype(BF16)
    bq_ref[...] = (_rope(seg("bq"), cb, sb, D_B // 2) * (D_B ** -0.5 * LOG2E)).astype(BF16)
    bk = _rope(seg("bk"), cb, sb, D_B // 2)
    put_cache(bk_ref, bk, WB)
    bk16_ref[...] = bk.astype(BF16)
    bv = seg("bv")
    put_cache(bv_ref, bv, WB)
    bv16_ref[...] = bv.astype(BF16)
    iq_ref[...] = (_rope(seg("iq"), cb[:, :WIQ], sb[:, :WIQ], D_I // 2) * (D_I ** -0.5)).astype(BF16)
    ik2 = _rope(seg("ik"), cb[:, :2 * D_I], sb[:, :2 * D_I], D_I // 2)
    ik2_ref[...] = ik2.astype(BF16)
    put_cache(ik_ref, ik2, D_I)
    iw_ref[...] = seg("iw") * (H_I ** -0.5)

    cq = seg("cq")
    cq_ref[...] = cq * jax.nn.sigmoid(cq)
    z = seg("cf")
    lb = lb_ref[...]
    la = jnp.log(jnp.maximum(lb, LB_FLOOR))
    lsig = jnp.minimum(z, 0.0) - jnp.log1p(jnp.exp(-jnp.abs(z)))
    bb = jnp.log1p(-lb) + lsig
    cgl_ref[...] = jnp.maximum(la, bb) + jnp.log1p(jnp.exp(-jnp.abs(la - bb)))
    ck_ref[...] = (1.0 - lb) * jax.nn.sigmoid(-z)
    cv_ref[...] = seg("ci")
    cg_ref[...] = seg("cg")


CACHE_OUTS = ("ak", "av", "bk", "bv", "ik")


def _in_projection(x2d, gain, w_pack, lb_row, tabs, tm, stacked=None):
    n = x2d.shape[0]
    ca, sa, cb, sb = tabs
    period = ca.shape[0] // tm
    row = lambda w: pl.BlockSpec((tm, w), lambda i: (i, 0))
    tab = lambda w: pl.BlockSpec((tm, w), lambda i: (i % period, 0))
    outs = (("aq", WA, BF16), ("ak", WA, F32), ("av", WA, F32), ("ak16", WA, BF16), ("av16", WA, BF16),
            ("bq", WBP, BF16), ("bk", WB, F32), ("bv", WB, F32), ("bk16", WBP, BF16), ("bv16", WBP, BF16),
            ("iq", WIQ, BF16), ("ik2", 2 * D_I, BF16), ("ik", D_I, F32), ("iw", LANES, F32),
            ("cq", WBP, F32), ("ck", WBP, F32), ("cgl", WBP, F32), ("cv", WBP, F32), ("cg", WBP, F32))
    out_specs = {name: row(w) for name, w, _ in outs}
    out_shape = {name: jax.ShapeDtypeStruct((n, w), dt) for name, w, dt in outs}
    inputs = [x2d, gain, w_pack, lb_row, ca, sa, cb, sb]
    in_specs = [row(x2d.shape[1]), _const_spec(gain.shape), _const_spec(w_pack.shape),
                _const_spec(lb_row.shape), tab(WA), tab(WA), tab(WBP), tab(WBP)]
    aliases = {}
    if stacked is not None:
        layer, depth, batch, caches = stacked
        t = n // batch
        tiles = t // tm
        for name, w, _ in outs:
            if name in CACHE_OUTS:
                out_specs[name] = pl.BlockSpec((None, None, w, tm),
                                               lambda i, layer=layer: (layer, i // tiles, 0, i % tiles))
                out_shape[name] = jax.ShapeDtypeStruct((depth, batch, w, t), F32)
        if caches is not None:
            names = [o[0] for o in outs]
            for name in CACHE_OUTS:
                aliases[len(inputs)] = names.index(name)
                inputs.append(caches[name])
                in_specs.append(pl.BlockSpec(memory_space=pl.ANY))
    kern = functools.partial(_inproj_kernel, n_aliased=len(aliases), feature_major_caches=stacked is not None)
    res = pl.pallas_call(
        kern,
        grid=(n // tm,),
        in_specs=in_specs,
        out_specs=[out_specs[o[0]] for o in outs],
        out_shape=[out_shape[o[0]] for o in outs],
        input_output_aliases=aliases,
        compiler_params=_cparams(1),
        name="in_projection",
    )(*inputs)
    return dict(zip([o[0] for o in outs], res))


NARROW_KEY_BLOCK = 256


def _attn_a_kernel(lam_ref, q_ref, k_ref, v_ref, o_ref, q4_ref, m_ref, acc_ref, *, tq, tk, q_off, kv_len):
    qi = pl.program_id(2)
    q = q_ref[...]
    lane = lax.broadcasted_iota(jnp.int32, q.shape, 1)
    for i in range(4):
        q4_ref[i * tq:(i + 1) * tq, :] = jnp.where(lane // DA == i, q, jnp.zeros_like(q))
    m_ref[...] = jnp.full(m_ref.shape, NEG_BIG, F32)
    acc_ref[...] = jnp.zeros(acc_ref.shape, F32)

    q_first = q_off + qi * tq
    q_last = q_first + tq - 1
    n_full = jnp.minimum((q_first // CHUNK + 1) * CHUNK, kv_len) // tk
    lim = jnp.minimum((q_last // CHUNK + 1) * CHUNK, kv_len)
    tkn = min(tk, NARROW_KEY_BLOCK)
    n_narrow = (lim + tkn - 1) // tkn
    n_wide = n_full + (lim - n_full * tk) // tk if tkn < tk else n_narrow

    def step(j, tk, masked):
        start = pl.multiple_of(j * tk, tk)
        kb = k_ref[pl.ds(start, tk), :]
        vb = v_ref[pl.ds(start, tk), :]
        s = _dot_nt(q4_ref[...], kb)
        if masked:
            kpos = start + lax.broadcasted_iota(jnp.int32, (tq, tk), 1)
            qpos = q_first + lax.broadcasted_iota(jnp.int32, (tq, tk), 0)
            valid = (kpos // CHUNK <= qpos // CHUNK) & (kpos < kv_len)
            s = jnp.where(valid[None], s.reshape(4, tq, tk), NEG_BIG).reshape(4 * tq, tk)
        m_old = m_ref[...]
        m_new = jnp.maximum(m_old, jnp.max(s, axis=-1, keepdims=True))
        alpha = jnp.exp2(m_old - m_new)
        p = jnp.exp2(s - jnp.concatenate([m_new] * (tk // LANES), axis=1)).astype(BF16)
        lane_v = lax.broadcasted_iota(jnp.int32, vb.shape, 1)
        ones = jnp.ones_like(vb)
        pv = jnp.concatenate([_dot(p[0:2 * tq], jnp.where(lane_v < DV_A, vb, ones)),
                              _dot(p[2 * tq:4 * tq], jnp.where(lane_v < DV_A, ones, vb))], axis=0)
        acc_ref[...] = alpha * acc_ref[...] + pv
        m_ref[...] = m_new

    def body(tk, masked):
        def run(j, c):
            step(j, tk, masked)
            return c
        return run

    lax.fori_loop(0, n_full, body(tk, False), 0)
    lax.fori_loop(n_full, n_wide, body(tk, True), 0)
    if tkn < tk:
        lax.fori_loop(n_wide * (tk // tkn), n_narrow, body(tkn, True), 0)

    lam = lam_ref[0]
    acc0 = acc_ref[0:2 * tq, :]
    acc1 = acc_ref[2 * tq:4 * tq, :]
    on0 = acc0 / acc0[:, DV_A:DV_A + 1]
    on1 = acc1 / acc1[:, 0:1]
    o0 = on0[0:tq] - lam * on0[tq:2 * tq]
    o1 = on1[0:tq] - lam * on1[tq:2 * tq]
    o_ref[...] = jnp.where(lane < DV_A, o0, o1)


def _attention_a(lam, q, k, v, *, q_off, kv_len, tq, tk):
    b, t, _ = q.shape
    lk = k.shape[1]
    kern = functools.partial(_attn_a_kernel, tq=tq, tk=tk, q_off=q_off, kv_len=kv_len)
    return pl.pallas_call(
        kern,
        grid=(b, WA // LANES, t // tq),
        in_specs=[pl.BlockSpec(memory_space=pltpu.SMEM),
                  pl.BlockSpec((None, tq, LANES), lambda bi, hi, qi: (bi, qi, hi)),
                  pl.BlockSpec((None, lk, LANES), lambda bi, hi, qi: (bi, 0, hi)),
                  pl.BlockSpec((None, lk, LANES), lambda bi, hi, qi: (bi, 0, hi))],
        out_specs=pl.BlockSpec((None, tq, LANES), lambda bi, hi, qi: (bi, qi, hi)),
        out_shape=jax.ShapeDtypeStruct((b, t, WA), F32),
        scratch_shapes=[pltpu.VMEM((4 * tq, LANES), BF16), pltpu.VMEM((4 * tq, LANES), F32),
                        pltpu.VMEM((4 * tq, LANES), F32)],
        compiler_params=_cparams(3),
        name="mixer_a",
    )(lam, q, k, v)


TIE_BLOCK = 256
SEARCH_BITS_PER_TRIP = 6


def _attn_b_kernel(q_ref, iq_ref, iw_ref, k_ref, v_ref, ik_ref, o_ref, key_ref, bias_ref,
                   *, tq, q_off, kv_len, n_sel, variants):
    q_first = q_off + pl.program_id(1) * tq
    need_keys = jnp.minimum(((q_first + tq - 1) // CHUNK + 1) * CHUNK, kv_len)
    lo = 0
    for lk in variants:
        @pl.when((need_keys > lo) & (need_keys <= lk))
        def _(lk=lk):
            _attn_b_body(q_ref, iq_ref, iw_ref, k_ref, v_ref, ik_ref, o_ref, key_ref, bias_ref,
                         tq=tq, q_first=q_first, need_keys=need_keys, kv_len=kv_len, n_sel=n_sel, lk=lk)
        lo = lk


def _stack_heads(x, n_heads, width):
    lane = lax.broadcasted_iota(jnp.int32, x.shape, 1)
    return jnp.concatenate([jnp.where(lane // width == h, x, jnp.zeros_like(x)) for h in range(n_heads)],
                           axis=0)


def _attn_b_body(q_ref, iq_ref, iw_ref, k_ref, v_ref, ik_ref, o_ref, key_ref, bias_ref,
                 *, tq, q_first, need_keys, kv_len, n_sel, lk):
    ik2 = ik_ref[0:lk, :]
    iq = iq_ref[...]
    iw = iw_ref[...]

    score = jnp.zeros((tq, lk), F32)
    for pi in range(H_I // 2):
        y = _stack_heads(iq[:, LANES * pi:LANES * (pi + 1)], 2, D_I)
        d = jnp.maximum(_dot_nt(y, ik2), 0.0)
        score = score + iw[:, 2 * pi:2 * pi + 1] * d[0:tq] + iw[:, 2 * pi + 1:2 * pi + 2] * d[tq:2 * tq]

    kpos = lax.broadcasted_iota(jnp.int32, (tq, lk), 1)
    qpos = q_first + lax.broadcasted_iota(jnp.int32, (tq, lk), 0)
    valid = (kpos // CHUNK <= qpos // CHUNK) & (kpos < kv_len)
    score = jnp.where(valid, score, NEG_BIG)

    bits = lax.bitcast_convert_type(score, jnp.int32)
    key = bits ^ ((bits >> 31) & jnp.int32(0x7FFFFFFF))
    key = jnp.where(key == -1, 0, key)
    key_ref[:, 0:lk] = key

    kf = float(n_sel)

    def count_ge(cand):
        wide = jnp.concatenate([cand] * (lk // LANES), axis=1)
        return jnp.sum(jnp.where(key_ref[:, 0:lk] >= wide, 1.0, 0.0), axis=-1, keepdims=True)

    int_min = jnp.int32(-2 ** 31)
    zero = jnp.zeros((tq, LANES), jnp.int32)
    select_all = need_keys <= n_sel
    cur = jnp.where(select_all | (count_ge(zero) < kf), int_min, zero)

    def try_bit(cur, bit):
        cand = cur | bit
        return jnp.where(count_ge(cand) >= kf, cand, cur)

    cur = jnp.where(select_all, int_min, try_bit(cur, jnp.int32(1 << 30)))

    def bits_body(i, cur):
        for j in range(SEARCH_BITS_PER_TRIP):
            cur = try_bit(cur, jnp.left_shift(jnp.int32(1), 29 - (SEARCH_BITS_PER_TRIP * i + j)))
        return cur

    thr = lax.fori_loop(0, jnp.where(select_all, 0, 30 // SEARCH_BITS_PER_TRIP), bits_body, cur)

    key = key_ref[:, 0:lk]
    thr = jnp.concatenate([thr] * (lk // LANES), axis=1)
    gt = key > thr
    eq = key == thr
    need = kf - jnp.sum(jnp.where(gt, 1.0, 0.0), axis=-1, keepdims=True)
    r_i = lax.broadcasted_iota(jnp.int32, (TIE_BLOCK, TIE_BLOCK), 0)
    c_i = lax.broadcasted_iota(jnp.int32, (TIE_BLOCK, TIE_BLOCK), 1)
    tri = jnp.where(r_i <= c_i, 1.0, 0.0).astype(BF16)
    carry = jnp.zeros((tq, 1), F32)
    for jb in range(lk // TIE_BLOCK):
        sl = slice(jb * TIE_BLOCK, (jb + 1) * TIE_BLOCK)
        eq_b = eq[:, sl]
        pref = _dot(jnp.where(eq_b, 1.0, 0.0).astype(BF16), tri) + carry
        carry = pref[:, TIE_BLOCK - 1:TIE_BLOCK]
        sel = (gt[:, sl] | (eq_b & (pref <= need))) & valid[:, sl]
        bias_ref[:, sl] = jnp.where(sel, 0.0, NEG_BIG)

    q = q_ref[...]
    lane = lax.broadcasted_iota(jnp.int32, (tq, LANES), 1)
    bias = bias_ref[:, 0:lk]
    for p in range(NPAIR):
        n_heads = min(2, H_B - 2 * p)
        sl = slice(p * LANES, (p + 1) * LANES)
        y = _stack_heads(q[:, sl], n_heads, D_B)
        s = _dot_nt(y, k_ref[0:lk, sl]).reshape(n_heads, tq, lk) + bias[None]
        m = jnp.max(s, axis=-1, keepdims=True)
        pr = jnp.exp2(s - m)
        l = jnp.sum(pr, axis=-1, keepdims=True)
        o = _dot(pr.reshape(n_heads * tq, lk).astype(BF16), v_ref[0:lk, sl]) / l.reshape(n_heads * tq, 1)
        o_ref[:, sl] = o if n_heads == 1 else jnp.where(lane < D_B, o[0:tq], o[tq:2 * tq])


def _attention_b(q, iq, iw, k, v, ik2, *, q_off, kv_len, tq, n_sel):
    b, t, _ = q.shape
    lk = k.shape[1]
    granule = tq if (tq % TIE_BLOCK == 0 and lk % (2 * tq) == 0) else lk
    variants = tuple(v for v in range(granule, lk + 1, granule) if v <= lk // 2 or v % (2 * granule) == 0 or v == lk)
    kern = functools.partial(_attn_b_kernel, tq=tq, q_off=q_off, kv_len=kv_len, n_sel=n_sel,
                             variants=variants)
    qspec = lambda w: pl.BlockSpec((None, tq, w), lambda bi, qi: (bi, qi, 0))
    kspec = lambda w: pl.BlockSpec((None, lk, w), lambda bi, qi: (bi, 0, 0))
    return pl.pallas_call(
        kern,
        grid=(b, t // tq),
        in_specs=[qspec(WBP), qspec(WIQ), qspec(LANES), kspec(WBP), kspec(WBP), kspec(2 * D_I)],
        out_specs=qspec(WBP),
        out_shape=jax.ShapeDtypeStruct((b, t, WBP), F32),
        scratch_shapes=[pltpu.VMEM((tq, lk), jnp.int32), pltpu.VMEM((tq, lk), F32)],
        compiler_params=_cparams(2),
        name="mixer_b",
    )(q, iq, iw, k, v, ik2)


HGRN2_CHUNKS_PER_STEP = 8
GROUP = SUBLANES
DECAY_SPLIT_LIMIT = 60.0


def _hgrn2_kernel(q_ref, k_ref, g_ref, v_ref, s0_ref, o_ref, s_out_ref, st_ref, *, c, chunks):
    ci = pl.program_id(1)

    @pl.when(ci == 0)
    def _():
        st_ref[...] = s0_ref[...]

    views = [tuple(r.at[pl.ds(i * c, c)] for r in (q_ref, k_ref, g_ref, v_ref, o_ref)) for i in range(chunks)]
    bound = functools.reduce(jnp.maximum, [_half_chunk_bound(view[2][...], c) for view in views])
    single_split = bound <= DECAY_SPLIT_LIMIT

    @pl.when(single_split)
    def _():
        for view in views:
            _hgrn2_chunk(*view, st_ref, c=c, single_split=True)

    @pl.when(jnp.logical_not(single_split))
    def _():
        for view in views:
            _hgrn2_chunk(*view, st_ref, c=c, single_split=False)

    @pl.when(ci == pl.num_programs(1) - 1)
    def _():
        s_out_ref[...] = st_ref[...]


def _half_chunk_bound(g, c):
    ag = jnp.abs(g)
    return jnp.maximum(jnp.max(jnp.sum(ag[0:c // 2], axis=0, keepdims=True)),
                       jnp.max(jnp.sum(ag[c // 2:c], axis=0, keepdims=True)))


def _hgrn2_chunk(q_ref, k_ref, g_ref, v_ref, o_ref, st_ref, *, c, single_split):
    q = q_ref[...]
    k = k_ref[...]
    g = g_ref[...]
    v = v_ref[...]
    w = q.shape[-1]
    row = lax.broadcasted_iota(jnp.int32, (c, w), 0)
    lane = lax.broadcasted_iota(jnp.int32, (c, w), 1)

    def segment_scans():
        cs, tots = {1: g}, {1: g}
        cum, tot, m = g, g, 1
        while m < c:
            upper = (row // m) % 2 == 1
            prev_tot = pltpu.roll(tot, m, 0)
            next_tot = pltpu.roll(tot, c - m, 0)
            cum = cum + jnp.where(upper, prev_tot, 0.0)
            tot = tot + jnp.where(upper, prev_tot, next_tot)
            m *= 2
            cs[m], tots[m] = cum, tot
        return cs, tots

    def cumsum_rows():
        cum, m = g, 1
        while m < c:
            cum = cum + jnp.where(row >= m, pltpu.roll(cum, m, 0), 0.0)
            m *= 2
        return cum

    head_masks = [lane // DK_C == h for h in range(w // DK_C)]
    n_heads = H_C
    rq = lax.broadcasted_iota(jnp.int32, (c, c), 0)
    rk = lax.broadcasted_iota(jnp.int32, (c, c), 1)

    def attend(att):
        res = _dot(att.astype(BF16), v.astype(BF16))
        out = jnp.zeros((c, w), F32)
        for h in range(n_heads):
            out = out + jnp.where(head_masks[h], res[h * c:(h + 1) * c], 0.0)
        return out

    def stack_heads(x):
        return jnp.concatenate([jnp.where(head_masks[h], x, 0.0) for h in range(n_heads)], axis=0)

    def finish(o_intra, bcum, blast):
        qe = (q * jnp.exp(bcum)).astype(BF16)
        k2 = (k * jnp.exp(blast - bcum)).astype(BF16)
        decay = jnp.exp(blast[0:1, :])
        pr = lax.broadcasted_iota(jnp.int32, (LANES, LANES), 0)
        pc = lax.broadcasted_iota(jnp.int32, (LANES, LANES), 1)
        diag = pr // DK_C == pc // DK_C
        o_state = []
        for p in range(w // LANES):
            sl = slice(p * LANES, (p + 1) * LANES)
            st = st_ref[p]
            o_state.append(_dot_nt(qe[:, sl], st.astype(BF16)))
            upd = _dot(v[:, sl].T.astype(BF16), k2[:, sl])
            st_ref[p] = st * decay[:, sl] + jnp.where(diag, upd, 0.0)
        o_ref[...] = o_intra + jnp.concatenate(o_state, axis=-1)

    if single_split:
        bcum = cumsum_rows()
        dmid = bcum - bcum[c // 2 - 1:c // 2, :]
        qt = stack_heads(q * jnp.exp(dmid)).astype(BF16)
        kt = (k * jnp.exp(-dmid)).astype(BF16)
        causal = jnp.concatenate([rq >= rk] * n_heads, axis=0)
        o_intra = attend(jnp.where(causal, _dot_nt(qt, kt), 0.0))
        finish(o_intra, bcum, jnp.broadcast_to(bcum[c - 1:c, :], (c, w)))

    else:
        cs, tots = segment_scans()
        att = jnp.zeros((n_heads * c, c), F32)
        half = GROUP
        while half < c:
            upper = (row // half) % 2 == 1
            qt = jnp.where(upper, q * jnp.exp(cs[half]), 0.0)
            kt = jnp.where(upper, 0.0, k * jnp.exp(tots[half] - cs[half])).astype(BF16)
            blk = _dot_nt(stack_heads(qt).astype(BF16), kt)
            same = (rq // (2 * half)) == (rk // (2 * half))
            same = jnp.concatenate([same] * n_heads, axis=0)
            att = att + jnp.where(same, blk, 0.0)
            half *= 2
        out = attend(att)

        def group_row(x, j):
            x3 = x.reshape(c // GROUP, GROUP, w)
            return jnp.broadcast_to(x3[:, j:j + 1, :], x3.shape).reshape(c, w)

        c8 = cs[GROUP]
        vals = []
        for j in range(GROUP):
            ok = (row % GROUP) >= j
            e = jnp.where(ok, c8 - group_row(c8, j), 0.0)
            vals.append(jnp.where(ok, q * group_row(k, j) * jnp.exp(e), 0.0))
        r_i = lax.broadcasted_iota(jnp.int32, (w, w), 0)
        c_i = lax.broadcasted_iota(jnp.int32, (w, w), 1)
        head_sum = jnp.where(r_i // DK_C == c_i // DK_C, 1.0, 0.0).astype(BF16)
        wts = _dot(jnp.concatenate(vals, axis=0).astype(BF16), head_sum)
        for j in range(GROUP):
            out = out + wts[j * c:(j + 1) * c] * group_row(v, j)
        finish(out, cs[c], tots[c])


def _hgrn2(q, k, g, v, s0, *, c):
    b, t, w = q.shape
    chunks = HGRN2_CHUNKS_PER_STEP if t % (HGRN2_CHUNKS_PER_STEP * c) == 0 else 1
    kern = functools.partial(_hgrn2_kernel, c=c, chunks=chunks)
    blk = pl.BlockSpec((None, chunks * c, w), lambda bi, ci: (bi, ci, 0))
    sblk = pl.BlockSpec((None, w // LANES, LANES, LANES), lambda bi, ci: (bi, 0, 0, 0))
    return pl.pallas_call(
        kern,
        grid=(b, t // (chunks * c)),
        in_specs=[blk, blk, blk, blk, sblk],
        out_specs=[blk, sblk],
        out_shape=[jax.ShapeDtypeStruct((b, t, w), F32),
                   jax.ShapeDtypeStruct((b, w // LANES, LANES, LANES), F32)],
        scratch_shapes=[pltpu.VMEM((w // LANES, LANES, LANES), F32)],
        compiler_params=_cparams(2),
        name="mixer_c",
    )(q, k, g, v, s0)


def _state_to_pairs(s):
    b = s.shape[0]
    st = jnp.swapaxes(s.astype(F32), -1, -2)
    st = jnp.pad(st, ((0, 0), (0, 2 * NPAIR - H_C), (0, 0), (0, 0)))
    st = st.reshape(b, NPAIR, 2, DV_C, DK_C)
    eye = jnp.eye(2, dtype=F32)
    full = st[:, :, :, :, None, :] * eye[None, None, :, None, :, None]
    return full.reshape(b, NPAIR, 2 * DV_C, 2 * DK_C)


def _pairs_to_state(sp):
    b = sp.shape[0]
    s6 = sp.reshape(b, NPAIR, 2, DV_C, 2, DK_C)
    diag = jnp.stack([s6[:, :, a, :, a, :] for a in range(2)], axis=2)
    return jnp.swapaxes(diag.reshape(b, 2 * NPAIR, DV_C, DK_C)[:, :H_C], -1, -2)


FF_BLOCK = 2816


def _head_norm(y, gain):
    w = y.shape[-1]
    r_i = lax.broadcasted_iota(jnp.int32, (w, w), 0)
    c_i = lax.broadcasted_iota(jnp.int32, (w, w), 1)
    head_sum = jnp.where(r_i // DV_A == c_i // DV_A, 1.0, 0.0).astype(BF16)
    y2 = y * y
    hi = y2.astype(BF16)
    lo = (y2 - hi.astype(F32)).astype(BF16)
    ms = (_dot(hi, head_sum) + _dot(lo, head_sum)) * (1.0 / DV_A)
    return y * lax.rsqrt(ms + EPS) * gain


def _merge_kernel(scal_ref, x_ref, oa_ref, ob_ref, oc_ref, cg_ref, hist_ref, ag_ref, cgn_ref, wo_ref,
                  n2_ref, wup_ref, cw_ref, cb_ref, wdn_ref, fn_ref, *out_and_scratch, d_ff, final):
    if final:
        x_out_ref, fc_ref, y_ref, carry_ref = out_and_scratch
    else:
        x_out_ref, fc_ref, carry_ref = out_and_scratch
    ti = pl.program_id(1)
    tm = x_ref.shape[0]

    @pl.when(ti == 0)
    def _():
        carry_ref[...] = hist_ref[...]

    oa = _head_norm(oa_ref[...], ag_ref[...]) * scal_ref[0]
    cg = cg_ref[...]
    oc = _head_norm(oc_ref[...], cgn_ref[...]) * (cg * jax.nn.sigmoid(cg))
    mixed = (_dot(oa.astype(BF16), wo_ref[0:WA, :])
             + _dot(ob_ref[...].astype(BF16), wo_ref[WA:WA + WBP, :])
             + _dot(oc.astype(BF16), wo_ref[WA + WBP:WA + 2 * WBP, :]))
    x = x_ref[...] + mixed

    ms = jnp.mean(x * x, axis=-1, keepdims=True)
    h = (x * lax.rsqrt(ms + EPS) * n2_ref[...]).astype(BF16)
    row = lax.broadcasted_iota(jnp.int32, (tm, FF_BLOCK), 0)
    acc = jnp.zeros(x.shape, F32)
    for cblk in range(d_ff // FF_BLOCK):
        sl = slice(cblk * FF_BLOCK, (cblk + 1) * FF_BLOCK)
        a = _dot(h, wup_ref[:, sl])
        gate = _dot(h, wup_ref[:, d_ff + cblk * FF_BLOCK:d_ff + (cblk + 1) * FF_BLOCK])
        prev2 = carry_ref[0:1, sl]
        prev1 = carry_ref[1:2, sl]
        a1 = jnp.where(row == 0, prev1, pltpu.roll(a, 1, 0))
        a2 = jnp.where(row == 0, prev2, jnp.where(row == 1, prev1, pltpu.roll(a, 2, 0)))
        conv = cb_ref[:, sl] + a2 * cw_ref[0:1, sl] + a1 * cw_ref[1:2, sl] + a * cw_ref[2:3, sl]
        act = conv * jax.nn.sigmoid(conv) * gate
        acc = acc + _dot(act.astype(BF16), wdn_ref[sl, :])
        carry_ref[:, sl] = a[tm - (CONV_W - 1):, :]
    x = x + acc
    x_out_ref[...] = x
    fc_ref[...] = carry_ref[...]
    if final:
        ms = jnp.mean(x * x, axis=-1, keepdims=True)
        y_ref[...] = x * lax.rsqrt(ms + EPS) * fn_ref[...]


def _merge_ffn(scal, x, oa, ob, oc, cg, hist, a_gain, c_gain, wo, n2, wup, cw, cb, wdn, fnorm, *, tm, final):
    b, t, d = x.shape
    d_ff = wdn.shape[0]
    kern = functools.partial(_merge_kernel, d_ff=d_ff, final=final)
    blk = lambda w: pl.BlockSpec((None, tm, w), lambda bi, ti: (bi, ti, 0))
    per_b = pl.BlockSpec((None, CONV_W - 1, d_ff), lambda bi, ti: (bi, 0, 0))
    out_specs = [blk(d), per_b]
    out_shape = [jax.ShapeDtypeStruct((b, t, d), F32), jax.ShapeDtypeStruct((b, CONV_W - 1, d_ff), F32)]
    if final:
        out_specs.append(blk(d))
        out_shape.append(jax.ShapeDtypeStruct((b, t, d), F32))
    return pl.pallas_call(
        kern,
        grid=(b, t // tm),
        in_specs=[pl.BlockSpec(memory_space=pltpu.SMEM), blk(d), blk(WA), blk(WBP), blk(WBP), blk(WBP), per_b,
                  _const_spec(a_gain.shape), _const_spec(c_gain.shape), _const_spec(wo.shape),
                  _const_spec(n2.shape), _const_spec(wup.shape), _const_spec(cw.shape),
                  _const_spec(cb.shape), _const_spec(wdn.shape), _const_spec(fnorm.shape)],
        out_specs=out_specs,
        out_shape=out_shape,
        scratch_shapes=[pltpu.VMEM((CONV_W - 1, d_ff), F32)],
        compiler_params=_cparams(2),
        name="merge_ffn",
    )(scal, x, oa, ob, oc, cg, hist, a_gain, c_gain, wo, n2, wup, cw, cb, wdn, fnorm)


def _pack_w_in(w):
    parts = jnp.split(w, np.cumsum(IN_SIZES)[:-1].tolist(), axis=-1)
    cols = []
    for (name, width, padded), part in zip(SEGS, parts):
        if name == "ik":
            part = jnp.concatenate([part, part], axis=-1)
            width = 2 * D_I
        cols.append(jnp.pad(part, ((0, 0), (0, padded - width))))
    return jnp.concatenate(cols, axis=-1).astype(BF16)


def _pack_w_out(w):
    wa, wb, wc = w[:WA], w[WA:WA + WB], w[WA + WB:]
    pad = lambda m: jnp.pad(m, ((0, WBP - WB), (0, 0)))
    return jnp.concatenate([wa, pad(wb), pad(wc)], axis=0).astype(BF16)


def _rope_table(pos, dim, width):
    inv_freq = ROPE_THETA ** (-jnp.arange(0, dim, 2, dtype=F32) / dim)
    ang = pos.astype(F32)[:, None] * inv_freq[None, :]
    reps = width // (dim // 2)
    return jnp.tile(jnp.cos(ang), (1, reps)), jnp.tile(jnp.sin(ang), (1, reps))


def _cache_prep_kernel(c_ref, *o_refs, past):
    for layer, o_ref in enumerate(o_refs):
        x = c_ref[layer]
        n_feat = x.shape[0]
        if n_feat < LANES:
            x = jnp.concatenate([x] * (LANES // n_feat), axis=0)
        elif n_feat % LANES:
            x = jnp.concatenate([x, jnp.zeros((LANES - n_feat % LANES, past), x.dtype)], axis=0)
        for j in range(x.shape[0] // LANES):
            o_ref[0:past, j * LANES:(j + 1) * LANES] = x[j * LANES:(j + 1) * LANES].T.astype(BF16)
        o_ref[past:, :] = jnp.zeros((o_ref.shape[0] - past, o_ref.shape[1]), BF16)


def _cache_prep(cache, rows):
    depth, batch, past = cache.shape[:3]
    n_feat = int(np.prod(cache.shape[3:]))
    nd = cache.ndim
    c = jnp.transpose(cache, (0, 1) + tuple(range(3, nd)) + (2,)).reshape(depth, batch, n_feat, past)
    width = _round_up(n_feat, LANES)
    return pl.pallas_call(
        functools.partial(_cache_prep_kernel, past=past),
        grid=(batch,),
        in_specs=[pl.BlockSpec((depth, None, n_feat, past), lambda i: (0, i, 0, 0))],
        out_specs=[pl.BlockSpec((None, rows, width), lambda i: (i, 0, 0))] * depth,
        out_shape=[jax.ShapeDtypeStruct((batch, rows, width), BF16)] * depth,
        compiler_params=_cparams(1),
        name="cache_prep",
    )(c)


def _round_up(n, m):
    return (n + m - 1) // m * m


def kernel(x_prompt, x_sample, cache_a_k, cache_a_v, cache_b_k, cache_b_v, cache_b_kidx, state_c, state_ffn_conv, norm1, w_in, lam_q1, lam_k1, lam_q2, lam_k2, a_norm, c_lower, c_norm, w_out, norm2, ffn_up, ffn_conv_w, ffn_conv_b, ffn_down, final_norm):
    depth = w_in.shape[0]
    b_p, t_p, d = x_prompt.shape
    b_s, t_s, _ = x_sample.shape
    past = cache_a_k.shape[2]
    d_ff = ffn_down.shape[1]
    kv_s = past + t_s
    n_sel_p = min(TOPK_MAX, t_p // 4)
    n_sel_s = min(TOPK_MAX, kv_s // 4)
    tk = 256
    lk_s = _round_up(kv_s, tk)

    lb_soft = jax.nn.softmax(c_lower.astype(F32), axis=0)
    lower = jnp.cumsum(lb_soft, axis=0) - lb_soft[0]
    lower = jnp.pad(lower, ((0, 0), (0, WBP - WB)))

    tm_p = min(512, t_p)
    tm_s = min(256, b_s * t_s)
    pos_p = jnp.arange(t_p)
    pos_s = jnp.tile(past + jnp.arange(t_s), tm_s // t_s)
    tabs_p = _rope_table(pos_p, DA, WA) + _rope_table(pos_p, D_B, WBP)
    tabs_s = _rope_table(pos_s, DA, WA) + _rope_table(pos_s, D_B, WBP)

    past_kv = {name: _cache_prep(c, lk_s) for name, c in
               (("ak", cache_a_k), ("av", cache_a_v), ("bk", cache_b_k), ("bv", cache_b_v), ("ik", cache_b_kidx))}

    xp, xs = x_prompt, x_sample
    caches_p, state_p, conv_p = None, [], []
    outs_s = [[] for _ in range(7)]
    y_p = y_s = None
    fnorm = final_norm.reshape(1, d)
    for l in range(depth):
        lam_init = 0.8 - 0.6 * math.exp(-0.3 * l)
        lam = (jnp.exp(jnp.sum(lam_q1[l].astype(F32) * lam_k1[l].astype(F32)))
               - jnp.exp(jnp.sum(lam_q2[l].astype(F32) * lam_k2[l].astype(F32))) + lam_init)
        lam_arr = lam.reshape(1).astype(F32)
        scal = jnp.full((1,), 1.0 - lam_init, F32)
        w_pack = _pack_w_in(w_in[l])
        wo = _pack_w_out(w_out[l])
        gain1 = norm1[l].reshape(1, d)
        gain2 = norm2[l].reshape(1, d)
        lb_row = lower[l].reshape(1, WBP)
        a_gain = jnp.tile(a_norm[l], H_A).reshape(1, WA)
        c_gain = jnp.pad(jnp.tile(c_norm[l], H_C), (0, WBP - WB)).reshape(1, WBP)
        wup = ffn_up[l].astype(BF16)
        wdn = ffn_down[l].astype(BF16)
        cw = ffn_conv_w[l]
        cb = ffn_conv_b[l].reshape(1, d_ff)
        final = l == depth - 1

        u = _in_projection(xp.reshape(b_p * t_p, d), gain1, w_pack, lb_row, tabs_p, tm_p,
                           stacked=(l, depth, b_p, caches_p))
        caches_p = {name: u[name] for name in CACHE_OUTS}
        r3 = lambda a, b=b_p, t=t_p: a.reshape(b, t, a.shape[-1])
        oa = _attention_a(lam_arr, r3(u["aq"]), r3(u["ak16"]), r3(u["av16"]),
                          q_off=0, kv_len=t_p, tq=min(256, t_p), tk=min(512, t_p))
        ob = _attention_b(r3(u["bq"]), r3(u["iq"]), r3(u["iw"]), r3(u["bk16"]), r3(u["bv16"]), r3(u["ik2"]),
                          q_off=0, kv_len=t_p, tq=min(256, t_p), n_sel=n_sel_p)
        s0 = jnp.zeros((b_p, NPAIR, LANES, LANES), F32)
        oc, s_new = _hgrn2(r3(u["cq"]), r3(u["ck"]), r3(u["cgl"]), r3(u["cv"]), s0, c=CHUNK)
        hist0 = jnp.zeros((b_p, CONV_W - 1, d_ff), F32)
        res = _merge_ffn(scal, xp, oa, ob, oc, r3(u["cg"]), hist0, a_gain, c_gain, wo, gain2, wup, cw, cb,
                         wdn, fnorm, tm=min(512, t_p), final=final)
        xp, fc = res[0], res[1]
        if final:
            y_p = res[2]
        state_p.append(_pairs_to_state(s_new))
        conv_p.append(fc)

        u = _in_projection(xs.reshape(b_s * t_s, d), gain1, w_pack, lb_row, tabs_s, tm_s)
        r3 = lambda a, b=b_s, t=t_s: a.reshape(b, t, a.shape[-1])

        def cat(name, new):
            return lax.dynamic_update_slice(past_kv[name][l], r3(new), (0, past, 0))

        oa = _attention_a(lam_arr, r3(u["aq"]), cat("ak", u["ak16"]), cat("av", u["av16"]),
                          q_off=past, kv_len=kv_s, tq=t_s, tk=tk)
        ob = _attention_b(r3(u["bq"]), r3(u["iq"]), r3(u["iw"]), cat("bk", u["bk16"]),
                          cat("bv", u["bv16"]), cat("ik", u["ik2"]),
                          q_off=past, kv_len=kv_s, tq=t_s, n_sel=n_sel_s)
        oc, s_new = _hgrn2(r3(u["cq"]), r3(u["ck"]), r3(u["cgl"]), r3(u["cv"]), _state_to_pairs(state_c[l]),
                           c=t_s)
        res = _merge_ffn(scal, xs, oa, ob, oc, r3(u["cg"]), state_ffn_conv[l].astype(F32), a_gain, c_gain, wo,
                         gain2, wup, cw, cb, wdn, fnorm, tm=t_s, final=final)
        xs, fc = res[0], res[1]
        if final:
            y_s = res[2]
        for lst, val in zip(outs_s, (u["ak"].reshape(b_s, t_s, H_A, 2 * DA), u["av"].reshape(b_s, t_s, H_A, DV_A),
                                     u["bk"].reshape(b_s, t_s, H_B, D_B), u["bv"].reshape(b_s, t_s, H_B, D_B),
                                     u["ik"].reshape(b_s, t_s, D_I), _pairs_to_state(s_new), fc)):
            lst.append(val)

    def frames_major(c, heads):
        c = c.reshape(depth, b_p, heads, c.shape[2] // heads, t_p)
        return jnp.transpose(c, (0, 1, 4, 2, 3))

    outs_p = (frames_major(caches_p["ak"], H_A), frames_major(caches_p["av"], H_A),
              frames_major(caches_p["bk"], H_B), frames_major(caches_p["bv"], H_B),
              jnp.swapaxes(caches_p["ik"], 2, 3), jnp.stack(state_p), jnp.stack(conv_p))
    return (y_p, y_s) + outs_p + tuple(jnp.stack(v) for v in outs_s)
```

```python
import functools
import math

import jax
import jax.numpy as jnp
import numpy as np
from jax import lax
from jax.experimental import pallas as pl
from jax.experimental.pallas import tpu as pltpu

F32 = jnp.float32
BF16 = jnp.bfloat16

CHUNK = 64
ROPE_THETA = 10000.0
EPS = 1e-6
NEG_BIG = -1e30
LB_FLOOR = 1e-20
H_A, DA, DV_A = 6, 32, 64
H_B, D_B = 5, 64
H_I, D_I = 4, 64
TOPK_MAX = 256
H_C, DK_C, DV_C = 5, 64, 64
CONV_W = 3

LANES = 128
SUBLANES = 8
VMEM_LIMIT = 56 * 1024 * 1024

WA = H_A * 2 * DA
WB = H_B * D_B
WBP = 384
WIQ = H_I * D_I
NPAIR = WBP // LANES

SEGS = (("aq", WA, WA), ("ak", WA, WA), ("av", WA, WA),
        ("bq", WB, WBP), ("bk", WB, WBP), ("bv", WB, WBP),
        ("iq", WIQ, WIQ), ("ik", D_I, 2 * D_I), ("iw", H_I, LANES),
        ("cq", WB, WBP), ("cf", WB, WBP), ("ci", WB, WBP), ("cg", WB, WBP))
SEG_OFF = {}
_o = 0
for _n, _w, _p in SEGS:
    SEG_OFF[_n] = (_o, _p)
    _o += _p
W_PACK = _o
PROJ_GROUPS = tuple((SEG_OFF[first][0], SEG_OFF[last][0] + SEG_OFF[last][1] - SEG_OFF[first][0])
                    for first, last in (("aq", "ak"), ("av", "bq"), ("bk", "bv"), ("iq", "iw"),
                                        ("cq", "cf"), ("ci", "cg")))
IN_SIZES = (WA, WA, WA, WB, WB, WB, WIQ, D_I, H_I, WB, WB, WB, WB)

LOG2E = math.log2(math.e)
NT_DIMS = (((1,), (1,)), ((), ()))


def _cparams(n_axes):
    return pltpu.CompilerParams(dimension_semantics=("arbitrary",) * n_axes,
                                vmem_limit_bytes=VMEM_LIMIT)


def _const_spec(shape):
    nd = len(shape)
    return pl.BlockSpec(shape, lambda *_: (0,) * nd, pipeline_mode=pl.Buffered(1))


def _dot(a, b):
    return jnp.dot(a, b, preferred_element_type=F32)


def _dot_nt(a, b):
    return lax.dot_general(a, b, NT_DIMS, preferred_element_type=F32)


def _rope(u, cos, sin, half):
    w = u.shape[-1]
    lane = lax.broadcasted_iota(jnp.int32, u.shape, 1)
    first = (lane % (2 * half)) < half
    rot = jnp.where(first, -pltpu.roll(u, w - half, 1), pltpu.roll(u, half, 1))
    return u * cos + rot * sin


N_PROJ_INPUTS = 8


def _inproj_kernel(*refs, n_aliased, feature_major_caches):
    x_ref, g_ref, w_ref, lb_ref, ca_ref, sa_ref, cb_ref, sb_ref = refs[:N_PROJ_INPUTS]
    (aq_ref, ak_ref, av_ref, ak16_ref, av16_ref, bq_ref, bk_ref, bv_ref, bk16_ref, bv16_ref,
     iq_ref, ik2_ref, ik_ref, iw_ref, cq_ref, ck_ref, cgl_ref, cv_ref, cg_ref) = refs[N_PROJ_INPUTS + n_aliased:]

    def put_cache(ref, val, width):
        ref[...] = val.T[:width, :] if feature_major_caches else val[:, :width]

    x = x_ref[...]
    ms = jnp.mean(x * x, axis=-1, keepdims=True)
    xn = (x * lax.rsqrt(ms + EPS) * g_ref[...]).astype(BF16)

    group_dots = {}

    def seg(name):
        off, width = SEG_OFF[name]
        g_off, g_width = next((o, wd) for o, wd in PROJ_GROUPS if o <= off < o + wd)
        if g_off not in group_dots:
            group_dots[g_off] = _dot(xn, w_ref[:, g_off:g_off + g_width])
        return group_dots[g_off][:, off - g_off:off - g_off + width]

    ca, sa = ca_ref[...], sa_ref[...]
    cb, sb = cb_ref[...], sb_ref[...]
    aq_ref[...] = (_rope(seg("aq"), ca, sa, DA // 2) * (DA ** -0.5 * LOG2E)).astype(BF16)
    ak = _rope(seg("ak"), ca, sa, DA // 2)
    put_cache(ak_ref, ak, WA)
    ak16_ref[...] = ak.astype(BF16)
    av = seg("av")
    put_cache(av_ref, av, WA)
    av16_ref[...] = av.astype(BF16)
    bq_ref[...] = (_rope(seg("bq"), cb, sb, D_B // 2) * (D_B ** -0.5 * LOG2E)).astype(BF16)
    bk = _rope(seg("bk"), cb, sb, D_B // 2)
    put_cache(bk_ref, bk, WB)
    bk16_ref[...] = bk.astype(BF16)
    bv = seg("bv")
    put_cache(bv_ref, bv, WB)
    bv16_ref[...] = bv.astype(BF16)
    iq_ref[...] = (_rope(seg("iq"), cb[:, :WIQ], sb[:, :WIQ], D_I // 2) * (D_I ** -0.5)).astype(BF16)
    ik2 = _rope(seg("ik"), cb[:, :2 * D_I], sb[:, :2 * D_I], D_I // 2)
    ik2_ref[...] = ik2.astype(BF16)
    put_cache(ik_ref, ik2, D_I)
    iw_ref[...] = seg("iw") * (H_I ** -0.5)

    cq = seg("cq")
    cq_ref[...] = cq * jax.nn.sigmoid(cq)
    z = seg("cf")
    lb = lb_ref[...]
    la = jnp.log(jnp.maximum(lb, LB_FLOOR))
    lsig = jnp.minimum(z, 0.0) - jnp.log1p(jnp.exp(-jnp.abs(z)))
    bb = jnp.log1p(-lb) + lsig
    cgl_ref[...] = jnp.maximum(la, bb) + jnp.log1p(jnp.exp(-jnp.abs(la - bb)))
    ck_ref[...] = (1.0 - lb) * jax.nn.sigmoid(-z)
    cv_ref[...] = seg("ci")
    cg_ref[...] = seg("cg")


CACHE_OUTS = ("ak", "av", "bk", "bv", "ik")


def _in_projection(x2d, gain, w_pack, lb_row, tabs, tm, stacked=None):
    n = x2d.shape[0]
    ca, sa, cb, sb = tabs
    period = ca.shape[0] // tm
    row = lambda w: pl.BlockSpec((tm, w), lambda i: (i, 0))
    tab = lambda w: pl.BlockSpec((tm, w), lambda i: (i % period, 0))
    outs = (("aq", WA, BF16), ("ak", WA, F32), ("av", WA, F32), ("ak16", WA, BF16), ("av16", WA, BF16),
            ("bq", WBP, BF16), ("bk", WB, F32), ("bv", WB, F32), ("bk16", WBP, BF16), ("bv16", WBP, BF16),
            ("iq", WIQ, BF16), ("ik2", 2 * D_I, BF16), ("ik", D_I, F32), ("iw", LANES, F32),
            ("cq", WBP, F32), ("ck", WBP, F32), ("cgl", WBP, F32), ("cv", WBP, F32), ("cg", WBP, F32))
    out_specs = {name: row(w) for name, w, _ in outs}
    out_shape = {name: jax.ShapeDtypeStruct((n, w), dt) for name, w, dt in outs}
    inputs = [x2d, gain, w_pack, lb_row, ca, sa, cb, sb]
    in_specs = [row(x2d.shape[1]), _const_spec(gain.shape), _const_spec(w_pack.shape),
                _const_spec(lb_row.shape), tab(WA), tab(WA), tab(WBP), tab(WBP)]
    aliases = {}
    if stacked is not None:
        layer, depth, batch, caches = stacked
        t = n // batch
        tiles = t // tm
        for name, w, _ in outs:
            if name in CACHE_OUTS:
                out_specs[name] = pl.BlockSpec((None, None, w, tm),
                                               lambda i, layer=layer: (layer, i // tiles, 0, i % tiles))
                out_shape[name] = jax.ShapeDtypeStruct((depth, batch, w, t), F32)
        if caches is not None:
            names = [o[0] for o in outs]
            for name in CACHE_OUTS:
                aliases[len(inputs)] = names.index(name)
                inputs.append(caches[name])
                in_specs.append(pl.BlockSpec(memory_space=pl.ANY))
    kern = functools.partial(_inproj_kernel, n_aliased=len(aliases), feature_major_caches=stacked is not None)
    res = pl.pallas_call(
        kern,
        grid=(n // tm,),
        in_specs=in_specs,
        out_specs=[out_specs[o[0]] for o in outs],
        out_shape=[out_shape[o[0]] for o in outs],
        input_output_aliases=aliases,
        compiler_params=_cparams(1),
        name="in_projection",
    )(*inputs)
    return dict(zip([o[0] for o in outs], res))


NARROW_KEY_BLOCK = 256


def _attn_a_kernel(lam_ref, q_ref, k_ref, v_ref, o_ref, q4_ref, m_ref, acc_ref, *, tq, tk, q_off, kv_len):
    qi = pl.program_id(2)
    q = q_ref[...]
    lane = lax.broadcasted_iota(jnp.int32, q.shape, 1)
    for i in range(4):
        q4_ref[i * tq:(i + 1) * tq, :] = jnp.where(lane // DA == i, q, jnp.zeros_like(q))
    m_ref[...] = jnp.full(m_ref.shape, NEG_BIG, F32)
    acc_ref[...] = jnp.zeros(acc_ref.shape, F32)

    q_first = q_off + qi * tq
    q_last = q_first + tq - 1
    n_full = jnp.minimum((q_first // CHUNK + 1) * CHUNK, kv_len) // tk
    lim = jnp.minimum((q_last // CHUNK + 1) * CHUNK, kv_len)
    tkn = min(tk, NARROW_KEY_BLOCK)
    n_narrow = (lim + tkn - 1) // tkn
    n_wide = n_full + (lim - n_full * tk) // tk if tkn < tk else n_narrow

    def step(j, tk, masked):
        start = pl.multiple_of(j * tk, tk)
        kb = k_ref[pl.ds(start, tk), :]
        vb = v_ref[pl.ds(start, tk), :]
        s = _dot_nt(q4_ref[...], kb)
        if masked:
            kpos = start + lax.broadcasted_iota(jnp.int32, (tq, tk), 1)
            qpos = q_first + lax.broadcasted_iota(jnp.int32, (tq, tk), 0)
            valid = (kpos // CHUNK <= qpos // CHUNK) & (kpos < kv_len)
            s = jnp.where(valid[None], s.reshape(4, tq, tk), NEG_BIG).reshape(4 * tq, tk)
        m_old = m_ref[...]
        m_new = jnp.maximum(m_old, jnp.max(s, axis=-1, keepdims=True))
        alpha = jnp.exp2(m_old - m_new)
        p = jnp.exp2(s - jnp.concatenate([m_new] * (tk // LANES), axis=1)).astype(BF16)
        lane_v = lax.broadcasted_iota(jnp.int32, vb.shape, 1)
        ones = jnp.ones_like(vb)
        pv = jnp.concatenate([_dot(p[0:2 * tq], jnp.where(lane_v < DV_A, vb, ones)),
                              _dot(p[2 * tq:4 * tq], jnp.where(lane_v < DV_A, ones, vb))], axis=0)
        acc_ref[...] = alpha * acc_ref[...] + pv
        m_ref[...] = m_new

    def body(tk, masked):
        def run(j, c):
            step(j, tk, masked)
            return c
        return run

    lax.fori_loop(0, n_full, body(tk, False), 0)
    lax.fori_loop(n_full, n_wide, body(tk, True), 0)
    if tkn < tk:
        lax.fori_loop(n_wide * (tk // tkn), n_narrow, body(tkn, True), 0)

    lam = lam_ref[0]
    acc0 = acc_ref[0:2 * tq, :]
    acc1 = acc_ref[2 * tq:4 * tq, :]
    on0 = acc0 / acc0[:, DV_A:DV_A + 1]
    on1 = acc1 / acc1[:, 0:1]
    o0 = on0[0:tq] - lam * on0[tq:2 * tq]
    o1 = on1[0:tq] - lam * on1[tq:2 * tq]
    o_ref[...] = jnp.where(lane < DV_A, o0, o1)


def _attention_a(lam, q, k, v, *, q_off, kv_len, tq, tk):
    b, t, _ = q.shape
    lk = k.shape[1]
    kern = functools.partial(_attn_a_kernel, tq=tq, tk=tk, q_off=q_off, kv_len=kv_len)
    return pl.pallas_call(
        kern,
        grid=(b, WA // LANES, t // tq),
        in_specs=[pl.BlockSpec(memory_space=pltpu.SMEM),
                  pl.BlockSpec((None, tq, LANES), lambda bi, hi, qi: (bi, qi, hi)),
                  pl.BlockSpec((None, lk, LANES), lambda bi, hi, qi: (bi, 0, hi)),
                  pl.BlockSpec((None, lk, LANES), lambda bi, hi, qi: (bi, 0, hi))],
        out_specs=pl.BlockSpec((None, tq, LANES), lambda bi, hi, qi: (bi, qi, hi)),
        out_shape=jax.ShapeDtypeStruct((b, t, WA), F32),
        scratch_shapes=[pltpu.VMEM((4 * tq, LANES), BF16), pltpu.VMEM((4 * tq, LANES), F32),
                        pltpu.VMEM((4 * tq, LANES), F32)],
        compiler_params=_cparams(3),
        name="mixer_a",
    )(lam, q, k, v)


TIE_BLOCK = 256
SEARCH_BITS_PER_TRIP = 6


def _attn_b_kernel(q_ref, iq_ref, iw_ref, k_ref, v_ref, ik_ref, o_ref, key_ref, bias_ref,
                   *, tq, q_off, kv_len, n_sel, variants):
    q_first = q_off + pl.program_id(1) * tq
    need_keys = jnp.minimum(((q_first + tq - 1) // CHUNK + 1) * CHUNK, kv_len)
    lo = 0
    for lk in variants:
        @pl.when((need_keys > lo) & (need_keys <= lk))
        def _(lk=lk):
            _attn_b_body(q_ref, iq_ref, iw_ref, k_ref, v_ref, ik_ref, o_ref, key_ref, bias_ref,
                         tq=tq, q_first=q_first, need_keys=need_keys, kv_len=kv_len, n_sel=n_sel, lk=lk)
        lo = lk


def _stack_heads(x, n_heads, width):
    lane = lax.broadcasted_iota(jnp.int32, x.shape, 1)
    return jnp.concatenate([jnp.where(lane // width == h, x, jnp.zeros_like(x)) for h in range(n_heads)],
                           axis=0)


def _attn_b_body(q_ref, iq_ref, iw_ref, k_ref, v_ref, ik_ref, o_ref, key_ref, bias_ref,
                 *, tq, q_first, need_keys, kv_len, n_sel, lk):
    ik2 = ik_ref[0:lk, :]
    iq = iq_ref[...]
    iw = iw_ref[...]

    score = jnp.zeros((tq, lk), F32)
    for pi in range(H_I // 2):
        y = _stack_heads(iq[:, LANES * pi:LANES * (pi + 1)], 2, D_I)
        d = jnp.maximum(_dot_nt(y, ik2), 0.0)
        score = score + iw[:, 2 * pi:2 * pi + 1] * d[0:tq] + iw[:, 2 * pi + 1:2 * pi + 2] * d[tq:2 * tq]

    kpos = lax.broadcasted_iota(jnp.int32, (tq, lk), 1)
    qpos = q_first + lax.broadcasted_iota(jnp.int32, (tq, lk), 0)
    valid = (kpos // CHUNK <= qpos // CHUNK) & (kpos < kv_len)
    score = jnp.where(valid, score, NEG_BIG)

    bits = lax.bitcast_convert_type(score, jnp.int32)
    key = bits ^ ((bits >> 31) & jnp.int32(0x7FFFFFFF))
    key = jnp.where(key == -1, 0, key)
    key_ref[:, 0:lk] = key

    kf = float(n_sel)

    def count_ge(cand):
        wide = jnp.concatenate([cand] * (lk // LANES), axis=1)
        return jnp.sum(jnp.where(key_ref[:, 0:lk] >= wide, 1.0, 0.0), axis=-1, keepdims=True)

    int_min = jnp.int32(-2 ** 31)
    zero = jnp.zeros((tq, LANES), jnp.int32)
    select_all = need_keys <= n_sel
    cur = jnp.where(select_all | (count_ge(zero) < kf), int_min, zero)

    def try_bit(cur, bit):
        cand = cur | bit
        return jnp.where(count_ge(cand) >= kf, cand, cur)

    cur = jnp.where(select_all, int_min, try_bit(cur, jnp.int32(1 << 30)))

    def bits_body(i, cur):
        for j in range(SEARCH_BITS_PER_TRIP):
            cur = try_bit(cur, jnp.left_shift(jnp.int32(1), 29 - (SEARCH_BITS_PER_TRIP * i + j)))
        return cur

    thr = lax.fori_loop(0, jnp.where(select_all, 0, 30 // SEARCH_BITS_PER_TRIP), bits_body, cur)

    key = key_ref[:, 0:lk]
    thr = jnp.concatenate([thr] * (lk // LANES), axis=1)
    gt = key > thr
    eq = key == thr
    need = kf - jnp.sum(jnp.where(gt, 1.0, 0.0), axis=-1, keepdims=True)
    r_i = lax.broadcasted_iota(jnp.int32, (TIE_BLOCK, TIE_BLOCK), 0)
    c_i = lax.broadcasted_iota(jnp.int32, (TIE_BLOCK, TIE_BLOCK), 1)
    tri = jnp.where(r_i <= c_i, 1.0, 0.0).astype(BF16)
    carry = jnp.zeros((tq, 1), F32)
    for jb in range(lk // TIE_BLOCK):
        sl = slice(jb * TIE_BLOCK, (jb + 1) * TIE_BLOCK)
        eq_b = eq[:, sl]
        pref = _dot(jnp.where(eq_b, 1.0, 0.0).astype(BF16), tri) + carry
        carry = pref[:, TIE_BLOCK - 1:TIE_BLOCK]
        sel = (gt[:, sl] | (eq_b & (pref <= need))) & valid[:, sl]
        bias_ref[:, sl] = jnp.where(sel, 0.0, NEG_BIG)

    q = q_ref[...]
    lane = lax.broadcasted_iota(jnp.int32, (tq, LANES), 1)
    bias = bias_ref[:, 0:lk]
    for p in range(NPAIR):
        n_heads = min(2, H_B - 2 * p)
        sl = slice(p * LANES, (p + 1) * LANES)
        y = _stack_heads(q[:, sl], n_heads, D_B)
        s = _dot_nt(y, k_ref[0:lk, sl]).reshape(n_heads, tq, lk) + bias[None]
        m = jnp.max(s, axis=-1, keepdims=True)
        pr = jnp.exp2(s - m)
        l = jnp.sum(pr, axis=-1, keepdims=True)
        o = _dot(pr.reshape(n_heads * tq, lk).astype(BF16), v_ref[0:lk, sl]) / l.reshape(n_heads * tq, 1)
        o_ref[:, sl] = o if n_heads == 1 else jnp.where(lane < D_B, o[0:tq], o[tq:2 * tq])


def _attention_b(q, iq, iw, k, v, ik2, *, q_off, kv_len, tq, n_sel):
    b, t, _ = q.shape
    lk = k.shape[1]
    granule = tq if (tq % TIE_BLOCK == 0 and lk % (2 * tq) == 0) else lk
    variants = tuple(v for v in range(granule, lk + 1, granule) if v <= lk // 2 or v % (2 * granule) == 0 or v == lk)
    kern = functools.partial(_attn_b_kernel, tq=tq, q_off=q_off, kv_len=kv_len, n_sel=n_sel,
                             variants=variants)
    qspec = lambda w: pl.BlockSpec((None, tq, w), lambda bi, qi: (bi, qi, 0))
    kspec = lambda w: pl.BlockSpec((None, lk, w), lambda bi, qi: (bi, 0, 0))
    return pl.pallas_call(
        kern,
        grid=(b, t // tq),
        in_specs=[qspec(WBP), qspec(WIQ), qspec(LANES), kspec(WBP), kspec(WBP), kspec(2 * D_I)],
        out_specs=qspec(WBP),
        out_shape=jax.ShapeDtypeStruct((b, t, WBP), F32),
        scratch_shapes=[pltpu.VMEM((tq, lk), jnp.int32), pltpu.VMEM((tq, lk), F32)],
        compiler_params=_cparams(2),
        name="mixer_b",
    )(q, iq, iw, k, v, ik2)


HGRN2_CHUNKS_PER_STEP = 8
GROUP = SUBLANES
DECAY_SPLIT_LIMIT = 60.0


def _hgrn2_kernel(q_ref, k_ref, g_ref, v_ref, s0_ref, o_ref, s_out_ref, st_ref, *, c, chunks):
    ci = pl.program_id(1)

    @pl.when(ci == 0)
    def _():
        st_ref[...] = s0_ref[...]

    views = [tuple(r.at[pl.ds(i * c, c)] for r in (q_ref, k_ref, g_ref, v_ref, o_ref)) for i in range(chunks)]
    bound = functools.reduce(jnp.maximum, [_half_chunk_bound(view[2][...], c) for view in views])
    single_split = bound <= DECAY_SPLIT_LIMIT

    @pl.when(single_split)
    def _():
        for view in views:
            _hgrn2_chunk(*view, st_ref, c=c, single_split=True)

    @pl.when(jnp.logical_not(single_split))
    def _():
        for view in views:
            _hgrn2_chunk(*view, st_ref, c=c, single_split=False)

    @pl.when(ci == pl.num_programs(1) - 1)
    def _():
        s_out_ref[...] = st_ref[...]


def _half_chunk_bound(g, c):
    ag = jnp.abs(g)
    return jnp.maximum(jnp.max(jnp.sum(ag[0:c // 2], axis=0, keepdims=True)),
                       jnp.max(jnp.sum(ag[c // 2:c], axis=0, keepdims=True)))


def _hgrn2_chunk(q_ref, k_ref, g_ref, v_ref, o_ref, st_ref, *, c, single_split):
    q = q_ref[...]
    k = k_ref[...]
    g = g_ref[...]
    v = v_ref[...]
    w = q.shape[-1]
    row = lax.broadcasted_iota(jnp.int32, (c, w), 0)
    lane = lax.broadcasted_iota(jnp.int32, (c, w), 1)

    def segment_scans():
        cs, tots = {1: g}, {1: g}
        cum, tot, m = g, g, 1
        while m < c:
            upper = (row // m) % 2 == 1
            prev_tot = pltpu.roll(tot, m, 0)
            next_tot = pltpu.roll(tot, c - m, 0)
            cum = cum + jnp.where(upper, prev_tot, 0.0)
            tot = tot + jnp.where(upper, prev_tot, next_tot)
            m *= 2
            cs[m], tots[m] = cum, tot
        return cs, tots

    def cumsum_rows():
        cum, m = g, 1
        while m < c:
            cum = cum + jnp.where(row >= m, pltpu.roll(cum, m, 0), 0.0)
            m *= 2
        return cum

    head_masks = [lane // DK_C == h for h in range(w // DK_C)]
    n_heads = H_C
    rq = lax.broadcasted_iota(jnp.int32, (c, c), 0)
    rk = lax.broadcasted_iota(jnp.int32, (c, c), 1)

    def attend(att):
        res = _dot(att.astype(BF16), v.astype(BF16))
        out = jnp.zeros((c, w), F32)
        for h in range(n_heads):
            out = out + jnp.where(head_masks[h], res[h * c:(h + 1) * c], 0.0)
        return out

    def stack_heads(x):
        return jnp.concatenate([jnp.where(head_masks[h], x, 0.0) for h in range(n_heads)], axis=0)

    def finish(o_intra, bcum, blast):
        qe = (q * jnp.exp(bcum)).astype(BF16)
        k2 = (k * jnp.exp(blast - bcum)).astype(BF16)
        decay = jnp.exp(blast[0:1, :])
        pr = lax.broadcasted_iota(jnp.int32, (LANES, LANES), 0)
        pc = lax.broadcasted_iota(jnp.int32, (LANES, LANES), 1)
        diag = pr // DK_C == pc // DK_C
        o_state = []
        for p in range(w // LANES):
            sl = slice(p * LANES, (p + 1) * LANES)
            st = st_ref[p]
            o_state.append(_dot_nt(qe[:, sl], st.astype(BF16)))
            upd = _dot(v[:, sl].T.astype(BF16), k2[:, sl])
            st_ref[p] = st * decay[:, sl] + jnp.where(diag, upd, 0.0)
        o_ref[...] = o_intra + jnp.concatenate(o_state, axis=-1)

    if single_split:
        bcum = cumsum_rows()
        dmid = bcum - bcum[c // 2 - 1:c // 2, :]
        qt = stack_heads(q * jnp.exp(dmid)).astype(BF16)
        kt = (k * jnp.exp(-dmid)).astype(BF16)
        causal = jnp.concatenate([rq >= rk] * n_heads, axis=0)
        o_intra = attend(jnp.where(causal, _dot_nt(qt, kt), 0.0))
        finish(o_intra, bcum, jnp.broadcast_to(bcum[c - 1:c, :], (c, w)))

    else:
        cs, tots = segment_scans()
        att = jnp.zeros((n_heads * c, c), F32)
        half = GROUP
        while half < c:
            upper = (row // half) % 2 == 1
            qt = jnp.where(upper, q * jnp.exp(cs[half]), 0.0)
            kt = jnp.where(upper, 0.0, k * jnp.exp(tots[half] - cs[half])).astype(BF16)
            blk = _dot_nt(stack_heads(qt).astype(BF16), kt)
            same = (rq // (2 * half)) == (rk // (2 * half))
            same = jnp.concatenate([same] * n_heads, axis=0)
            att = att + jnp.where(same, blk, 0.0)
            half *= 2
        out = attend(att)

        def group_row(x, j):
            x3 = x.reshape(c // GROUP, GROUP, w)
            return jnp.broadcast_to(x3[:, j:j + 1, :], x3.shape).reshape(c, w)

        c8 = cs[GROUP]
        vals = []
        for j in range(GROUP):
            ok = (row % GROUP) >= j
            e = jnp.where(ok, c8 - group_row(c8, j), 0.0)
            vals.append(jnp.where(ok, q * group_row(k, j) * jnp.exp(e), 0.0))
        r_i = lax.broadcasted_iota(jnp.int32, (w, w), 0)
        c_i = lax.broadcasted_iota(jnp.int32, (w, w), 1)
        head_sum = jnp.where(r_i // DK_C == c_i // DK_C, 1.0, 0.0).astype(BF16)
        wts = _dot(jnp.concatenate(vals, axis=0).astype(BF16), head_sum)
        for j in range(GROUP):
            out = out + wts[j * c:(j + 1) * c] * group_row(v, j)
        finish(out, cs[c], tots[c])


def _hgrn2(q, k, g, v, s0, *, c):
    b, t, w = q.shape
    chunks = HGRN2_CHUNKS_PER_STEP if t % (HGRN2_CHUNKS_PER_STEP * c) == 0 else 1
    kern = functools.partial(_hgrn2_kernel, c=c, chunks=chunks)
    blk = pl.BlockSpec((None, chunks * c, w), lambda bi, ci: (bi, ci, 0))
    sblk = pl.BlockSpec((None, w // LANES, LANES, LANES), lambda bi, ci: (bi, 0, 0, 0))
    return pl.pallas_call(
        kern,
        grid=(b, t // (chunks * c)),
        in_specs=[blk, blk, blk, blk, sblk],
        out_specs=[blk, sblk],
        out_shape=[jax.ShapeDtypeStruct((b, t, w), F32),
                   jax.ShapeDtypeStruct((b, w // LANES, LANES, LANES), F32)],
        scratch_shapes=[pltpu.VMEM((w // LANES, LANES, LANES), F32)],
        compiler_params=_cparams(2),
        name="mixer_c",
    )(q, k, g, v, s0)


def _state_to_pairs(s):
    b = s.shape[0]
    st = jnp.swapaxes(s.astype(F32), -1, -2)
    st = jnp.pad(st, ((0, 0), (0, 2 * NPAIR - H_C), (0, 0), (0, 0)))
    st = st.reshape(b, NPAIR, 2, DV_C, DK_C)
    eye = jnp.eye(2, dtype=F32)
    full = st[:, :, :, :, None, :] * eye[None, None, :, None, :, None]
    return full.reshape(b, NPAIR, 2 * DV_C, 2 * DK_C)


def _pairs_to_state(sp):
    b = sp.shape[0]
    s6 = sp.reshape(b, NPAIR, 2, DV_C, 2, DK_C)
    diag = jnp.stack([s6[:, :, a, :, a, :] for a in range(2)], axis=2)
    return jnp.swapaxes(diag.reshape(b, 2 * NPAIR, DV_C, DK_C)[:, :H_C], -1, -2)


FF_BLOCK = 2816


def _head_norm(y, gain):
    w = y.shape[-1]
    r_i = lax.broadcasted_iota(jnp.int32, (w, w), 0)
    c_i = lax.broadcasted_iota(jnp.int32, (w, w), 1)
    head_sum = jnp.where(r_i // DV_A == c_i // DV_A, 1.0, 0.0).astype(BF16)
    y2 = y * y
    hi = y2.astype(BF16)
    lo = (y2 - hi.astype(F32)).astype(BF16)
    ms = (_dot(hi, head_sum) + _dot(lo, head_sum)) * (1.0 / DV_A)
    return y * lax.rsqrt(ms + EPS) * gain


def _merge_kernel(scal_ref, x_ref, oa_ref, ob_ref, oc_ref, cg_ref, hist_ref, ag_ref, cgn_ref, wo_ref,
                  n2_ref, wup_ref, cw_ref, cb_ref, wdn_ref, fn_ref, *out_and_scratch, d_ff, final):
    if final:
        x_out_ref, fc_ref, y_ref, carry_ref = out_and_scratch
    else:
        x_out_ref, fc_ref, carry_ref = out_and_scratch
    ti = pl.program_id(1)
    tm = x_ref.shape[0]

    @pl.when(ti == 0)
    def _():
        carry_ref[...] = hist_ref[...]

    oa = _head_norm(oa_ref[...], ag_ref[...]) * scal_ref[0]
    cg = cg_ref[...]
    oc = _head_norm(oc_ref[...], cgn_ref[...]) * (cg * jax.nn.sigmoid(cg))
    mixed = (_dot(oa.astype(BF16), wo_ref[0:WA, :])
             + _dot(ob_ref[...].astype(BF16), wo_ref[WA:WA + WBP, :])
             + _dot(oc.astype(BF16), wo_ref[WA + WBP:WA + 2 * WBP, :]))
    x = x_ref[...] + mixed

    ms = jnp.mean(x * x, axis=-1, keepdims=True)
    h = (x * lax.rsqrt(ms + EPS) * n2_ref[...]).astype(BF16)
    row = lax.broadcasted_iota(jnp.int32, (tm, FF_BLOCK), 0)
    acc = jnp.zeros(x.shape, F32)
    for cblk in range(d_ff // FF_BLOCK):
        sl = slice(cblk * FF_BLOCK, (cblk + 1) * FF_BLOCK)
        a = _dot(h, wup_ref[:, sl])
        gate = _dot(h, wup_ref[:, d_ff + cblk * FF_BLOCK:d_ff + (cblk + 1) * FF_BLOCK])
        prev2 = carry_ref[0:1, sl]
        prev1 = carry_ref[1:2, sl]
        a1 = jnp.where(row == 0, prev1, pltpu.roll(a, 1, 0))
        a2 = jnp.where(row == 0, prev2, jnp.where(row == 1, prev1, pltpu.roll(a, 2, 0)))
        conv = cb_ref[:, sl] + a2 * cw_ref[0:1, sl] + a1 * cw_ref[1:2, sl] + a * cw_ref[2:3, sl]
        act = conv * jax.nn.sigmoid(conv) * gate
        acc = acc + _dot(act.astype(BF16), wdn_ref[sl, :])
        carry_ref[:, sl] = a[tm - (CONV_W - 1):, :]
    x = x + acc
    x_out_ref[...] = x
    fc_ref[...] = carry_ref[...]
    if final:
        ms = jnp.mean(x * x, axis=-1, keepdims=True)
        y_ref[...] = x * lax.rsqrt(ms + EPS) * fn_ref[...]


def _merge_ffn(scal, x, oa, ob, oc, cg, hist, a_gain, c_gain, wo, n2, wup, cw, cb, wdn, fnorm, *, tm, final):
    b, t, d = x.shape
    d_ff = wdn.shape[0]
    kern = functools.partial(_merge_kernel, d_ff=d_ff, final=final)
    blk = lambda w: pl.BlockSpec((None, tm, w), lambda bi, ti: (bi, ti, 0))
    per_b = pl.BlockSpec((None, CONV_W - 1, d_ff), lambda bi, ti: (bi, 0, 0))
    out_specs = [blk(d), per_b]
    out_shape = [jax.ShapeDtypeStruct((b, t, d), F32), jax.ShapeDtypeStruct((b, CONV_W - 1, d_ff), F32)]
    if final:
        out_specs.append(blk(d))
        out_shape.append(jax.ShapeDtypeStruct((b, t, d), F32))
    return pl.pallas_call(
        kern,
        grid=(b, t // tm),
        in_specs=[pl.BlockSpec(memory_space=pltpu.SMEM), blk(d), blk(WA), blk(WBP), blk(WBP), blk(WBP), per_b,
                  _const_spec(a_gain.shape), _const_spec(c_gain.shape), _const_spec(wo.shape),
                  _const_spec(n2.shape), _const_spec(wup.shape), _const_spec(cw.shape),
                  _const_spec(cb.shape), _const_spec(wdn.shape), _const_spec(fnorm.shape)],
        out_specs=out_specs,
        out_shape=out_shape,
        scratch_shapes=[pltpu.VMEM((CONV_W - 1, d_ff), F32)],
        compiler_params=_cparams(2),
        name="merge_ffn",
    )(scal, x, oa, ob, oc, cg, hist, a_gain, c_gain, wo, n2, wup, cw, cb, wdn, fnorm)


def _pack_w_in(w):
    parts = jnp.split(w, np.cumsum(IN_SIZES)[:-1].tolist(), axis=-1)
    cols = []
    for (name, width, padded), part in zip(SEGS, parts):
        if name == "ik":
            part = jnp.concatenate([part, part], axis=-1)
            width = 2 * D_I
        cols.append(jnp.pad(part, ((0, 0), (0, padded - width))))
    return jnp.concatenate(cols, axis=-1).astype(BF16)


def _pack_w_out(w):
    wa, wb, wc = w[:WA], w[WA:WA + WB], w[WA + WB:]
    pad = lambda m: jnp.pad(m, ((0, WBP - WB), (0, 0)))
    return jnp.concatenate([wa, pad(wb), pad(wc)], axis=0).astype(BF16)


def _rope_table(pos, dim, width):
    inv_freq = ROPE_THETA ** (-jnp.arange(0, dim, 2, dtype=F32) / dim)
    ang = pos.astype(F32)[:, None] * inv_freq[None, :]
    reps = width // (dim // 2)
    return jnp.tile(jnp.cos(ang), (1, reps)), jnp.tile(jnp.sin(ang), (1, reps))


def _cache_prep_kernel(c_ref, *o_refs, past):
    for layer, o_ref in enumerate(o_refs):
        x = c_ref[layer]
        n_feat = x.shape[0]
        if n_feat < LANES:
            x = jnp.concatenate([x] * (LANES // n_feat), axis=0)
        elif n_feat % LANES:
            x = jnp.concatenate([x, jnp.zeros((LANES - n_feat % LANES, past), x.dtype)], axis=0)
        for j in range(x.shape[0] // LANES):
            o_ref[0:past, j * LANES:(j + 1) * LANES] = x[j * LANES:(j + 1) * LANES].T.astype(BF16)
        o_ref[past:, :] = jnp.zeros((o_ref.shape[0] - past, o_ref.shape[1]), BF16)


def _cache_prep(cache, rows):
    depth, batch, past = cache.shape[:3]
    n_feat = int(np.prod(cache.shape[3:]))
    nd = cache.ndim
    c = jnp.transpose(cache, (0, 1) + tuple(range(3, nd)) + (2,)).reshape(depth, batch, n_feat, past)
    width = _round_up(n_feat, LANES)
    return pl.pallas_call(
        functools.partial(_cache_prep_kernel, past=past),
        grid=(batch,),
        in_specs=[pl.BlockSpec((depth, None, n_feat, past), lambda i: (0, i, 0, 0))],
        out_specs=[pl.BlockSpec((None, rows, width), lambda i: (i, 0, 0))] * depth,
        out_shape=[jax.ShapeDtypeStruct((batch, rows, width), BF16)] * depth,
        compiler_params=_cparams(1),
        name="cache_prep",
    )(c)


def _round_up(n, m):
    return (n + m - 1) // m * m


def kernel(x_prompt, x_sample, cache_a_k, cache_a_v, cache_b_k, cache_b_v, cache_b_kidx, state_c, state_ffn_conv, norm1, w_in, lam_q1, lam_k1, lam_q2, lam_k2, a_norm, c_lower, c_norm, w_out, norm2, ffn_up, ffn_conv_w, ffn_conv_b, ffn_down, final_norm):
    depth = w_in.shape[0]
    b_p, t_p, d = x_prompt.shape
    b_s, t_s, _ = x_sample.shape
    past = cache_a_k.shape[2]
    d_ff = ffn_down.shape[1]
    kv_s = past + t_s
    n_sel_p = min(TOPK_MAX, t_p // 4)
    n_sel_s = min(TOPK_MAX, kv_s // 4)
    tk = 256
    lk_s = _round_up(kv_s, tk)

    lb_soft = jax.nn.softmax(c_lower.astype(F32), axis=0)
    lower = jnp.cumsum(lb_soft, axis=0) - lb_soft[0]
    lower = jnp.pad(lower, ((0, 0), (0, WBP - WB)))

    tm_p = min(512, t_p)
    tm_s = min(256, b_s * t_s)
    pos_p = jnp.arange(t_p)
    pos_s = jnp.tile(past + jnp.arange(t_s), tm_s // t_s)
    tabs_p = _rope_table(pos_p, DA, WA) + _rope_table(pos_p, D_B, WBP)
    tabs_s = _rope_table(pos_s, DA, WA) + _rope_table(pos_s, D_B, WBP)

    past_kv = {name: _cache_prep(c, lk_s) for name, c in
               (("ak", cache_a_k), ("av", cache_a_v), ("bk", cache_b_k), ("bv", cache_b_v), ("ik", cache_b_kidx))}

    xp, xs = x_prompt, x_sample
    caches_p, state_p, conv_p = None, [], []
    outs_s = [[] for _ in range(7)]
    y_p = y_s = None
    fnorm = final_norm.reshape(1, d)
    for l in range(depth):
        lam_init = 0.8 - 0.6 * math.exp(-0.3 * l)
        lam = (jnp.exp(jnp.sum(lam_q1[l].astype(F32) * lam_k1[l].astype(F32)))
               - jnp.exp(jnp.sum(lam_q2[l].astype(F32) * lam_k2[l].astype(F32))) + lam_init)
        lam_arr = lam.reshape(1).astype(F32)
        scal = jnp.full((1,), 1.0 - lam_init, F32)
        w_pack = _pack_w_in(w_in[l])
        wo = _pack_w_out(w_out[l])
        gain1 = norm1[l].reshape(1, d)
        gain2 = norm2[l].reshape(1, d)
        lb_row = lower[l].reshape(1, WBP)
        a_gain = jnp.tile(a_norm[l], H_A).reshape(1, WA)
        c_gain = jnp.pad(jnp.tile(c_norm[l], H_C), (0, WBP - WB)).reshape(1, WBP)
        wup = ffn_up[l].astype(BF16)
        wdn = ffn_down[l].astype(BF16)
        cw = ffn_conv_w[l]
        cb = ffn_conv_b[l].reshape(1, d_ff)
        final = l == depth - 1

        u = _in_projection(xp.reshape(b_p * t_p, d), gain1, w_pack, lb_row, tabs_p, tm_p,
                           stacked=(l, depth, b_p, caches_p))
        caches_p = {name: u[name] for name in CACHE_OUTS}
        r3 = lambda a, b=b_p, t=t_p: a.reshape(b, t, a.shape[-1])
        oa = _attention_a(lam_arr, r3(u["aq"]), r3(u["ak16"]), r3(u["av16"]),
                          q_off=0, kv_len=t_p, tq=min(512, t_p), tk=min(512, t_p))
        ob = _attention_b(r3(u["bq"]), r3(u["iq"]), r3(u["iw"]), r3(u["bk16"]), r3(u["bv16"]), r3(u["ik2"]),
                          q_off=0, kv_len=t_p, tq=min(256, t_p), n_sel=n_sel_p)
        s0 = jnp.zeros((b_p, NPAIR, LANES, LANES), F32)
        oc, s_new = _hgrn2(r3(u["cq"]), r3(u["ck"]), r3(u["cgl"]), r3(u["cv"]), s0, c=CHUNK)
        hist0 = jnp.zeros((b_p, CONV_W - 1, d_ff), F32)
        res = _merge_ffn(scal, xp, oa, ob, oc, r3(u["cg"]), hist0, a_gain, c_gain, wo, gain2, wup, cw, cb,
                         wdn, fnorm, tm=min(512, t_p), final=final)
        xp, fc = res[0], res[1]
        if final:
            y_p = res[2]
        state_p.append(_pairs_to_state(s_new))
        conv_p.append(fc)

        u = _in_projection(xs.reshape(b_s * t_s, d), gain1, w_pack, lb_row, tabs_s, tm_s)
        r3 = lambda a, b=b_s, t=t_s: a.reshape(b, t, a.shape[-1])

        def cat(name, new):
            return lax.dynamic_update_slice(past_kv[name][l], r3(new), (0, past, 0))

        oa = _attention_a(lam_arr, r3(u["aq"]), cat("ak", u["ak16"]), cat("av", u["av16"]),
                          q_off=past, kv_len=kv_s, tq=t_s, tk=tk)
        ob = _attention_b(r3(u["bq"]), r3(u["iq"]), r3(u["iw"]), cat("bk", u["bk16"]),
                          cat("bv", u["bv16"]), cat("ik", u["ik2"]),
                          q_off=past, kv_len=kv_s, tq=t_s, n_sel=n_sel_s)
        oc, s_new = _hgrn2(r3(u["cq"]), r3(u["ck"]), r3(u["cgl"]), r3(u["cv"]), _state_to_pairs(state_c[l]),
                           c=t_s)
        res = _merge_ffn(scal, xs, oa, ob, oc, r3(u["cg"]), state_ffn_conv[l].astype(F32), a_gain, c_gain, wo,
                         gain2, wup, cw, cb, wdn, fnorm, tm=t_s, final=final)
        xs, fc = res[0], res[1]
        if final:
            y_s = res[2]
        for lst, val in zip(outs_s, (u["ak"].reshape(b_s, t_s, H_A, 2 * DA), u["av"].reshape(b_s, t_s, H_A, DV_A),
                                     u["bk"].reshape(b_s, t_s, H_B, D_B), u["bv"].reshape(b_s, t_s, H_B, D_B),
                                     u["ik"].reshape(b_s, t_s, D_I), _pairs_to_state(s_new), fc)):
            lst.append(val)

    def frames_major(c, heads):
        c = c.reshape(depth, b_p, heads, c.shape[2] // heads, t_p)
        return jnp.transpose(c, (0, 1, 4, 2, 3))

    outs_p = (frames_major(caches_p["ak"], H_A), frames_major(caches_p["av"], H_A),
              frames_major(caches_p["bk"], H_B), frames_major(caches_p["bv"], H_B),
              jnp.swapaxes(caches_p["ik"], 2, 3), jnp.stack(state_p), jnp.stack(conv_p))
    return (y_p, y_s) + outs_p + tuple(jnp.stack(v) for v in outs_s)
```
